```python
import math
import jax
import jax.numpy as jnp
from jax import lax
import numpy as np

D_MODEL = 2048
BATCH = 16
SEQ = 2048
DEPTH = 4
DEC_BATCH = 16
DEC_SEQ = 16
PAST_LEN = 2048

CHUNK = 64
SSD_EXPAND = 2
SSD_D_INNER = SSD_EXPAND * D_MODEL
SSD_HEAD_DIM = 64
SSD_HEADS = SSD_D_INNER // SSD_HEAD_DIM
SSD_GROUPS = 8
SSD_HPG = SSD_HEADS // SSD_GROUPS
SSD_D_STATE = 128
SSD_CONV = 4
SSD_CONV_DIM = SSD_D_INNER + 2 * SSD_GROUPS * SSD_D_STATE
DT_MIN = 0.001
DT_MAX = 0.1
GMLP_CHUNK = 128
GMLP_DIM = D_MODEL
GMLP_GROUPS = 8
GMLP_GROUP_DIM = GMLP_DIM // GMLP_GROUPS
N_BRANCH = 2
IN_DIM = SSD_D_INNER + SSD_CONV_DIM + SSD_HEADS + 2 * GMLP_DIM + N_BRANCH * D_MODEL
IN_SPLITS = [SSD_D_INNER, SSD_D_INNER + SSD_CONV_DIM, SSD_D_INNER + SSD_CONV_DIM + SSD_HEADS, SSD_D_INNER + SSD_CONV_DIM + SSD_HEADS + 2 * GMLP_DIM]
D_FF = 11 * D_MODEL // 4
N_EXPERTS = 8
TOP_K = 2
D_FF_EXPERT = 7 * D_MODEL // 2
MOE_BLOCK = 128
N_DENSE = (DEPTH + 1) // 2
N_MOE = DEPTH // 2
ALPHA = (2 * DEPTH) ** 0.25
BETA = (8 * DEPTH) ** -0.25
LN_EPS = 1e-5

kernel_name = 'hybrid_ssd_gmlp_streaming_encoder_step'


def layer_norm(x, g, b):
    xf = x.astype(jnp.float32)
    mu = jnp.mean(xf, axis=-1, keepdims=True)
    var = jnp.mean(jnp.square(xf - mu), axis=-1, keepdims=True)
    return ((xf - mu) * lax.rsqrt(var + LN_EPS) * g + b).astype(x.dtype)


def gated_group_rmsnorm(y, z, g):
    hz = y * jax.nn.silu(z.astype(jnp.float32))
    hg = hz.reshape(*hz.shape[:-1], SSD_GROUPS, -1)
    hg = hg * lax.rsqrt(jnp.mean(hg * hg, axis=-1, keepdims=True) + LN_EPS)
    return hg.reshape(hz.shape) * g


def causal_depthwise_conv(x, buf, w, b):
    l = x.shape[1]
    xp = jnp.concatenate([buf.astype(x.dtype), x], axis=1)
    out = b
    for k in range(SSD_CONV):
        out = out + xp[:, k:k + l] * w[k]
    return out, xp[:, l:]


def ssd_scan(xdt, a, bmat, cmat, state0):
    bsz, l = xdt.shape[:2]
    q = min(CHUNK, l)
    nc = l // q

    def to_chunks(t):
        return jnp.moveaxis(t.reshape(bsz, nc, q, *t.shape[2:]), 1, 0)

    mask = jnp.tril(jnp.ones((q, q), bool))[None, :, :, None, None]

    def step(state, inp):
        x_c, a_c, b_c, c_c = inp
        a_cum = jnp.cumsum(a_c, axis=1)
        seg = a_cum[:, :, None] - a_cum[:, None, :]
        decay_ls = jnp.where(mask, jnp.exp(jnp.where(mask, seg, 0.0)), 0.0)
        cb = jnp.einsum('blgn,bsgn->blsg', c_c, b_c)
        y = jnp.einsum('blsg,blsgh,bsghp->blghp', cb, decay_ls, x_c)
        y = y + jnp.einsum('blgn,bghpn->blghp', c_c, state) * jnp.exp(a_cum)[..., None]
        to_end = jnp.exp(a_cum[:, -1:] - a_cum)
        state = state * jnp.exp(a_cum[:, -1])[..., None, None] + jnp.einsum('blgn,blgh,blghp->bghpn', b_c, to_end, x_c)
        return state, y

    state, ys = lax.scan(step, state0, (to_chunks(xdt), to_chunks(a), to_chunks(bmat), to_chunks(cmat)))
    return jnp.moveaxis(ys, 0, 1).reshape(xdt.shape), state


def token_mixer(h, conv_buf, ssm_state, p, i):
    f32 = jnp.float32
    bsz, l, _ = h.shape
    z, xbc, dt, uv, gate = jnp.split(h @ p['w_in'][i], IN_SPLITS, axis=-1)
    xbc, new_conv = causal_depthwise_conv(xbc, conv_buf, p['conv_w'][i], p['conv_b'][i])
    xbc = jax.nn.silu(xbc)
    xs, bm, cm = jnp.split(xbc, [SSD_D_INNER, SSD_D_INNER + SSD_GROUPS * SSD_D_STATE], axis=-1)
    dt = jax.nn.softplus(dt.astype(f32) + p['dt_bias'][i].astype(f32))
    a = -jnp.exp(p['a_log'][i].astype(f32)) * dt
    grp = (bsz, l, SSD_GROUPS, SSD_HPG)
    xh = xs.astype(f32).reshape(bsz, l, SSD_GROUPS, SSD_HPG, SSD_HEAD_DIM)
    y, new_ssm = ssd_scan(xh * dt.reshape(grp)[..., None], a.reshape(grp),
                          bm.astype(f32).reshape(bsz, l, SSD_GROUPS, SSD_D_STATE),
                          cm.astype(f32).reshape(bsz, l, SSD_GROUPS, SSD_D_STATE),
                          ssm_state.astype(f32).reshape(bsz, SSD_GROUPS, SSD_HPG, SSD_HEAD_DIM, SSD_D_STATE))
    y = y + p['d_skip'][i].astype(f32).reshape(SSD_GROUPS, SSD_HPG, 1) * xh
    y_a = gated_group_rmsnorm(y.reshape(bsz, l, SSD_D_INNER), z, p['ssd_norm_g'][i]).astype(h.dtype)
    new_ssm = new_ssm.reshape(ssm_state.shape).astype(ssm_state.dtype)
    u, v = jnp.split(jax.nn.gelu(uv, approximate=False), 2, axis=-1)
    v = layer_norm(v, p['gmlp_ln_g'][i], p['gmlp_ln_b'][i])
    q = min(GMLP_CHUNK, l)
    ws = jnp.tril(p['w_s'][i][:, :q, :q])
    bs = jnp.swapaxes(p['b_s'][i][:, :q], 0, 1)[:, :, None]
    vc = v.reshape(bsz, l // q, q, GMLP_GROUPS, GMLP_GROUP_DIM)
    s = jnp.einsum('gts,bcsgd->bctgd', ws, vc) + bs
    y_b = u * s.reshape(bsz, l, GMLP_DIM)
    g_a, g_b = jnp.split(jax.nn.sigmoid(gate + p['b_gate'][i]), 2, axis=-1)
    merged = g_a * (y_a @ p['w_br_a'][i]) + g_b * (y_b @ p['w_br_b'][i])
    return merged @ p['w_o'][i], new_conv, new_ssm, v


def swiglu(h, wg, wu, wd):
    return (jax.nn.silu(h @ wg) * (h @ wu)) @ wd


def moe_swiglu(h, w_router, wg, wu, wd):
    shp = h.shape
    x = h.reshape(-1, D_MODEL)
    t = x.shape[0]
    logits = (x @ w_router).astype(jnp.float32)
    top_v, top_i = lax.top_k(logits, TOP_K)
    top_w = jax.nn.softmax(top_v, axis=-1)
    e_flat = top_i.reshape(-1)
    order = jnp.argsort(e_flat)
    e_sorted = e_flat[order]
    tok_sorted = (order // TOP_K).astype(jnp.int32)
    w_sorted = top_w.reshape(-1)[order]
    counts = jnp.bincount(e_flat, length=N_EXPERTS)
    padded = (counts + MOE_BLOCK - 1) // MOE_BLOCK * MOE_BLOCK
    pad_end = jnp.cumsum(padded)
    pad_start = pad_end - padded
    start = jnp.cumsum(counts) - counts
    dest = pad_start[e_sorted] + jnp.arange(t * TOP_K) - start[e_sorted]
    n_blocks = -(-(t * TOP_K) // MOE_BLOCK) + N_EXPERTS
    cap = n_blocks * MOE_BLOCK
    slot_tok = jnp.full((cap,), t, jnp.int32).at[dest].set(tok_sorted)
    slot_w = jnp.zeros((cap,), x.dtype).at[dest].set(w_sorted.astype(x.dtype))
    block_expert = jnp.minimum(jnp.searchsorted(pad_end, jnp.arange(n_blocks) * MOE_BLOCK, side='right'), N_EXPERTS - 1)
    x_pad = jnp.concatenate([x, jnp.zeros((1, D_MODEL), x.dtype)], axis=0)

    def run_block(args):
        toks, e = args
        return swiglu(x_pad[toks], wg[e], wu[e], wd[e])

    out = lax.map(run_block, (slot_tok.reshape(n_blocks, MOE_BLOCK), block_expert))
    y = jnp.zeros((t + 1, D_MODEL), x.dtype).at[slot_tok].add(out.reshape(cap, D_MODEL) * slot_w[:, None])
    return y[:t].reshape(shp)


def trunk(x, conv_bufs, ssm_states, p):
    x = layer_norm(x, p['ln_in_g'], p['ln_in_b'])
    convs, ssms, vs = [], [], []
    for i in range(DEPTH):
        mix, conv_i, ssm_i, v_i = token_mixer(x, conv_bufs[i], ssm_states[i], p, i)
        x = layer_norm(ALPHA * x + mix, p['ln1_g'][i], p['ln1_b'][i])
        j = i // 2
        if i % 2 == 0:
            ffn = swiglu(x, p['w_ff_gate'][j], p['w_ff_up'][j], p['w_ff_down'][j])
        else:
            ffn = moe_swiglu(x, p['w_router'][j], p['w_moe_gate'][j], p['w_moe_up'][j], p['w_moe_down'][j])
        x = layer_norm(ALPHA * x + ffn, p['ln2_g'][i], p['ln2_b'][i])
        convs.append(conv_i)
        ssms.append(ssm_i)
        vs.append(v_i)
    return x, jnp.stack(convs), jnp.stack(ssms), jnp.stack(vs)


def setup_inputs(seed: int = 0) -> dict:
    key = jax.random.key(seed)
    ks = iter(jax.random.split(key, 48))
    f32 = jnp.float32

    def nrm(shape, scale):
        return jax.random.normal(next(ks), shape, f32) * scale

    def gain(shape):
        return 1.0 + nrm(shape, 0.02)

    dt_u = jax.random.uniform(next(ks), (DEPTH, SSD_HEADS), f32)
    dt0 = jnp.exp(dt_u * (math.log(DT_MAX) - math.log(DT_MIN)) + math.log(DT_MIN))
    dt_bias = dt0 + jnp.log(-jnp.expm1(-dt0))
    a_log = jnp.log(jax.random.uniform(next(ks), (DEPTH, SSD_HEADS), f32, 1.0, 16.0))
    return {
        'x_prompt': nrm((BATCH, SEQ, D_MODEL), 1.0),
        'x_sample': nrm((DEC_BATCH, DEC_SEQ, D_MODEL), 1.0),
        'cache_conv': nrm((DEPTH, DEC_BATCH, SSD_CONV - 1, SSD_CONV_DIM), 1.0),
        'state_ssm': nrm((DEPTH, DEC_BATCH, SSD_HEADS, SSD_HEAD_DIM, SSD_D_STATE), 0.05),
        'ln_in_g': gain((D_MODEL,)),
        'ln_in_b': nrm((D_MODEL,), 0.02),
        'w_in': nrm((DEPTH, D_MODEL, IN_DIM), D_MODEL ** -0.5),
        'conv_w': nrm((DEPTH, SSD_CONV, SSD_CONV_DIM), 0.5),
        'conv_b': nrm((DEPTH, SSD_CONV_DIM), 0.02),
        'dt_bias': dt_bias,
        'a_log': a_log,
        'd_skip': 1.0 + nrm((DEPTH, SSD_HEADS), 0.1),
        'ssd_norm_g': gain((DEPTH, SSD_D_INNER)),
        'gmlp_ln_g': gain((DEPTH, GMLP_DIM)),
        'gmlp_ln_b': nrm((DEPTH, GMLP_DIM), 0.02),
        'w_s': nrm((DEPTH, GMLP_GROUPS, GMLP_CHUNK, GMLP_CHUNK), GMLP_CHUNK ** -0.5),
        'b_s': gain((DEPTH, GMLP_GROUPS, GMLP_CHUNK)),
        'b_gate': nrm((DEPTH, N_BRANCH * D_MODEL), 0.02),
        'w_br_a': nrm((DEPTH, SSD_D_INNER, D_MODEL), BETA * SSD_D_INNER ** -0.5),
        'w_br_b': nrm((DEPTH, GMLP_DIM, D_MODEL), BETA * GMLP_DIM ** -0.5),
        'w_o': nrm((DEPTH, D_MODEL, D_MODEL), BETA * D_MODEL ** -0.5),
        'ln1_g': gain((DEPTH, D_MODEL)),
        'ln1_b': nrm((DEPTH, D_MODEL), 0.02),
        'w_ff_gate': nrm((N_DENSE, D_MODEL, D_FF), D_MODEL ** -0.5),
        'w_ff_up': nrm((N_DENSE, D_MODEL, D_FF), D_MODEL ** -0.5),
        'w_ff_down': nrm((N_DENSE, D_FF, D_MODEL), BETA * D_FF ** -0.5),
        'w_router': nrm((N_MOE, D_MODEL, N_EXPERTS), D_MODEL ** -0.5),
        'w_moe_gate': nrm((N_MOE, N_EXPERTS, D_MODEL, D_FF_EXPERT), D_MODEL ** -0.5),
        'w_moe_up': nrm((N_MOE, N_EXPERTS, D_MODEL, D_FF_EXPERT), D_MODEL ** -0.5),
        'w_moe_down': nrm((N_MOE, N_EXPERTS, D_FF_EXPERT, D_MODEL), BETA * D_FF_EXPERT ** -0.5),
        'ln2_g': gain((DEPTH, D_MODEL)),
        'ln2_b': nrm((DEPTH, D_MODEL), 0.02),
    }


def reference(x_prompt, x_sample, cache_conv, state_ssm, ln_in_g, ln_in_b, w_in, conv_w, conv_b,
              dt_bias, a_log, d_skip, ssd_norm_g, gmlp_ln_g, gmlp_ln_b, w_s, b_s, b_gate,
              w_br_a, w_br_b, w_o, ln1_g, ln1_b, w_ff_gate, w_ff_up, w_ff_down,
              w_router, w_moe_gate, w_moe_up, w_moe_down, ln2_g, ln2_b):
    p = dict(ln_in_g=ln_in_g, ln_in_b=ln_in_b, w_in=w_in, conv_w=conv_w, conv_b=conv_b,
             dt_bias=dt_bias, a_log=a_log, d_skip=d_skip, ssd_norm_g=ssd_norm_g,
             gmlp_ln_g=gmlp_ln_g, gmlp_ln_b=gmlp_ln_b, w_s=w_s, b_s=b_s, b_gate=b_gate,
             w_br_a=w_br_a, w_br_b=w_br_b, w_o=w_o, ln1_g=ln1_g, ln1_b=ln1_b,
             w_ff_gate=w_ff_gate, w_ff_up=w_ff_up, w_ff_down=w_ff_down, w_router=w_router,
             w_moe_gate=w_moe_gate, w_moe_up=w_moe_up, w_moe_down=w_moe_down,
             ln2_g=ln2_g, ln2_b=ln2_b)
    bp = x_prompt.shape[0]
    zero_conv = jnp.zeros((DEPTH, bp, SSD_CONV - 1, SSD_CONV_DIM), x_prompt.dtype)
    zero_ssm = jnp.zeros((DEPTH, bp, SSD_HEADS, SSD_HEAD_DIM, SSD_D_STATE), state_ssm.dtype)
    y_prompt, prompt_conv, prompt_ssm, _ = trunk(x_prompt, zero_conv, zero_ssm, p)
    y_sample, sample_conv, sample_ssm, sample_gmlp_v = trunk(x_sample, cache_conv, state_ssm, p)
    return (y_prompt, y_sample, prompt_conv, prompt_ssm, sample_conv, sample_ssm, sample_gmlp_v)
```

```python
import functools

import jax
import jax.numpy as jnp
from jax import lax
from jax.experimental import pallas as pl
from jax.experimental.pallas import tpu as pltpu

F32 = jnp.float32
BF16 = jnp.bfloat16
I32 = jnp.int32
HIGHEST = lax.Precision.HIGHEST

LN_EPS = 1e-5
LANE = 128
SUBLANE = 8
V7X_VMEM_BYTES = 64 * 1024 * 1024
VMEM_LIMIT_BYTES = V7X_VMEM_BYTES - 8 * 1024 * 1024
MOE_TOP_K = 2
_NEG_BIG = -1e30
_TILE_CANDIDATES = (1024, 512, 256, 128, 64, 32, 16, 8)


def _pick(n, cap):
    for c in _TILE_CANDIDATES:
        if c <= cap and n % c == 0:
            return c
    raise ValueError(f"no tile for {n}")


def _params(*sem):
    return pltpu.CompilerParams(dimension_semantics=sem, vmem_limit_bytes=VMEM_LIMIT_BYTES)


def _ln_rows(x, g, b):
    mu = jnp.mean(x, axis=-1, keepdims=True)
    xc = x - mu
    var = jnp.mean(xc * xc, axis=-1, keepdims=True)
    return xc * lax.rsqrt(var + LN_EPS) * g + b


def _silu(x):
    return x * jax.nn.sigmoid(x)


def _gelu(x):
    return 0.5 * x * (1.0 + lax.erf(x * (2.0 ** -0.5)))


def _dot(a, b):
    return jnp.dot(a, b, preferred_element_type=F32)


def _dot_exact(a, b):
    return jnp.dot(a, b, precision=HIGHEST, preferred_element_type=F32)


def _ln_kernel(x_ref, g_ref, b_ref, o32_ref, o16_ref):
    y = _ln_rows(x_ref[...], g_ref[...], b_ref[...])
    o32_ref[...] = y
    o16_ref[...] = y.astype(BF16)


def _ln_call(x, g, b):
    t, d = x.shape
    tm = _pick(t, 512)
    row = pl.BlockSpec((tm, d), lambda i: (i, 0))
    vec = pl.BlockSpec((1, d), lambda i: (0, 0))
    return pl.pallas_call(
        _ln_kernel, grid=(t // tm,), in_specs=[row, vec, vec], out_specs=[row, row],
        out_shape=[jax.ShapeDtypeStruct((t, d), F32), jax.ShapeDtypeStruct((t, d), BF16)],
        compiler_params=_params("parallel"), name="ln_in",
    )(x, g.reshape(1, d), b.reshape(1, d))


def _mm_kernel(x_ref, w_ref, o_ref):
    o_ref[...] = _dot(x_ref[...], w_ref[...]).astype(o_ref.dtype)


def _mm_call(x, w, out_dtype, name):
    t, k = x.shape
    n = w.shape[1]
    tm, tn = _pick(t, 1024), _pick(n, 1024)
    return pl.pallas_call(
        _mm_kernel, grid=(t // tm, n // tn),
        in_specs=[pl.BlockSpec((tm, k), lambda i, j: (i, 0)), pl.BlockSpec((k, tn), lambda i, j: (0, j))],
        out_specs=pl.BlockSpec((tm, tn), lambda i, j: (i, j)),
        out_shape=jax.ShapeDtypeStruct((t, n), out_dtype),
        compiler_params=_params("parallel", "parallel"), name=name,
    )(x, w)


def _dt_kernel(x_ref, w_ref, b_ref, o_ref):
    raw = _dot_exact(x_ref[...], w_ref[...]) + b_ref[...]
    o_ref[...] = jnp.maximum(raw, 0.0) + jnp.log1p(jnp.exp(-jnp.abs(raw)))


def _dt_call(x, w, bias):
    t, k = x.shape
    n = w.shape[1]
    tm = _pick(t, 512)
    return pl.pallas_call(
        _dt_kernel, grid=(t // tm,),
        in_specs=[pl.BlockSpec((tm, k), lambda i: (i, 0)), pl.BlockSpec((k, n), lambda i: (0, 0)),
                  pl.BlockSpec((1, n), lambda i: (0, 0))],
        out_specs=pl.BlockSpec((tm, n), lambda i: (i, 0)),
        out_shape=jax.ShapeDtypeStruct((t, n), F32),
        compiler_params=_params("parallel"), name="dt_proj",
    )(x, w, bias)


_CONV_PAD = SUBLANE


def _ssd_kernel(xs_ref, bm_ref, cm_ref, z_ref, dt_ref, dtt_ref, cbx_ref, cbb_ref, cbc_ref, s0_ref,
                cwx_ref, cwb_ref, cwc_ref, ccx_ref, ccb_ref, ccc_ref, alr_ref, alc_ref, dsk_ref, ng_ref,
                y_ref, sout_ref, xpad_ref, st_ref, *, q, hpg, p, n, kc, nc):
    c = pl.program_id(2)
    w = hpg * p
    tail = kc - 1
    t0 = _CONV_PAD - tail

    @pl.when(c == 0)
    def _():
        xpad_ref[t0:_CONV_PAD, 0:w] = cbx_ref[...]
        xpad_ref[t0:_CONV_PAD, w:w + n] = cbb_ref[...]
        xpad_ref[t0:_CONV_PAD, w + n:w + 2 * n] = cbc_ref[...]
        st_ref[...] = s0_ref[...].reshape(w, n).T

    xpad_ref[_CONV_PAD:_CONV_PAD + q, 0:w] = xs_ref[...]
    xpad_ref[_CONV_PAD:_CONV_PAD + q, w:w + n] = bm_ref[...]
    xpad_ref[_CONV_PAD:_CONV_PAD + q, w + n:w + 2 * n] = cm_ref[...]

    def conv(lo, hi, cw_ref, cc_ref):
        acc = cc_ref[...]
        for k in range(kc):
            acc = acc + xpad_ref[t0 + k:t0 + k + q, lo:hi] * cw_ref[k:k + 1, :]
        return _silu(acc)

    xs = conv(0, w, cwx_ref, ccx_ref)
    bm = conv(w, w + n, cwb_ref, ccb_ref)
    cm = conv(w + n, w + 2 * n, cwc_ref, ccc_ref)
    xpad_ref[t0:_CONV_PAD, :] = xpad_ref[t0 + q:_CONV_PAD + q, :]

    a = dt_ref[...] * (-jnp.exp(alr_ref[...]))
    a_t = dtt_ref[...] * (-jnp.exp(alc_ref[...]))
    ri = lax.broadcasted_iota(I32, (q, q), 0)
    ci = lax.broadcasted_iota(I32, (q, q), 1)
    tril = ri >= ci
    acol = _dot_exact(tril.astype(F32), a)
    arow = _dot_exact(a_t, (ri <= ci).astype(F32))

    hrow = lax.broadcasted_iota(I32, (hpg, w), 0)
    hlane = lax.broadcasted_iota(I32, (hpg, w), 1)
    expand = ((hlane >= hrow * p) & (hlane < (hrow + 1) * p)).astype(F32)
    xdt = xs * _dot_exact(dt_ref[...], expand)
    eacol = _dot_exact(jnp.exp(acol), expand)
    to_end = _dot_exact(jnp.exp(acol[q - 1:q, :] - acol), expand)

    bm16 = bm.astype(BF16)
    cm16 = cm.astype(BF16)
    cb = lax.dot_general(cm16, bm16, (((1,), (1,)), ((), ())), preferred_element_type=F32)
    st = st_ref[...]
    y = _dot(cm16, st.astype(BF16)) * eacol
    lane = lax.broadcasted_iota(I32, (q, w), 1)
    for h in range(hpg):
        seg = acol[:, h:h + 1] - arow[h:h + 1, :]
        m = (cb * jnp.exp(jnp.where(tril, seg, _NEG_BIG))).astype(BF16)
        head = (lane >= h * p) & (lane < (h + 1) * p)
        y = y + _dot(m, jnp.where(head, xdt, 0.0).astype(BF16))
    y = y + dsk_ref[...] * xs
    hz = y * _silu(z_ref[...])
    y_ref[...] = (hz * lax.rsqrt(jnp.mean(hz * hz, axis=-1, keepdims=True) + LN_EPS) * ng_ref[...]).astype(BF16)

    upd = lax.dot_general(bm16, (xdt * to_end).astype(BF16), (((0,), (0,)), ((), ())),
                          preferred_element_type=F32)
    st_ref[...] = st * eacol[q - 1:q, :] + upd

    @pl.when(c == nc - 1)
    def _():
        sout_ref[...] = st_ref[...].T.reshape(hpg, p, n)


def _ssd_call(proj, dt, conv_buf, state0, conv_w, conv_b, a_log, d_skip, norm_g, *, bsz, seq, d_inner, off_z,
              off_xbc, q):
    heads, p, n = state0.shape[1:]
    cdim = conv_w.shape[1]
    kc = conv_w.shape[0]
    groups = (cdim - d_inner) // (2 * n)
    hpg = heads // groups
    w = hpg * p
    nc = seq // q
    t = bsz * seq
    assert w * groups == d_inner and seq % q == 0 and q >= kc - 1
    assert off_z % w == 0 and off_xbc % w == 0 and (off_xbc + d_inner) % n == 0

    dt4 = dt[:, :heads].reshape(bsz, seq, groups, hpg).transpose(0, 2, 1, 3)
    dtt4 = dt4.transpose(0, 1, 3, 2)
    alr = a_log.reshape(groups, 1, hpg)
    alc = a_log.reshape(groups, hpg, 1)
    dsk = jnp.repeat(d_skip, p).reshape(1, d_inner)
    ng = norm_g.reshape(1, d_inner)
    ccb = conv_b.reshape(1, cdim)

    zb, xb = off_z // w, off_xbc // w
    bb, cb_ = (off_xbc + d_inner) // n, (off_xbc + d_inner + groups * n) // n
    row = lambda b, g, c: b * nc + c
    in_specs = [
        pl.BlockSpec((q, w), lambda b, g, c: (row(b, g, c), xb + g)),
        pl.BlockSpec((q, n), lambda b, g, c: (row(b, g, c), bb + g)),
        pl.BlockSpec((q, n), lambda b, g, c: (row(b, g, c), cb_ + g)),
        pl.BlockSpec((q, w), lambda b, g, c: (row(b, g, c), zb + g)),
        pl.BlockSpec((None, None, q, hpg), lambda b, g, c: (b, g, c, 0)),
        pl.BlockSpec((None, None, hpg, q), lambda b, g, c: (b, g, 0, c)),
        pl.BlockSpec((None, kc - 1, w), lambda b, g, c: (b, 0, g)),
        pl.BlockSpec((None, kc - 1, n), lambda b, g, c: (b, 0, d_inner // n + g)),
        pl.BlockSpec((None, kc - 1, n), lambda b, g, c: (b, 0, d_inner // n + groups + g)),
        pl.BlockSpec((None, hpg, p, n), lambda b, g, c: (b, g, 0, 0)),
        pl.BlockSpec((kc, w), lambda b, g, c: (0, g)),
        pl.BlockSpec((kc, n), lambda b, g, c: (0, d_inner // n + g)),
        pl.BlockSpec((kc, n), lambda b, g, c: (0, d_inner // n + groups + g)),
        pl.BlockSpec((1, w), lambda b, g, c: (0, g)),
        pl.BlockSpec((1, n), lambda b, g, c: (0, d_inner // n + g)),
        pl.BlockSpec((1, n), lambda b, g, c: (0, d_inner // n + groups + g)),
        pl.BlockSpec((None, 1, hpg), lambda b, g, c: (g, 0, 0)),
        pl.BlockSpec((None, hpg, 1), lambda b, g, c: (g, 0, 0)),
        pl.BlockSpec((1, w), lambda b, g, c: (0, g)),
        pl.BlockSpec((1, w), lambda b, g, c: (0, g)),
    ]
    out_specs = [
        pl.BlockSpec((q, w), lambda b, g, c: (row(b, g, c), g)),
        pl.BlockSpec((None, hpg, p, n), lambda b, g, c: (b, g, 0, 0)),
    ]
    kern = functools.partial(_ssd_kernel, q=q, hpg=hpg, p=p, n=n, kc=kc, nc=nc)
    return pl.pallas_call(
        kern, grid=(bsz, groups, nc), in_specs=in_specs, out_specs=out_specs,
        out_shape=[jax.ShapeDtypeStruct((t, d_inner), BF16), jax.ShapeDtypeStruct(state0.shape, F32)],
        scratch_shapes=[pltpu.VMEM((_CONV_PAD + q, w + 2 * n), F32), pltpu.VMEM((n, w), F32)],
        compiler_params=_params("parallel", "parallel", "arbitrary"), name="ssd",
    )(proj, proj, proj, proj, dt4, dtt4, conv_buf, conv_buf, conv_buf, state0,
      conv_w, conv_w, conv_w, ccb, ccb, ccb, alr, alc, dsk, ng)


def _gmlp_kernel(u_ref, v_ref, lg_ref, lb_ref, ws_ref, bs_ref, yb_ref, vn_ref, *, groups, q):
    u = _gelu(u_ref[...])
    vn = _ln_rows(_gelu(v_ref[...]), lg_ref[...], lb_ref[...])
    vn_ref[...] = vn
    vn16 = vn.astype(BF16)
    d = vn.shape[1] // groups
    ri = lax.broadcasted_iota(I32, (q, q), 0)
    ci = lax.broadcasted_iota(I32, (q, q), 1)
    for g in range(groups):
        wg = jnp.where(ri >= ci, ws_ref[g], 0.0).astype(BF16)
        s = _dot(wg, vn16[:, g * d:(g + 1) * d]) + bs_ref[g]
        yb_ref[:, g * d:(g + 1) * d] = (u[:, g * d:(g + 1) * d] * s).astype(BF16)


def _gmlp_call(proj, ln_g, ln_b, w_s, b_s, *, bsz, seq, gd, off_u, q):
    groups = w_s.shape[0]
    t = bsz * seq
    assert off_u % gd == 0 and seq % q == 0 and (gd // groups) % LANE == 0
    ub = off_u // gd
    ws = w_s[:, :q, :q]
    bs = b_s[:, :q, None]
    row = pl.BlockSpec((q, gd), lambda i: (i, 0))
    kern = functools.partial(_gmlp_kernel, groups=groups, q=q)
    return pl.pallas_call(
        kern, grid=(t // q,),
        in_specs=[pl.BlockSpec((q, gd), lambda i: (i, ub)), pl.BlockSpec((q, gd), lambda i: (i, ub + 1)),
                  pl.BlockSpec((1, gd), lambda i: (0, 0)), pl.BlockSpec((1, gd), lambda i: (0, 0)),
                  pl.BlockSpec((groups, q, q), lambda i: (0, 0, 0)),
                  pl.BlockSpec((groups, q, 1), lambda i: (0, 0, 0))],
        out_specs=[row, row],
        out_shape=[jax.ShapeDtypeStruct((t, gd), BF16), jax.ShapeDtypeStruct((t, gd), F32)],
        compiler_params=_params("parallel"), name="gmlp",
    )(proj, proj, ln_g.reshape(1, gd), ln_b.reshape(1, gd), ws, bs)


def _merge_kernel(ya_ref, yb_ref, wa_ref, wb_ref, ga_ref, gb_ref, bga_ref, bgb_ref, o_ref):
    a = _dot(ya_ref[...], wa_ref[...])
    b = _dot(yb_ref[...], wb_ref[...])
    o = jax.nn.sigmoid(ga_ref[...] + bga_ref[...]) * a + jax.nn.sigmoid(gb_ref[...] + bgb_ref[...]) * b
    o_ref[...] = o.astype(o_ref.dtype)


def _merge_call(ya, yb, proj, w_a, w_b, b_gate, *, off_gate):
    t, ka = ya.shape
    kb = yb.shape[1]
    d = w_a.shape[1]
    tm, tn = _pick(t, 512), _pick(d, 512)
    assert off_gate % tn == 0
    gb0 = off_gate // tn
    nd = d // tn
    bg = b_gate.reshape(1, 2 * d)
    return pl.pallas_call(
        _merge_kernel, grid=(t // tm, nd),
        in_specs=[pl.BlockSpec((tm, ka), lambda i, j: (i, 0)), pl.BlockSpec((tm, kb), lambda i, j: (i, 0)),
                  pl.BlockSpec((ka, tn), lambda i, j: (0, j)), pl.BlockSpec((kb, tn), lambda i, j: (0, j)),
                  pl.BlockSpec((tm, tn), lambda i, j: (i, gb0 + j)),
                  pl.BlockSpec((tm, tn), lambda i, j: (i, gb0 + nd + j)),
                  pl.BlockSpec((1, tn), lambda i, j: (0, j)), pl.BlockSpec((1, tn), lambda i, j: (0, nd + j))],
        out_specs=pl.BlockSpec((tm, tn), lambda i, j: (i, j)),
        out_shape=jax.ShapeDtypeStruct((t, d), BF16),
        compiler_params=_params("parallel", "parallel"), name="merge",
    )(ya, yb, w_a, w_b, proj, proj, bg, bg)


def _mm_res_ln_kernel(a_ref, w_ref, x_ref, g_ref, b_ref, o32_ref, o16_ref, acc_ref, *, alpha, nk):
    k = pl.program_id(1)

    @pl.when(k == 0)
    def _():
        acc_ref[...] = jnp.zeros_like(acc_ref)

    acc_ref[...] += _dot(a_ref[...], w_ref[...])

    @pl.when(k == nk - 1)
    def _():
        y = _ln_rows(alpha * x_ref[...] + acc_ref[...], g_ref[...], b_ref[...])
        o32_ref[...] = y
        o16_ref[...] = y.astype(BF16)


def _k_tile(k, cap):
    if k % LANE != 0:
        return k
    best = LANE
    for m in range(1, k // LANE + 1):
        tk = m * LANE
        if k % tk == 0 and tk <= cap:
            best = tk
    return best


def _mm_res_ln_call(a, w, x, g, b, alpha, name):
    t, k = a.shape
    d = w.shape[1]
    tm = _pick(t, 512)
    tk = _k_tile(k, 2048)
    nk = k // tk
    row = pl.BlockSpec((tm, d), lambda i, kk: (i, 0))
    vec = pl.BlockSpec((1, d), lambda i, kk: (0, 0))
    kern = functools.partial(_mm_res_ln_kernel, alpha=alpha, nk=nk)
    return pl.pallas_call(
        kern, grid=(t // tm, nk),
        in_specs=[pl.BlockSpec((tm, tk), lambda i, kk: (i, kk)), pl.BlockSpec((tk, d), lambda i, kk: (kk, 0)),
                  row, vec, vec],
        out_specs=[row, row],
        out_shape=[jax.ShapeDtypeStruct((t, d), F32), jax.ShapeDtypeStruct((t, d), BF16)],
        scratch_shapes=[pltpu.VMEM((tm, d), F32)],
        compiler_params=_params("parallel", "arbitrary"), name=name,
    )(a, w, x, g.reshape(1, d), b.reshape(1, d))


def _swiglu_up_kernel(meta_ref, x_ref, wg_ref, wu_ref, o_ref):
    live = pl.program_id(0) < meta_ref[0]

    @pl.when(live)
    def _():
        x = x_ref[...].astype(BF16)
        g = _dot(x, wg_ref[...])
        u = _dot(x, wu_ref[...])
        o_ref[...] = (_silu(g) * u).astype(o_ref.dtype)

    @pl.when(jnp.logical_not(live))
    def _():
        o_ref[...] = jnp.zeros_like(o_ref)


def _swiglu_up_call(meta, x, wg, wu, tm, name):
    t, k = x.shape
    n = wg.shape[2]
    tn = _pick(n, 512)
    nn = n // tn

    def live(i, m):
        return jnp.minimum(i, m[0] - 1)

    def col(i, j, m):
        return jnp.where(i < m[0], j, nn - 1)

    wspec = pl.BlockSpec((None, k, tn), lambda i, j, m: (m[1 + live(i, m)], 0, col(i, j, m)))
    grid_spec = pltpu.PrefetchScalarGridSpec(
        num_scalar_prefetch=1, grid=(t // tm, nn),
        in_specs=[pl.BlockSpec((tm, k), lambda i, j, m: (live(i, m), 0)), wspec, wspec],
        out_specs=pl.BlockSpec((tm, tn), lambda i, j, m: (i, j)))
    return pl.pallas_call(
        _swiglu_up_kernel, grid_spec=grid_spec, out_shape=jax.ShapeDtypeStruct((t, n), BF16),
        compiler_params=_params("arbitrary", "arbitrary"), name=name,
    )(meta, x, wg, wu)


def _moe_down_kernel(meta_ref, h_ref, w_ref, o_ref, acc_ref, *, nk):
    k = pl.program_id(1)
    live = pl.program_id(0) < meta_ref[0]

    @pl.when(live)
    def _():
        @pl.when(k == 0)
        def _():
            acc_ref[...] = jnp.zeros_like(acc_ref)

        acc_ref[...] += _dot(h_ref[...], w_ref[...])

        @pl.when(k == nk - 1)
        def _():
            o_ref[...] = acc_ref[...]

    @pl.when(jnp.logical_not(live) & (k == nk - 1))
    def _():
        o_ref[...] = jnp.zeros_like(o_ref)


def _moe_down_call(meta, h, wd, tm):
    t, k = h.shape
    d = wd.shape[2]
    tk = _k_tile(k, 2048)
    nk = k // tk

    def live(i, m):
        return jnp.minimum(i, m[0] - 1)

    def kk(i, k_, m):
        return jnp.where(i < m[0], k_, nk - 1)

    grid_spec = pltpu.PrefetchScalarGridSpec(
        num_scalar_prefetch=1, grid=(t // tm, nk),
        in_specs=[pl.BlockSpec((tm, tk), lambda i, k_, m: (live(i, m), kk(i, k_, m))),
                  pl.BlockSpec((None, tk, d), lambda i, k_, m: (m[1 + live(i, m)], kk(i, k_, m), 0))],
        out_specs=pl.BlockSpec((tm, d), lambda i, k_, m: (i, 0)),
        scratch_shapes=[pltpu.VMEM((tm, d), F32)])
    return pl.pallas_call(
        functools.partial(_moe_down_kernel, nk=nk), grid_spec=grid_spec,
        out_shape=jax.ShapeDtypeStruct((t, d), F32),
        compiler_params=_params("arbitrary", "arbitrary"), name="moe_down",
    )(meta, h, wd)


def _router_kernel(x_ref, wr_ref, e_ref, w_ref, r_ref, cnt_ref, carry_ref, *, n_exp, tm):
    @pl.when(pl.program_id(0) == 0)
    def _():
        carry_ref[...] = jnp.zeros_like(carry_ref)

    logits = _dot_exact(x_ref[...], wr_ref[...])
    lane = lax.broadcasted_iota(I32, logits.shape, 1)
    lg = jnp.where(lane < n_exp, logits, -jnp.inf)
    m1 = jnp.max(lg, axis=-1, keepdims=True)
    i1 = jnp.min(jnp.where(lg == m1, lane, LANE), axis=-1, keepdims=True)
    lg2 = jnp.where(lane == i1, -jnp.inf, lg)
    m2 = jnp.max(lg2, axis=-1, keepdims=True)
    i2 = jnp.min(jnp.where(lg2 == m2, lane, LANE), axis=-1, keepdims=True)
    ex = jnp.exp(m2 - m1)
    w1 = 1.0 / (1.0 + ex)
    w2 = ex / (1.0 + ex)

    oh1 = (lane == i1).astype(F32)
    oh2 = (lane == i2).astype(F32)
    both = oh1 + oh2
    ri = lax.broadcasted_iota(I32, (tm, tm), 0)
    ci = lax.broadcasted_iota(I32, (tm, tm), 1)
    before = _dot((ri > ci).astype(BF16), both.astype(BF16)) + carry_ref[0:1, :]
    r1 = jnp.sum(before * oh1, axis=-1, keepdims=True)
    r2 = jnp.sum(before * oh2, axis=-1, keepdims=True)
    carry_ref[...] = carry_ref[...] + jnp.sum(both, axis=0, keepdims=True)

    e_ref[...] = jnp.where(lane == 0, i1, jnp.where(lane == 1, i2, 0))
    w_ref[...] = jnp.where(lane == 0, w1, jnp.where(lane == 1, w2, 0.0))
    r_ref[...] = jnp.where(lane == 0, r1, jnp.where(lane == 1, r2, 0.0)).astype(I32)
    cnt_ref[...] = carry_ref[...].astype(I32)


def _router_call(x, w_router):
    t, d = x.shape
    n_exp = w_router.shape[1]
    wr = jnp.zeros((d, LANE), F32).at[:, :n_exp].set(w_router)
    tm = _pick(t, 512)
    row = pl.BlockSpec((tm, LANE), lambda i: (i, 0))
    kern = functools.partial(_router_kernel, n_exp=n_exp, tm=tm)
    return pl.pallas_call(
        kern, grid=(t // tm,),
        in_specs=[pl.BlockSpec((tm, d), lambda i: (i, 0)), pl.BlockSpec((d, LANE), lambda i: (0, 0))],
        out_specs=[row, row, row, pl.BlockSpec((SUBLANE, LANE), lambda i: (0, 0))],
        out_shape=[jax.ShapeDtypeStruct((t, LANE), I32), jax.ShapeDtypeStruct((t, LANE), F32),
                   jax.ShapeDtypeStruct((t, LANE), I32), jax.ShapeDtypeStruct((SUBLANE, LANE), I32)],
        scratch_shapes=[pltpu.VMEM((SUBLANE, LANE), F32)],
        compiler_params=_params("arbitrary"), name="router",
    )(x, wr)


def _row_copy(src_hbm, dst_vmem, src_row, dst_row, sem):
    return pltpu.make_async_copy(src_hbm.at[pl.ds(src_row, 1)], dst_vmem.at[pl.ds(dst_row, 1)], sem)


def _gather_kernel(tok_ref, x_hbm, o_ref, sem, *, rows):
    def start(r, carry):
        _row_copy(x_hbm, o_ref, tok_ref[0, r], r, sem).start()
        return carry

    def wait(r, carry):
        _row_copy(x_hbm, o_ref, 0, r, sem).wait()
        return carry

    lax.fori_loop(0, rows, start, 0)
    lax.fori_loop(0, rows, wait, 0)


def _gather_call(x, tok, rows):
    cap = tok.shape[0]
    d = x.shape[1]
    tok3 = tok.reshape(cap // rows, 1, rows)
    return pl.pallas_call(
        functools.partial(_gather_kernel, rows=rows), grid=(cap // rows,),
        in_specs=[pl.BlockSpec((None, 1, rows), lambda i: (i, 0, 0), memory_space=pltpu.SMEM),
                  pl.BlockSpec(memory_space=pl.ANY)],
        out_specs=pl.BlockSpec((rows, d), lambda i: (i, 0)),
        out_shape=jax.ShapeDtypeStruct((cap, d), x.dtype),
        scratch_shapes=[pltpu.SemaphoreType.DMA(())],
        compiler_params=_params("arbitrary"), name="moe_gather",
    )(tok3, x)


def _combine_ln_kernel(d0_ref, d1_ref, o_hbm, w_ref, x_ref, g_ref, b_ref, o32_ref, o16_ref, buf0, buf1, sem,
                       *, alpha, rows):
    def start(r, carry):
        _row_copy(o_hbm, buf0, d0_ref[0, r], r, sem).start()
        _row_copy(o_hbm, buf1, d1_ref[0, r], r, sem).start()
        return carry

    def wait(r, carry):
        _row_copy(o_hbm, buf0, 0, r, sem).wait()
        _row_copy(o_hbm, buf1, 0, r, sem).wait()
        return carry

    lax.fori_loop(0, rows, start, 0)
    lax.fori_loop(0, rows, wait, 0)
    wts = w_ref[...]
    ffn = wts[:, 0:1] * buf0[...] + wts[:, 1:2] * buf1[...]
    y = _ln_rows(alpha * x_ref[...] + ffn, g_ref[...], b_ref[...])
    o32_ref[...] = y
    o16_ref[...] = y.astype(BF16)


def _combine_ln_call(dest, o_sorted, wts, x, g, b, alpha):
    t, d = x.shape
    rows = _pick(t, 256)
    d0 = dest[:, 0].reshape(t // rows, 1, rows)
    d1 = dest[:, 1].reshape(t // rows, 1, rows)
    idx = pl.BlockSpec((None, 1, rows), lambda i: (i, 0, 0), memory_space=pltpu.SMEM)
    row = pl.BlockSpec((rows, d), lambda i: (i, 0))
    vec = pl.BlockSpec((1, d), lambda i: (0, 0))
    kern = functools.partial(_combine_ln_kernel, alpha=alpha, rows=rows)
    return pl.pallas_call(
        kern, grid=(t // rows,),
        in_specs=[idx, idx, pl.BlockSpec(memory_space=pl.ANY), pl.BlockSpec((rows, LANE), lambda i: (i, 0)),
                  row, vec, vec],
        out_specs=[row, row],
        out_shape=[jax.ShapeDtypeStruct((t, d), F32), jax.ShapeDtypeStruct((t, d), BF16)],
        scratch_shapes=[pltpu.VMEM((rows, d), F32), pltpu.VMEM((rows, d), F32), pltpu.SemaphoreType.DMA(())],
        compiler_params=_params("arbitrary"), name="moe_combine_ln",
    )(d0, d1, o_sorted, wts, x, g.reshape(1, d), b.reshape(1, d))


def _moe_layer(xf, w_router, wg, wu, wd, ln_g, ln_b, alpha):
    t, d = xf.shape
    n_exp = wg.shape[0]
    e_out, w_out, r_out, cnt = _router_call(xf, w_router)
    e = e_out[:, :MOE_TOP_K]
    counts = cnt[0, :n_exp]

    tm = 512 if t * MOE_TOP_K >= 8 * 512 else 128
    n_tiles = -(-(t * MOE_TOP_K) // tm) + n_exp
    cap = n_tiles * tm
    padded = (counts + tm - 1) // tm * tm
    pad_end = jnp.cumsum(padded)
    pad_start = pad_end - padded
    dest = (pad_start[e] + r_out[:, :MOE_TOP_K]).astype(I32)
    tok = jnp.repeat(jnp.arange(t, dtype=I32), MOE_TOP_K)
    slot_tok = jnp.zeros((cap,), I32).at[dest.reshape(-1)].set(tok)
    tile_expert = jnp.minimum(jnp.searchsorted(pad_end, jnp.arange(n_tiles, dtype=I32) * tm, side="right"),
                              n_exp - 1)
    meta = jnp.concatenate([(pad_end[-1:] // tm), tile_expert]).astype(I32)

    xs = _gather_call(xf, slot_tok, _pick(cap, 256))
    h = _swiglu_up_call(meta, xs, wg, wu, tm, "moe_up")
    o_sorted = _moe_down_call(meta, h, wd, tm)
    return _combine_ln_call(dest, o_sorted, w_out, xf, ln_g, ln_b, alpha)


def _trunk(x, conv_bufs, ssm_states, pr):
    bsz, seq, d = x.shape
    t = bsz * seq
    depth = pr["w_main"].shape[0]
    alpha = float((2 * depth) ** 0.25)
    d_inner = pr["w_br_a"].shape[1]
    cdim = pr["conv_w"].shape[2]
    gd = pr["w_br_b"].shape[1]
    kc = pr["conv_w"].shape[1]
    off_xbc = d_inner
    off_u = d_inner + cdim
    off_gate = off_u + 2 * gd
    q_ssd = _pick(seq, 128)
    q_gmlp = min(pr["w_s"].shape[2], seq)

    xf, xb = _ln_call(x.reshape(t, d), pr["ln_in_g"], pr["ln_in_b"])
    convs, ssms, vs = [], [], []
    for i in range(depth):
        proj = _mm_call(xb, pr["w_main"][i], F32, "in_proj")
        dt = _dt_call(xf, pr["w_dt"][i], pr["dt_bias"][i])
        ya, ssm_i = _ssd_call(proj, dt, conv_bufs[i], ssm_states[i], pr["conv_w"][i], pr["conv_b"][i],
                              pr["a_log"][i], pr["d_skip"][i], pr["ssd_norm_g"][i], bsz=bsz, seq=seq,
                              d_inner=d_inner, off_z=0, off_xbc=off_xbc, q=q_ssd)
        xbc = proj[:, off_xbc:off_xbc + cdim].reshape(bsz, seq, cdim)
        conv_i = jnp.concatenate([conv_bufs[i], xbc[:, max(seq - (kc - 1), 0):]], axis=1)[:, -(kc - 1):]
        yb, vn = _gmlp_call(proj, pr["gmlp_ln_g"][i], pr["gmlp_ln_b"][i], pr["w_s"][i], pr["b_s"][i],
                            bsz=bsz, seq=seq, gd=gd, off_u=off_u, q=q_gmlp)
        merged = _merge_call(ya, yb, proj, pr["w_br_a"][i], pr["w_br_b"][i], pr["b_gate"][i], off_gate=off_gate)
        xf, xb = _mm_res_ln_call(merged, pr["w_o"][i], xf, pr["ln1_g"][i], pr["ln1_b"][i], alpha, "out_proj_ln")
        j = i // 2
        if i % 2 == 0:
            tm = _pick(t, 1024)
            meta = jnp.concatenate([jnp.full((1,), t // tm, I32), jnp.zeros((t // tm,), I32)])
            h = _swiglu_up_call(meta, xb, pr["w_ff_gate"][j][None], pr["w_ff_up"][j][None], tm, "ffn_up")
            xf, xb = _mm_res_ln_call(h, pr["w_ff_down"][j], xf, pr["ln2_g"][i], pr["ln2_b"][i], alpha,
                                     "ffn_down_ln")
        else:
            xf, xb = _moe_layer(xf, pr["w_router"][j], pr["w_moe_gate"][j], pr["w_moe_up"][j],
                                pr["w_moe_down"][j], pr["ln2_g"][i], pr["ln2_b"][i], alpha)
        convs.append(conv_i)
        ssms.append(ssm_i)
        vs.append(vn.reshape(bsz, seq, gd))
    return xf.reshape(bsz, seq, d), jnp.stack(convs), jnp.stack(ssms), jnp.stack(vs)


def kernel(x_prompt, x_sample, cache_conv, state_ssm, ln_in_g, ln_in_b, w_in, conv_w, conv_b, dt_bias, a_log,
           d_skip, ssd_norm_g, gmlp_ln_g, gmlp_ln_b, w_s, b_s, b_gate, w_br_a, w_br_b, w_o, ln1_g, ln1_b,
           w_ff_gate, w_ff_up, w_ff_down, w_router, w_moe_gate, w_moe_up, w_moe_down, ln2_g, ln2_b):
    depth, d_model, _ = w_in.shape
    d_inner = w_br_a.shape[1]
    cdim = conv_w.shape[2]
    heads = a_log.shape[1]
    o_dt = d_inner + cdim
    w_main = jnp.concatenate([w_in[:, :, :o_dt], w_in[:, :, o_dt + heads:]], axis=2).astype(BF16)
    w_dt = jnp.zeros((depth, d_model, LANE), F32).at[:, :, :heads].set(w_in[:, :, o_dt:o_dt + heads])
    dt_b = jnp.zeros((depth, 1, LANE), F32).at[:, 0, :heads].set(dt_bias)
    pr = dict(
        ln_in_g=ln_in_g, ln_in_b=ln_in_b, w_main=w_main, w_dt=w_dt, dt_bias=dt_b, conv_w=conv_w, conv_b=conv_b,
        a_log=a_log, d_skip=d_skip, ssd_norm_g=ssd_norm_g, gmlp_ln_g=gmlp_ln_g, gmlp_ln_b=gmlp_ln_b, w_s=w_s,
        b_s=b_s, b_gate=b_gate, w_br_a=w_br_a.astype(BF16), w_br_b=w_br_b.astype(BF16), w_o=w_o.astype(BF16),
        ln1_g=ln1_g, ln1_b=ln1_b, w_ff_gate=w_ff_gate.astype(BF16), w_ff_up=w_ff_up.astype(BF16),
        w_ff_down=w_ff_down.astype(BF16), w_router=w_router, w_moe_gate=w_moe_gate.astype(BF16),
        w_moe_up=w_moe_up.astype(BF16), w_moe_down=w_moe_down.astype(BF16), ln2_g=ln2_g, ln2_b=ln2_b)
    bp = x_prompt.shape[0]
    zero_conv = jnp.zeros((depth, bp) + cache_conv.shape[2:], x_prompt.dtype)
    zero_ssm = jnp.zeros((depth, bp) + state_ssm.shape[2:], state_ssm.dtype)
    y_prompt, prompt_conv, prompt_ssm, _ = _trunk(x_prompt, zero_conv, zero_ssm, pr)
    y_sample, sample_conv, sample_ssm, sample_v = _trunk(x_sample, cache_conv, state_ssm, pr)
    return (y_prompt, y_sample, prompt_conv, prompt_ssm, sample_conv, sample_ssm, sample_v)
```

```python
import functools

import jax
import jax.numpy as jnp
from jax import lax
from jax.experimental import pallas as pl
from jax.experimental.pallas import tpu as pltpu

F32 = jnp.float32
BF16 = jnp.bfloat16
I32 = jnp.int32
HIGHEST = lax.Precision.HIGHEST

LN_EPS = 1e-5
LANE = 128
SUBLANE = 8
V7X_VMEM_BYTES = 64 * 1024 * 1024
VMEM_LIMIT_BYTES = V7X_VMEM_BYTES - 8 * 1024 * 1024
MOE_TOP_K = 2
_NEG_BIG = -1e30
_TILE_CANDIDATES = (1024, 512, 256, 128, 64, 32, 16, 8)


def _pick(n, cap):
    for c in _TILE_CANDIDATES:
        if c <= cap and n % c == 0:
            return c
    raise ValueError(f"no tile for {n}")


def _params(*sem):
    return pltpu.CompilerParams(dimension_semantics=sem, vmem_limit_bytes=VMEM_LIMIT_BYTES)


def _ln_rows(x, g, b):
    mu = jnp.mean(x, axis=-1, keepdims=True)
    xc = x - mu
    var = jnp.mean(xc * xc, axis=-1, keepdims=True)
    return xc * lax.rsqrt(var + LN_EPS) * g + b


def _sigmoid(x):
    return 0.5 * jnp.tanh(0.5 * x) + 0.5


def _silu(x):
    return x * _sigmoid(x)


def _gelu(x):
    return 0.5 * x * (1.0 + lax.erf(x * (2.0 ** -0.5)))


def _dot(a, b):
    return jnp.dot(a, b, preferred_element_type=F32)


def _dot_exact(a, b):
    return jnp.dot(a, b, precision=HIGHEST, preferred_element_type=F32)


def _ln_kernel(x_ref, g_ref, b_ref, o32_ref, o16_ref):
    y = _ln_rows(x_ref[...], g_ref[...], b_ref[...])
    o32_ref[...] = y
    o16_ref[...] = y.astype(BF16)


def _ln_call(x, g, b):
    t, d = x.shape
    tm = _pick(t, 512)
    row = pl.BlockSpec((tm, d), lambda i: (i, 0))
    vec = pl.BlockSpec((1, d), lambda i: (0, 0))
    return pl.pallas_call(
        _ln_kernel, grid=(t // tm,), in_specs=[row, vec, vec], out_specs=[row, row],
        out_shape=[jax.ShapeDtypeStruct((t, d), F32), jax.ShapeDtypeStruct((t, d), BF16)],
        compiler_params=_params("parallel"), name="ln_in",
    )(x, g.reshape(1, d), b.reshape(1, d))


def _mm_kernel(x_ref, w_ref, o_ref):
    o_ref[...] = _dot(x_ref[...], w_ref[...]).astype(o_ref.dtype)


def _mm_call(x, w, out_dtype, name):
    t, k = x.shape
    n = w.shape[1]
    tm, tn = _pick(t, 1024), _pick(n, 1024)
    return pl.pallas_call(
        _mm_kernel, grid=(t // tm, n // tn),
        in_specs=[pl.BlockSpec((tm, k), lambda i, j: (i, 0)), pl.BlockSpec((k, tn), lambda i, j: (0, j))],
        out_specs=pl.BlockSpec((tm, tn), lambda i, j: (i, j)),
        out_shape=jax.ShapeDtypeStruct((t, n), out_dtype),
        compiler_params=_params("parallel", "parallel"), name=name,
    )(x, w)


def _dt_kernel(x_ref, w_ref, b_ref, al_ref, dt_ref, ac_ref, *, q):
    raw = _dot_exact(x_ref[...], w_ref[...]) + b_ref[...]
    dt = jnp.maximum(raw, 0.0) + jnp.log1p(jnp.exp(-jnp.abs(raw)))
    dt_ref[...] = dt
    a = dt * (-jnp.exp(al_ref[...]))
    ri = lax.broadcasted_iota(I32, (q, q), 0)
    ci = lax.broadcasted_iota(I32, (q, q), 1)
    tril = (ri >= ci).astype(F32)
    for c in range(a.shape[0] // q):
        ac_ref[c * q:(c + 1) * q, :] = _dot_exact(tril, a[c * q:(c + 1) * q, :])


def _dt_call(x, w, bias, a_log, q):
    t, k = x.shape
    n = w.shape[1]
    tm = max(_pick(t, 512), q)
    assert tm % q == 0 and t % tm == 0
    row = pl.BlockSpec((tm, n), lambda i: (i, 0))
    vec = pl.BlockSpec((1, n), lambda i: (0, 0))
    return pl.pallas_call(
        functools.partial(_dt_kernel, q=q), grid=(t // tm,),
        in_specs=[pl.BlockSpec((tm, k), lambda i: (i, 0)), pl.BlockSpec((k, n), lambda i: (0, 0)), vec, vec],
        out_specs=[row, row],
        out_shape=[jax.ShapeDtypeStruct((t, n), F32), jax.ShapeDtypeStruct((t, n), F32)],
        compiler_params=_params("parallel"), name="dt_proj",
    )(x, w, bias, a_log)


_CONV_PAD = SUBLANE


def _ssd_kernel(xs_ref, bm_ref, cm_ref, z_ref, dt_ref, dtt_ref, cbx_ref, cbb_ref, cbc_ref, s0_ref,
                cwx_ref, cwb_ref, cwc_ref, ccx_ref, ccb_ref, ccc_ref, alr_ref, alc_ref, dsk_ref, ng_ref,
                y_ref, sout_ref, xpad_ref, st_ref, *, q, hpg, p, n, kc, nc):
    c = pl.program_id(2)
    w = hpg * p
    tail = kc - 1
    t0 = _CONV_PAD - tail

    @pl.when(c == 0)
    def _():
        xpad_ref[t0:_CONV_PAD, 0:w] = cbx_ref[...]
        xpad_ref[t0:_CONV_PAD, w:w + n] = cbb_ref[...]
        xpad_ref[t0:_CONV_PAD, w + n:w + 2 * n] = cbc_ref[...]
        st_ref[...] = s0_ref[...].reshape(w, n).T

    xpad_ref[_CONV_PAD:_CONV_PAD + q, 0:w] = xs_ref[...]
    xpad_ref[_CONV_PAD:_CONV_PAD + q, w:w + n] = bm_ref[...]
    xpad_ref[_CONV_PAD:_CONV_PAD + q, w + n:w + 2 * n] = cm_ref[...]

    def conv(lo, hi, cw_ref, cc_ref):
        acc = cc_ref[...]
        for k in range(kc):
            acc = acc + xpad_ref[t0 + k:t0 + k + q, lo:hi] * cw_ref[k:k + 1, :]
        return _silu(acc)

    xs = conv(0, w, cwx_ref, ccx_ref)
    bm = conv(w, w + n, cwb_ref, ccb_ref)
    cm = conv(w + n, w + 2 * n, cwc_ref, ccc_ref)
    xpad_ref[t0:_CONV_PAD, :] = xpad_ref[t0 + q:_CONV_PAD + q, :]

    a = dt_ref[...] * (-jnp.exp(alr_ref[...]))
    a_t = dtt_ref[...] * (-jnp.exp(alc_ref[...]))
    ri = lax.broadcasted_iota(I32, (q, q), 0)
    ci = lax.broadcasted_iota(I32, (q, q), 1)
    tril = ri >= ci
    acol = _dot_exact(tril.astype(F32), a)
    arow = _dot_exact(a_t, (ri <= ci).astype(F32))

    hrow = lax.broadcasted_iota(I32, (hpg, w), 0)
    hlane = lax.broadcasted_iota(I32, (hpg, w), 1)
    expand = ((hlane >= hrow * p) & (hlane < (hrow + 1) * p)).astype(F32)
    xdt = xs * _dot_exact(dt_ref[...], expand)
    eacol = _dot_exact(jnp.exp(acol), expand)
    to_end = _dot_exact(jnp.exp(acol[q - 1:q, :] - acol), expand)

    bm16 = bm.astype(BF16)
    cm16 = cm.astype(BF16)
    cb = lax.dot_general(cm16, bm16, (((1,), (1,)), ((), ())), preferred_element_type=F32)
    st = st_ref[...]
    y = _dot(cm16, st.astype(BF16)) * eacol
    lane = lax.broadcasted_iota(I32, (q, w), 1)
    for h in range(hpg):
        seg = acol[:, h:h + 1] - arow[h:h + 1, :]
        m = (cb * jnp.exp(jnp.where(tril, seg, _NEG_BIG))).astype(BF16)
        head = (lane >= h * p) & (lane < (h + 1) * p)
        y = y + _dot(m, jnp.where(head, xdt, 0.0).astype(BF16))
    y = y + dsk_ref[...] * xs
    hz = y * _silu(z_ref[...])
    y_ref[...] = (hz * lax.rsqrt(jnp.mean(hz * hz, axis=-1, keepdims=True) + LN_EPS) * ng_ref[...]).astype(BF16)

    upd = lax.dot_general(bm16, (xdt * to_end).astype(BF16), (((0,), (0,)), ((), ())),
                          preferred_element_type=F32)
    st_ref[...] = st * eacol[q - 1:q, :] + upd

    @pl.when(c == nc - 1)
    def _():
        sout_ref[...] = st_ref[...].T.reshape(hpg, p, n)


def _ssd_call(proj, dt, conv_buf, state0, conv_w, conv_b, a_log, d_skip, norm_g, *, bsz, seq, d_inner, off_z,
              off_xbc, q):
    heads, p, n = state0.shape[1:]
    cdim = conv_w.shape[1]
    kc = conv_w.shape[0]
    groups = (cdim - d_inner) // (2 * n)
    hpg = heads // groups
    w = hpg * p
    nc = seq // q
    t = bsz * seq
    assert w * groups == d_inner and seq % q == 0 and q >= kc - 1
    assert off_z % w == 0 and off_xbc % w == 0 and (off_xbc + d_inner) % n == 0

    dt4 = dt[:, :heads].reshape(bsz, seq, groups, hpg).transpose(0, 2, 1, 3)
    dtt4 = dt4.transpose(0, 1, 3, 2)
    alr = a_log.reshape(groups, 1, hpg)
    alc = a_log.reshape(groups, hpg, 1)
    dsk = jnp.repeat(d_skip, p).reshape(1, d_inner)
    ng = norm_g.reshape(1, d_inner)
    ccb = conv_b.reshape(1, cdim)

    zb, xb = off_z // w, off_xbc // w
    bb, cb_ = (off_xbc + d_inner) // n, (off_xbc + d_inner + groups * n) // n
    row = lambda b, g, c: b * nc + c
    in_specs = [
        pl.BlockSpec((q, w), lambda b, g, c: (row(b, g, c), xb + g)),
        pl.BlockSpec((q, n), lambda b, g, c: (row(b, g, c), bb + g)),
        pl.BlockSpec((q, n), lambda b, g, c: (row(b, g, c), cb_ + g)),
        pl.BlockSpec((q, w), lambda b, g, c: (row(b, g, c), zb + g)),
        pl.BlockSpec((None, None, q, hpg), lambda b, g, c: (b, g, c, 0)),
        pl.BlockSpec((None, None, hpg, q), lambda b, g, c: (b, g, 0, c)),
        pl.BlockSpec((None, kc - 1, w), lambda b, g, c: (b, 0, g)),
        pl.BlockSpec((None, kc - 1, n), lambda b, g, c: (b, 0, d_inner // n + g)),
        pl.BlockSpec((None, kc - 1, n), lambda b, g, c: (b, 0, d_inner // n + groups + g)),
        pl.BlockSpec((None, hpg, p, n), lambda b, g, c: (b, g, 0, 0)),
        pl.BlockSpec((kc, w), lambda b, g, c: (0, g)),
        pl.BlockSpec((kc, n), lambda b, g, c: (0, d_inner // n + g)),
        pl.BlockSpec((kc, n), lambda b, g, c: (0, d_inner // n + groups + g)),
        pl.BlockSpec((1, w), lambda b, g, c: (0, g)),
        pl.BlockSpec((1, n), lambda b, g, c: (0, d_inner // n + g)),
        pl.BlockSpec((1, n), lambda b, g, c: (0, d_inner // n + groups + g)),
        pl.BlockSpec((None, 1, hpg), lambda b, g, c: (g, 0, 0)),
        pl.BlockSpec((None, hpg, 1), lambda b, g, c: (g, 0, 0)),
        pl.BlockSpec((1, w), lambda b, g, c: (0, g)),
        pl.BlockSpec((1, w), lambda b, g, c: (0, g)),
    ]
    out_specs = [
        pl.BlockSpec((q, w), lambda b, g, c: (row(b, g, c), g)),
        pl.BlockSpec((None, hpg, p, n), lambda b, g, c: (b, g, 0, 0)),
    ]
    kern = functools.partial(_ssd_kernel, q=q, hpg=hpg, p=p, n=n, kc=kc, nc=nc)
    return pl.pallas_call(
        kern, grid=(bsz, groups, nc), in_specs=in_specs, out_specs=out_specs,
        out_shape=[jax.ShapeDtypeStruct((t, d_inner), BF16), jax.ShapeDtypeStruct(state0.shape, F32)],
        scratch_shapes=[pltpu.VMEM((_CONV_PAD + q, w + 2 * n), F32), pltpu.VMEM((n, w), F32)],
        compiler_params=_params("parallel", "parallel", "arbitrary"), name="ssd",
    )(proj, proj, proj, proj, dt4, dtt4, conv_buf, conv_buf, conv_buf, state0,
      conv_w, conv_w, conv_w, ccb, ccb, ccb, alr, alc, dsk, ng)


def _ssd_cm_kernel(xs_ref, bm_ref, cm_ref, z_ref, dtt_ref, act_ref, ac_ref, cbx_ref, cbb_ref, cbc_ref, s0_ref,
                   cwx_ref, cwb_ref, cwc_ref, ccx_ref, ccb_ref, ccc_ref, dsk_ref, ng_ref,
                   y_ref, sout_ref, xpad_ref, st_ref, *, q, gs, hpg, p, n, kc, nc):
    c = pl.program_id(2)
    w = hpg * p
    tail = kc - 1
    t0 = _CONV_PAD - tail
    ob, oc = gs * w, gs * (w + n)

    @pl.when(c == 0)
    def _():
        xpad_ref[t0:_CONV_PAD, 0:ob] = cbx_ref[...]
        xpad_ref[t0:_CONV_PAD, ob:oc] = cbb_ref[...]
        xpad_ref[t0:_CONV_PAD, oc:oc + gs * n] = cbc_ref[...]
        st_ref[...] = s0_ref[...].reshape(gs * w, n)

    xpad_ref[_CONV_PAD:_CONV_PAD + q, 0:ob] = xs_ref[...]
    xpad_ref[_CONV_PAD:_CONV_PAD + q, ob:oc] = bm_ref[...]
    xpad_ref[_CONV_PAD:_CONV_PAD + q, oc:oc + gs * n] = cm_ref[...]

    def conv(lo, hi, cw_ref, cc_ref, wlo, whi):
        acc = cc_ref[:, wlo:whi]
        for k in range(kc):
            acc = acc + xpad_ref[t0 + k:t0 + k + q, lo:hi] * cw_ref[k:k + 1, wlo:whi]
        return _silu(acc)

    def rows(v):
        return jnp.concatenate([jnp.broadcast_to(v[h:h + 1, :], (p, v.shape[1])) for h in range(hpg)], axis=0)

    si = lax.broadcasted_iota(I32, (q, q), 0)
    li = lax.broadcasted_iota(I32, (q, q), 1)
    keep = si <= li
    nt = (((1,), (1,)), ((), ()))
    for g in range(gs):
        xs = conv(g * w, (g + 1) * w, cwx_ref, ccx_ref, g * w, (g + 1) * w)
        bm16 = conv(ob + g * n, ob + (g + 1) * n, cwb_ref, ccb_ref, g * n, (g + 1) * n).astype(BF16)
        cm16 = conv(oc + g * n, oc + (g + 1) * n, cwc_ref, ccc_ref, g * n, (g + 1) * n).astype(BF16)
        arow = act_ref[g]
        acol = ac_ref[g]
        ear = jnp.exp(arow)
        te = jnp.exp(arow[:, q - 1:q] - arow)
        xdt_t = xs.T * rows(dtt_ref[g])
        xdt16 = xdt_t.astype(BF16)
        cb_t = lax.dot_general(bm16, cm16, nt, preferred_element_type=F32)
        ys = []
        for h in range(hpg):
            seg = arow[h:h + 1, :] - acol[:, h:h + 1]
            m_t = (cb_t * jnp.exp(jnp.where(keep, seg, _NEG_BIG))).astype(BF16)
            ys.append(_dot(xdt16[h * p:(h + 1) * p, :], m_t))
        st = st_ref[g * w:(g + 1) * w, :]
        y_t = jnp.concatenate(ys, axis=0)
        y_t = y_t + lax.dot_general(st.astype(BF16), cm16, nt, preferred_element_type=F32) * rows(ear)
        y = y_t.T + dsk_ref[:, g * w:(g + 1) * w] * xs
        hz = y * _silu(z_ref[:, g * w:(g + 1) * w])
        y_ref[:, g * w:(g + 1) * w] = (hz * lax.rsqrt(jnp.mean(hz * hz, axis=-1, keepdims=True) + LN_EPS)
                                       * ng_ref[:, g * w:(g + 1) * w]).astype(BF16)
        upd = _dot((xdt_t * rows(te)).astype(BF16), bm16)
        st_ref[g * w:(g + 1) * w, :] = st * rows(ear[:, q - 1:q]) + upd
    xpad_ref[t0:_CONV_PAD, :] = xpad_ref[t0 + q:_CONV_PAD + q, :]

    @pl.when(c == nc - 1)
    def _():
        sout_ref[...] = st_ref[...].reshape(gs * hpg, p, n)


def _ssd_cm_call(proj, dt, acum, conv_buf, state0, conv_w, conv_b, d_skip, norm_g, *, bsz, seq, d_inner, off_z,
                 off_xbc, q, gs):
    heads, p, n = state0.shape[1:]
    cdim = conv_w.shape[1]
    kc = conv_w.shape[0]
    groups = (cdim - d_inner) // (2 * n)
    hpg = heads // groups
    w = hpg * p
    nc = seq // q
    t = bsz * seq
    gw, gn = gs * w, gs * n
    assert w * groups == d_inner and seq % q == 0 and q % LANE == 0 and groups % gs == 0
    assert off_z % gw == 0 and off_xbc % gw == 0 and (off_xbc + d_inner) % gn == 0 and (groups * n) % gn == 0

    def heads_major(v):
        return v[:, :heads].reshape(bsz, seq, groups, hpg).transpose(0, 2, 3, 1)

    dtt4 = heads_major(dt)
    act4 = heads_major(acum)
    ac4 = act4.transpose(0, 1, 3, 2)
    dsk = jnp.repeat(d_skip, p).reshape(1, d_inner)
    ng = norm_g.reshape(1, d_inner)
    ccb = conv_b.reshape(1, cdim)

    zb, xb = off_z // gw, off_xbc // gw
    bb, cb_ = (off_xbc + d_inner) // gn, (off_xbc + d_inner + groups * n) // gn
    wb, wc = d_inner // gn, (d_inner + groups * n) // gn
    row = lambda b, g, c: b * nc + c
    in_specs = [
        pl.BlockSpec((q, gw), lambda b, g, c: (row(b, g, c), xb + g)),
        pl.BlockSpec((q, gn), lambda b, g, c: (row(b, g, c), bb + g)),
        pl.BlockSpec((q, gn), lambda b, g, c: (row(b, g, c), cb_ + g)),
        pl.BlockSpec((q, gw), lambda b, g, c: (row(b, g, c), zb + g)),
        pl.BlockSpec((None, gs, hpg, q), lambda b, g, c: (b, g, 0, c)),
        pl.BlockSpec((None, gs, hpg, q), lambda b, g, c: (b, g, 0, c)),
        pl.BlockSpec((None, gs, q, hpg), lambda b, g, c: (b, g, c, 0)),
        pl.BlockSpec((None, kc - 1, gw), lambda b, g, c: (b, 0, g)),
        pl.BlockSpec((None, kc - 1, gn), lambda b, g, c: (b, 0, wb + g)),
        pl.BlockSpec((None, kc - 1, gn), lambda b, g, c: (b, 0, wc + g)),
        pl.BlockSpec((None, gs * hpg, p, n), lambda b, g, c: (b, g, 0, 0)),
        pl.BlockSpec((kc, gw), lambda b, g, c: (0, g)),
        pl.BlockSpec((kc, gn), lambda b, g, c: (0, wb + g)),
        pl.BlockSpec((kc, gn), lambda b, g, c: (0, wc + g)),
        pl.BlockSpec((1, gw), lambda b, g, c: (0, g)),
        pl.BlockSpec((1, gn), lambda b, g, c: (0, wb + g)),
        pl.BlockSpec((1, gn), lambda b, g, c: (0, wc + g)),
        pl.BlockSpec((1, gw), lambda b, g, c: (0, g)),
        pl.BlockSpec((1, gw), lambda b, g, c: (0, g)),
    ]
    out_specs = [
        pl.BlockSpec((q, gw), lambda b, g, c: (row(b, g, c), g)),
        pl.BlockSpec((None, gs * hpg, p, n), lambda b, g, c: (b, g, 0, 0)),
    ]
    kern = functools.partial(_ssd_cm_kernel, q=q, gs=gs, hpg=hpg, p=p, n=n, kc=kc, nc=nc)
    return pl.pallas_call(
        kern, grid=(bsz, groups // gs, nc), in_specs=in_specs, out_specs=out_specs,
        out_shape=[jax.ShapeDtypeStruct((t, d_inner), BF16), jax.ShapeDtypeStruct(state0.shape, F32)],
        scratch_shapes=[pltpu.VMEM((_CONV_PAD + q, gs * (w + 2 * n)), F32), pltpu.VMEM((gs * w, n), F32)],
        compiler_params=_params("parallel", "parallel", "arbitrary"), name="ssd_cm",
    )(proj, proj, proj, proj, dtt4, act4, ac4, conv_buf, conv_buf, conv_buf, state0,
      conv_w, conv_w, conv_w, ccb, ccb, ccb, dsk, ng)


def _gmlp_kernel(u_ref, v_ref, lg_ref, lb_ref, ws_ref, bs_ref, yb_ref, vn_ref, *, groups, q):
    u = _gelu(u_ref[...])
    vn = _ln_rows(_gelu(v_ref[...]), lg_ref[...], lb_ref[...])
    vn_ref[...] = vn
    vn16 = vn.astype(BF16)
    d = vn.shape[1] // groups
    ri = lax.broadcasted_iota(I32, (q, q), 0)
    ci = lax.broadcasted_iota(I32, (q, q), 1)
    for g in range(groups):
        wg = jnp.where(ri >= ci, ws_ref[g], 0.0).astype(BF16)
        s = _dot(wg, vn16[:, g * d:(g + 1) * d]) + bs_ref[g]
        yb_ref[:, g * d:(g + 1) * d] = (u[:, g * d:(g + 1) * d] * s).astype(BF16)


def _gmlp_call(proj, ln_g, ln_b, w_s, b_s, *, bsz, seq, gd, off_u, q):
    groups = w_s.shape[0]
    t = bsz * seq
    assert off_u % gd == 0 and seq % q == 0 and (gd // groups) % LANE == 0
    ub = off_u // gd
    ws = w_s[:, :q, :q]
    bs = b_s[:, :q, None]
    row = pl.BlockSpec((q, gd), lambda i: (i, 0))
    kern = functools.partial(_gmlp_kernel, groups=groups, q=q)
    return pl.pallas_call(
        kern, grid=(t // q,),
        in_specs=[pl.BlockSpec((q, gd), lambda i: (i, ub)), pl.BlockSpec((q, gd), lambda i: (i, ub + 1)),
                  pl.BlockSpec((1, gd), lambda i: (0, 0)), pl.BlockSpec((1, gd), lambda i: (0, 0)),
                  pl.BlockSpec((groups, q, q), lambda i: (0, 0, 0)),
                  pl.BlockSpec((groups, q, 1), lambda i: (0, 0, 0))],
        out_specs=[row, row],
        out_shape=[jax.ShapeDtypeStruct((t, gd), BF16), jax.ShapeDtypeStruct((t, gd), F32)],
        compiler_params=_params("parallel"), name="gmlp",
    )(proj, proj, ln_g.reshape(1, gd), ln_b.reshape(1, gd), ws, bs)


def _merge_kernel(ya_ref, yb_ref, wa_ref, wb_ref, ga_ref, gb_ref, bga_ref, bgb_ref, o_ref):
    a = _dot(ya_ref[...], wa_ref[...])
    b = _dot(yb_ref[...], wb_ref[...])
    o = _sigmoid(ga_ref[...] + bga_ref[...]) * a + _sigmoid(gb_ref[...] + bgb_ref[...]) * b
    o_ref[...] = o.astype(o_ref.dtype)


def _merge_call(ya, yb, proj, w_a, w_b, b_gate, *, off_gate):
    t, ka = ya.shape
    kb = yb.shape[1]
    d = w_a.shape[1]
    tm, tn = _pick(t, 512), _pick(d, 512)
    assert off_gate % tn == 0
    gb0 = off_gate // tn
    nd = d // tn
    bg = b_gate.reshape(1, 2 * d)
    return pl.pallas_call(
        _merge_kernel, grid=(t // tm, nd),
        in_specs=[pl.BlockSpec((tm, ka), lambda i, j: (i, 0)), pl.BlockSpec((tm, kb), lambda i, j: (i, 0)),
                  pl.BlockSpec((ka, tn), lambda i, j: (0, j)), pl.BlockSpec((kb, tn), lambda i, j: (0, j)),
                  pl.BlockSpec((tm, tn), lambda i, j: (i, gb0 + j)),
                  pl.BlockSpec((tm, tn), lambda i, j: (i, gb0 + nd + j)),
                  pl.BlockSpec((1, tn), lambda i, j: (0, j)), pl.BlockSpec((1, tn), lambda i, j: (0, nd + j))],
        out_specs=pl.BlockSpec((tm, tn), lambda i, j: (i, j)),
        out_shape=jax.ShapeDtypeStruct((t, d), BF16),
        compiler_params=_params("parallel", "parallel"), name="merge",
    )(ya, yb, w_a, w_b, proj, proj, bg, bg)


def _mm_res_ln_kernel(a_ref, w_ref, x_ref, g_ref, b_ref, o32_ref, o16_ref, acc_ref, *, alpha, nk):
    k = pl.program_id(1)

    @pl.when(k == 0)
    def _():
        acc_ref[...] = jnp.zeros_like(acc_ref)

    acc_ref[...] += _dot(a_ref[...], w_ref[...])

    @pl.when(k == nk - 1)
    def _():
        y = _ln_rows(alpha * x_ref[...] + acc_ref[...], g_ref[...], b_ref[...])
        o32_ref[...] = y
        o16_ref[...] = y.astype(BF16)


def _k_tile(k, cap):
    if k % LANE != 0:
        return k
    best = LANE
    for m in range(1, k // LANE + 1):
        tk = m * LANE
        if k % tk == 0 and tk <= cap:
            best = tk
    return best


def _mm_res_ln_call(a, w, x, g, b, alpha, name):
    t, k = a.shape
    d = w.shape[1]
    tm = _pick(t, 512)
    tk = _k_tile(k, 2048)
    nk = k // tk
    row = pl.BlockSpec((tm, d), lambda i, kk: (i, 0))
    vec = pl.BlockSpec((1, d), lambda i, kk: (0, 0))
    kern = functools.partial(_mm_res_ln_kernel, alpha=alpha, nk=nk)
    return pl.pallas_call(
        kern, grid=(t // tm, nk),
        in_specs=[pl.BlockSpec((tm, tk), lambda i, kk: (i, kk)), pl.BlockSpec((tk, d), lambda i, kk: (kk, 0)),
                  row, vec, vec],
        out_specs=[row, row],
        out_shape=[jax.ShapeDtypeStruct((t, d), F32), jax.ShapeDtypeStruct((t, d), BF16)],
        scratch_shapes=[pltpu.VMEM((tm, d), F32)],
        compiler_params=_params("parallel", "arbitrary"), name=name,
    )(a, w, x, g.reshape(1, d), b.reshape(1, d))


def _swiglu_up_kernel(meta_ref, x_ref, wg_ref, wu_ref, o_ref):
    live = pl.program_id(0) < meta_ref[0]

    @pl.when(live)
    def _():
        x = x_ref[...].astype(BF16)
        g = _dot(x, wg_ref[...])
        u = _dot(x, wu_ref[...])
        o_ref[...] = (_silu(g) * u).astype(o_ref.dtype)

    @pl.when(jnp.logical_not(live))
    def _():
        o_ref[...] = jnp.zeros_like(o_ref)


def _swiglu_up_call(meta, x, wg, wu, tm, name):
    t, k = x.shape
    n = wg.shape[2]
    tn = _pick(n, 1024)
    nn = n // tn

    def live(i, m):
        return jnp.minimum(i, m[0] - 1)

    def col(i, j, m):
        return jnp.where(i < m[0], j, nn - 1)

    wspec = pl.BlockSpec((None, k, tn), lambda i, j, m: (m[1 + live(i, m)], 0, col(i, j, m)))
    grid_spec = pltpu.PrefetchScalarGridSpec(
        num_scalar_prefetch=1, grid=(t // tm, nn),
        in_specs=[pl.BlockSpec((tm, k), lambda i, j, m: (live(i, m), 0)), wspec, wspec],
        out_specs=pl.BlockSpec((tm, tn), lambda i, j, m: (i, j)))
    return pl.pallas_call(
        _swiglu_up_kernel, grid_spec=grid_spec, out_shape=jax.ShapeDtypeStruct((t, n), BF16),
        compiler_params=_params("arbitrary", "arbitrary"), name=name,
    )(meta, x, wg, wu)


def _moe_down_kernel(meta_ref, h_ref, w_ref, o_ref, acc_ref, *, nk):
    k = pl.program_id(1)
    live = pl.program_id(0) < meta_ref[0]

    @pl.when(live)
    def _():
        @pl.when(k == 0)
        def _():
            acc_ref[...] = jnp.zeros_like(acc_ref)

        acc_ref[...] += _dot(h_ref[...], w_ref[...])

        @pl.when(k == nk - 1)
        def _():
            o_ref[...] = acc_ref[...]

    @pl.when(jnp.logical_not(live) & (k == nk - 1))
    def _():
        o_ref[...] = jnp.zeros_like(o_ref)


def _moe_down_call(meta, h, wd, tm):
    t, k = h.shape
    d = wd.shape[2]
    tk = _k_tile(k, 2048)
    nk = k // tk

    def live(i, m):
        return jnp.minimum(i, m[0] - 1)

    def kk(i, k_, m):
        return jnp.where(i < m[0], k_, nk - 1)

    grid_spec = pltpu.PrefetchScalarGridSpec(
        num_scalar_prefetch=1, grid=(t // tm, nk),
        in_specs=[pl.BlockSpec((tm, tk), lambda i, k_, m: (live(i, m), kk(i, k_, m))),
                  pl.BlockSpec((None, tk, d), lambda i, k_, m: (m[1 + live(i, m)], kk(i, k_, m), 0))],
        out_specs=pl.BlockSpec((tm, d), lambda i, k_, m: (i, 0)),
        scratch_shapes=[pltpu.VMEM((tm, d), F32)])
    return pl.pallas_call(
        functools.partial(_moe_down_kernel, nk=nk), grid_spec=grid_spec,
        out_shape=jax.ShapeDtypeStruct((t, d), F32),
        compiler_params=_params("arbitrary", "arbitrary"), name="moe_down",
    )(meta, h, wd)


def _router_kernel(x_ref, wr_ref, e_ref, w_ref, r_ref, cnt_ref, carry_ref, *, n_exp, tm):
    @pl.when(pl.program_id(0) == 0)
    def _():
        carry_ref[...] = jnp.zeros_like(carry_ref)

    logits = _dot_exact(x_ref[...], wr_ref[...])
    lane = lax.broadcasted_iota(I32, logits.shape, 1)
    lg = jnp.where(lane < n_exp, logits, -jnp.inf)
    m1 = jnp.max(lg, axis=-1, keepdims=True)
    i1 = jnp.min(jnp.where(lg == m1, lane, LANE), axis=-1, keepdims=True)
    lg2 = jnp.where(lane == i1, -jnp.inf, lg)
    m2 = jnp.max(lg2, axis=-1, keepdims=True)
    i2 = jnp.min(jnp.where(lg2 == m2, lane, LANE), axis=-1, keepdims=True)
    ex = jnp.exp(m2 - m1)
    w1 = 1.0 / (1.0 + ex)
    w2 = ex / (1.0 + ex)

    oh1 = (lane == i1).astype(F32)
    oh2 = (lane == i2).astype(F32)
    both = oh1 + oh2
    ri = lax.broadcasted_iota(I32, (tm, tm), 0)
    ci = lax.broadcasted_iota(I32, (tm, tm), 1)
    before = _dot((ri > ci).astype(BF16), both.astype(BF16)) + carry_ref[0:1, :]
    r1 = jnp.sum(before * oh1, axis=-1, keepdims=True)
    r2 = jnp.sum(before * oh2, axis=-1, keepdims=True)
    carry_ref[...] = carry_ref[...] + jnp.sum(both, axis=0, keepdims=True)

    e_ref[...] = jnp.where(lane == 0, i1, jnp.where(lane == 1, i2, 0))
    w_ref[...] = jnp.where(lane == 0, w1, jnp.where(lane == 1, w2, 0.0))
    r_ref[...] = jnp.where(lane == 0, r1, jnp.where(lane == 1, r2, 0.0)).astype(I32)
    cnt_ref[...] = carry_ref[...].astype(I32)


def _router_call(x, w_router):
    t, d = x.shape
    n_exp = w_router.shape[1]
    wr = jnp.zeros((d, LANE), F32).at[:, :n_exp].set(w_router)
    tm = _pick(t, 512)
    row = pl.BlockSpec((tm, LANE), lambda i: (i, 0))
    kern = functools.partial(_router_kernel, n_exp=n_exp, tm=tm)
    return pl.pallas_call(
        kern, grid=(t // tm,),
        in_specs=[pl.BlockSpec((tm, d), lambda i: (i, 0)), pl.BlockSpec((d, LANE), lambda i: (0, 0))],
        out_specs=[row, row, row, pl.BlockSpec((SUBLANE, LANE), lambda i: (0, 0))],
        out_shape=[jax.ShapeDtypeStruct((t, LANE), I32), jax.ShapeDtypeStruct((t, LANE), F32),
                   jax.ShapeDtypeStruct((t, LANE), I32), jax.ShapeDtypeStruct((SUBLANE, LANE), I32)],
        scratch_shapes=[pltpu.VMEM((SUBLANE, LANE), F32)],
        compiler_params=_params("arbitrary"), name="router",
    )(x, wr)


def _row_copy(src_hbm, dst_vmem, src_row, dst_row, sem):
    return pltpu.make_async_copy(src_hbm.at[pl.ds(src_row, 1)], dst_vmem.at[pl.ds(dst_row, 1)], sem)


def _gather_kernel(tok_ref, x_hbm, o_ref, sem, *, rows):
    def start(r, carry):
        _row_copy(x_hbm, o_ref, tok_ref[0, r], r, sem).start()
        return carry

    def wait(r, carry):
        _row_copy(x_hbm, o_ref, 0, r, sem).wait()
        return carry

    lax.fori_loop(0, rows, start, 0)
    lax.fori_loop(0, rows, wait, 0)


def _gather_call(x, tok, rows):
    cap = tok.shape[0]
    d = x.shape[1]
    tok3 = tok.reshape(cap // rows, 1, rows)
    return pl.pallas_call(
        functools.partial(_gather_kernel, rows=rows), grid=(cap // rows,),
        in_specs=[pl.BlockSpec((None, 1, rows), lambda i: (i, 0, 0), memory_space=pltpu.SMEM),
                  pl.BlockSpec(memory_space=pl.ANY)],
        out_specs=pl.BlockSpec((rows, d), lambda i: (i, 0)),
        out_shape=jax.ShapeDtypeStruct((cap, d), x.dtype),
        scratch_shapes=[pltpu.SemaphoreType.DMA(())],
        compiler_params=_params("arbitrary"), name="moe_gather",
    )(tok3, x)


def _combine_ln_kernel(d0_ref, d1_ref, o_hbm, w_ref, x_ref, g_ref, b_ref, o32_ref, o16_ref, buf0, buf1, sem,
                       *, alpha, rows):
    def start(r, carry):
        _row_copy(o_hbm, buf0, d0_ref[0, r], r, sem).start()
        _row_copy(o_hbm, buf1, d1_ref[0, r], r, sem).start()
        return carry

    def wait(r, carry):
        _row_copy(o_hbm, buf0, 0, r, sem).wait()
        _row_copy(o_hbm, buf1, 0, r, sem).wait()
        return carry

    lax.fori_loop(0, rows, start, 0)
    lax.fori_loop(0, rows, wait, 0)
    wts = w_ref[...]
    ffn = wts[:, 0:1] * buf0[...] + wts[:, 1:2] * buf1[...]
    y = _ln_rows(alpha * x_ref[...] + ffn, g_ref[...], b_ref[...])
    o32_ref[...] = y
    o16_ref[...] = y.astype(BF16)


def _combine_ln_call(dest, o_sorted, wts, x, g, b, alpha):
    t, d = x.shape
    rows = _pick(t, 256)
    d0 = dest[:, 0].reshape(t // rows, 1, rows)
    d1 = dest[:, 1].reshape(t // rows, 1, rows)
    idx = pl.BlockSpec((None, 1, rows), lambda i: (i, 0, 0), memory_space=pltpu.SMEM)
    row = pl.BlockSpec((rows, d), lambda i: (i, 0))
    vec = pl.BlockSpec((1, d), lambda i: (0, 0))
    kern = functools.partial(_combine_ln_kernel, alpha=alpha, rows=rows)
    return pl.pallas_call(
        kern, grid=(t // rows,),
        in_specs=[idx, idx, pl.BlockSpec(memory_space=pl.ANY), pl.BlockSpec((rows, LANE), lambda i: (i, 0)),
                  row, vec, vec],
        out_specs=[row, row],
        out_shape=[jax.ShapeDtypeStruct((t, d), F32), jax.ShapeDtypeStruct((t, d), BF16)],
        scratch_shapes=[pltpu.VMEM((rows, d), F32), pltpu.VMEM((rows, d), F32), pltpu.SemaphoreType.DMA(())],
        compiler_params=_params("arbitrary"), name="moe_combine_ln",
    )(d0, d1, o_sorted, wts, x, g.reshape(1, d), b.reshape(1, d))


def _moe_layer(xf, w_router, wg, wu, wd, ln_g, ln_b, alpha):
    t, d = xf.shape
    n_exp = wg.shape[0]
    e_out, w_out, r_out, cnt = _router_call(xf, w_router)
    e = e_out[:, :MOE_TOP_K]
    counts = cnt[0, :n_exp]

    tm = 512 if t * MOE_TOP_K >= 8 * 512 else 128
    n_tiles = -(-(t * MOE_TOP_K) // tm) + n_exp
    cap = n_tiles * tm
    padded = (counts + tm - 1) // tm * tm
    pad_end = jnp.cumsum(padded)
    pad_start = pad_end - padded
    dest = (pad_start[e] + r_out[:, :MOE_TOP_K]).astype(I32)
    tok = jnp.repeat(jnp.arange(t, dtype=I32), MOE_TOP_K)
    slot_tok = jnp.zeros((cap,), I32).at[dest.reshape(-1)].set(tok)
    tile_expert = jnp.minimum(jnp.searchsorted(pad_end, jnp.arange(n_tiles, dtype=I32) * tm, side="right"),
                              n_exp - 1)
    meta = jnp.concatenate([(pad_end[-1:] // tm), tile_expert]).astype(I32)

    xs = _gather_call(xf, slot_tok, _pick(cap, 256))
    h = _swiglu_up_call(meta, xs, wg, wu, tm, "moe_up")
    o_sorted = _moe_down_call(meta, h, wd, tm)
    return _combine_ln_call(dest, o_sorted, w_out, xf, ln_g, ln_b, alpha)


def _trunk(x, conv_bufs, ssm_states, pr):
    bsz, seq, d = x.shape
    t = bsz * seq
    depth = pr["w_main"].shape[0]
    alpha = float((2 * depth) ** 0.25)
    d_inner = pr["w_br_a"].shape[1]
    cdim = pr["conv_w"].shape[2]
    gd = pr["w_br_b"].shape[1]
    kc = pr["conv_w"].shape[1]
    off_xbc = d_inner
    off_u = d_inner + cdim
    off_gate = off_u + 2 * gd
    q_ssd = _pick(seq, 128)
    n_groups = (cdim - d_inner) // (2 * ssm_states.shape[-1])
    gs_ssd = 2 if n_groups % 2 == 0 else 1
    q_gmlp = min(pr["w_s"].shape[2], seq)

    xf, xb = _ln_call(x.reshape(t, d), pr["ln_in_g"], pr["ln_in_b"])
    convs, ssms, vs = [], [], []
    for i in range(depth):
        proj = _mm_call(xb, pr["w_main"][i], F32, "in_proj")
        dt, acum = _dt_call(xf, pr["w_dt"][i], pr["dt_bias"][i], pr["a_log_pad"][i], q_ssd)
        if q_ssd % LANE == 0:
            ya, ssm_i = _ssd_cm_call(proj, dt, acum, conv_bufs[i], ssm_states[i], pr["conv_w"][i], pr["conv_b"][i],
                                     pr["d_skip"][i], pr["ssd_norm_g"][i], bsz=bsz, seq=seq, d_inner=d_inner,
                                     off_z=0, off_xbc=off_xbc, q=q_ssd, gs=gs_ssd)
        else:
            ya, ssm_i = _ssd_call(proj, dt, conv_bufs[i], ssm_states[i], pr["conv_w"][i], pr["conv_b"][i],
                                  pr["a_log"][i], pr["d_skip"][i], pr["ssd_norm_g"][i], bsz=bsz, seq=seq,
                                  d_inner=d_inner, off_z=0, off_xbc=off_xbc, q=q_ssd)
        new_rows = proj.reshape(bsz, seq, -1)[:, max(seq - (kc - 1), 0):, off_xbc:off_xbc + cdim]
        conv_i = jnp.concatenate([conv_bufs[i], new_rows], axis=1)[:, -(kc - 1):]
        yb, vn = _gmlp_call(proj, pr["gmlp_ln_g"][i], pr["gmlp_ln_b"][i], pr["w_s"][i], pr["b_s"][i],
                            bsz=bsz, seq=seq, gd=gd, off_u=off_u, q=q_gmlp)
        merged = _merge_call(ya, yb, proj, pr["w_br_a"][i], pr["w_br_b"][i], pr["b_gate"][i], off_gate=off_gate)
        xf, xb = _mm_res_ln_call(merged, pr["w_o"][i], xf, pr["ln1_g"][i], pr["ln1_b"][i], alpha, "out_proj_ln")
        j = i // 2
        if i % 2 == 0:
            tm = _pick(t, 1024)
            meta = jnp.concatenate([jnp.full((1,), t // tm, I32), jnp.zeros((t // tm,), I32)])
            h = _swiglu_up_call(meta, xb, pr["w_ff_gate"][j][None], pr["w_ff_up"][j][None], tm, "ffn_up")
            xf, xb = _mm_res_ln_call(h, pr["w_ff_down"][j], xf, pr["ln2_g"][i], pr["ln2_b"][i], alpha,
                                     "ffn_down_ln")
        else:
            xf, xb = _moe_layer(xf, pr["w_router"][j], pr["w_moe_gate"][j], pr["w_moe_up"][j],
                                pr["w_moe_down"][j], pr["ln2_g"][i], pr["ln2_b"][i], alpha)
        convs.append(conv_i)
        ssms.append(ssm_i)
        vs.append(vn.reshape(bsz, seq, gd))
    return xf.reshape(bsz, seq, d), jnp.stack(convs), jnp.stack(ssms), jnp.stack(vs)


def kernel(x_prompt, x_sample, cache_conv, state_ssm, ln_in_g, ln_in_b, w_in, conv_w, conv_b, dt_bias, a_log,
           d_skip, ssd_norm_g, gmlp_ln_g, gmlp_ln_b, w_s, b_s, b_gate, w_br_a, w_br_b, w_o, ln1_g, ln1_b,
           w_ff_gate, w_ff_up, w_ff_down, w_router, w_moe_gate, w_moe_up, w_moe_down, ln2_g, ln2_b):
    depth, d_model, _ = w_in.shape
    d_inner = w_br_a.shape[1]
    cdim = conv_w.shape[2]
    heads = a_log.shape[1]
    o_dt = d_inner + cdim
    w_main = jnp.concatenate([w_in[:, :, :o_dt], w_in[:, :, o_dt + heads:]], axis=2).astype(BF16)
    w_dt = jnp.zeros((depth, d_model, LANE), F32).at[:, :, :heads].set(w_in[:, :, o_dt:o_dt + heads])
    dt_b = jnp.zeros((depth, 1, LANE), F32).at[:, 0, :heads].set(dt_bias)
    al_pad = jnp.zeros((depth, 1, LANE), F32).at[:, 0, :heads].set(a_log)
    pr = dict(a_log_pad=al_pad,
        ln_in_g=ln_in_g, ln_in_b=ln_in_b, w_main=w_main, w_dt=w_dt, dt_bias=dt_b, conv_w=conv_w, conv_b=conv_b,
        a_log=a_log, d_skip=d_skip, ssd_norm_g=ssd_norm_g, gmlp_ln_g=gmlp_ln_g, gmlp_ln_b=gmlp_ln_b, w_s=w_s,
        b_s=b_s, b_gate=b_gate, w_br_a=w_br_a.astype(BF16), w_br_b=w_br_b.astype(BF16), w_o=w_o.astype(BF16),
        ln1_g=ln1_g, ln1_b=ln1_b, w_ff_gate=w_ff_gate.astype(BF16), w_ff_up=w_ff_up.astype(BF16),
        w_ff_down=w_ff_down.astype(BF16), w_router=w_router, w_moe_gate=w_moe_gate.astype(BF16),
        w_moe_up=w_moe_up.astype(BF16), w_moe_down=w_moe_down.astype(BF16), ln2_g=ln2_g, ln2_b=ln2_b)
    bp = x_prompt.shape[0]
    zero_conv = jnp.zeros((depth, bp) + cache_conv.shape[2:], x_prompt.dtype)
    zero_ssm = jnp.zeros((depth, bp) + state_ssm.shape[2:], state_ssm.dtype)
    y_prompt, prompt_conv, prompt_ssm, _ = _trunk(x_prompt, zero_conv, zero_ssm, pr)
    y_sample, sample_conv, sample_ssm, sample_v = _trunk(x_sample, cache_conv, state_ssm, pr)
    return (y_prompt, y_sample, prompt_conv, prompt_ssm, sample_conv, sample_ssm, sample_v)
```

```python
import functools

import jax
import jax.numpy as jnp
from jax import lax
from jax.experimental import pallas as pl
from jax.experimental.pallas import tpu as pltpu

F32 = jnp.float32
BF16 = jnp.bfloat16
I32 = jnp.int32
HIGHEST = lax.Precision.HIGHEST

LN_EPS = 1e-5
LANE = 128
SUBLANE = 8
V7X_VMEM_BYTES = 64 * 1024 * 1024
VMEM_LIMIT_BYTES = V7X_VMEM_BYTES - 8 * 1024 * 1024
MOE_TOP_K = 2
_NEG_BIG = -1e30
_TILE_CANDIDATES = (1024, 512, 256, 128, 64, 32, 16, 8)


def _pick(n, cap):
    for c in _TILE_CANDIDATES:
        if c <= cap and n % c == 0:
            return c
    raise ValueError(f"no tile for {n}")


def _params(*sem):
    return pltpu.CompilerParams(dimension_semantics=sem, vmem_limit_bytes=VMEM_LIMIT_BYTES)


def _ln_rows(x, g, b):
    mu = jnp.mean(x, axis=-1, keepdims=True)
    xc = x - mu
    var = jnp.mean(xc * xc, axis=-1, keepdims=True)
    return xc * lax.rsqrt(var + LN_EPS) * g + b


def _sigmoid(x):
    return 0.5 * jnp.tanh(0.5 * x) + 0.5


def _silu(x):
    return x * _sigmoid(x)


def _gelu(x):
    return 0.5 * x * (1.0 + lax.erf(x * (2.0 ** -0.5)))


def _dot(a, b):
    return jnp.dot(a, b, preferred_element_type=F32)


def _dot_exact(a, b):
    return jnp.dot(a, b, precision=HIGHEST, preferred_element_type=F32)


def _ln_kernel(x_ref, g_ref, b_ref, o32_ref, o16_ref):
    y = _ln_rows(x_ref[...], g_ref[...], b_ref[...])
    o32_ref[...] = y
    o16_ref[...] = y.astype(BF16)


def _ln_call(x, g, b):
    t, d = x.shape
    tm = _pick(t, 512)
    row = pl.BlockSpec((tm, d), lambda i: (i, 0))
    vec = pl.BlockSpec((1, d), lambda i: (0, 0))
    return pl.pallas_call(
        _ln_kernel, grid=(t // tm,), in_specs=[row, vec, vec], out_specs=[row, row],
        out_shape=[jax.ShapeDtypeStruct((t, d), F32), jax.ShapeDtypeStruct((t, d), BF16)],
        compiler_params=_params("parallel"), name="ln_in",
    )(x, g.reshape(1, d), b.reshape(1, d))


def _mm_kernel(x_ref, w_ref, o_ref):
    o_ref[...] = _dot(x_ref[...], w_ref[...]).astype(o_ref.dtype)


def _mm_call(x, w, out_dtype, name):
    t, k = x.shape
    n = w.shape[1]
    tm, tn = _pick(t, 1024), _pick(n, 1024)
    return pl.pallas_call(
        _mm_kernel, grid=(t // tm, n // tn),
        in_specs=[pl.BlockSpec((tm, k), lambda i, j: (i, 0)), pl.BlockSpec((k, tn), lambda i, j: (0, j))],
        out_specs=pl.BlockSpec((tm, tn), lambda i, j: (i, j)),
        out_shape=jax.ShapeDtypeStruct((t, n), out_dtype),
        compiler_params=_params("parallel", "parallel"), name=name,
    )(x, w)


def _dt_kernel(x_ref, w_ref, b_ref, al_ref, dt_ref, ac_ref, *, q):
    raw = _dot_exact(x_ref[...], w_ref[...]) + b_ref[...]
    dt = jnp.maximum(raw, 0.0) + jnp.log1p(jnp.exp(-jnp.abs(raw)))
    dt_ref[...] = dt
    a = dt * (-jnp.exp(al_ref[...]))
    ri = lax.broadcasted_iota(I32, (q, q), 0)
    ci = lax.broadcasted_iota(I32, (q, q), 1)
    tril = (ri >= ci).astype(F32)
    for c in range(a.shape[0] // q):
        ac_ref[c * q:(c + 1) * q, :] = _dot_exact(tril, a[c * q:(c + 1) * q, :])


def _dt_call(x, w, bias, a_log, q):
    t, k = x.shape
    n = w.shape[1]
    tm = max(_pick(t, 512), q)
    assert tm % q == 0 and t % tm == 0
    row = pl.BlockSpec((tm, n), lambda i: (i, 0))
    vec = pl.BlockSpec((1, n), lambda i: (0, 0))
    return pl.pallas_call(
        functools.partial(_dt_kernel, q=q), grid=(t // tm,),
        in_specs=[pl.BlockSpec((tm, k), lambda i: (i, 0)), pl.BlockSpec((k, n), lambda i: (0, 0)), vec, vec],
        out_specs=[row, row],
        out_shape=[jax.ShapeDtypeStruct((t, n), F32), jax.ShapeDtypeStruct((t, n), F32)],
        compiler_params=_params("parallel"), name="dt_proj",
    )(x, w, bias, a_log)


_CONV_PAD = SUBLANE


def _ssd_kernel(xs_ref, bm_ref, cm_ref, z_ref, dt_ref, dtt_ref, cbx_ref, cbb_ref, cbc_ref, s0_ref,
                cwx_ref, cwb_ref, cwc_ref, ccx_ref, ccb_ref, ccc_ref, alr_ref, alc_ref, dsk_ref, ng_ref,
                y_ref, sout_ref, xpad_ref, st_ref, *, q, hpg, p, n, kc, nc):
    c = pl.program_id(2)
    w = hpg * p
    tail = kc - 1
    t0 = _CONV_PAD - tail

    @pl.when(c == 0)
    def _():
        xpad_ref[t0:_CONV_PAD, 0:w] = cbx_ref[...]
        xpad_ref[t0:_CONV_PAD, w:w + n] = cbb_ref[...]
        xpad_ref[t0:_CONV_PAD, w + n:w + 2 * n] = cbc_ref[...]
        st_ref[...] = s0_ref[...].reshape(w, n).T

    xpad_ref[_CONV_PAD:_CONV_PAD + q, 0:w] = xs_ref[...]
    xpad_ref[_CONV_PAD:_CONV_PAD + q, w:w + n] = bm_ref[...]
    xpad_ref[_CONV_PAD:_CONV_PAD + q, w + n:w + 2 * n] = cm_ref[...]

    def conv(lo, hi, cw_ref, cc_ref):
        acc = cc_ref[...]
        for k in range(kc):
            acc = acc + xpad_ref[t0 + k:t0 + k + q, lo:hi] * cw_ref[k:k + 1, :]
        return _silu(acc)

    xs = conv(0, w, cwx_ref, ccx_ref)
    bm = conv(w, w + n, cwb_ref, ccb_ref)
    cm = conv(w + n, w + 2 * n, cwc_ref, ccc_ref)
    xpad_ref[t0:_CONV_PAD, :] = xpad_ref[t0 + q:_CONV_PAD + q, :]

    a = dt_ref[...] * (-jnp.exp(alr_ref[...]))
    a_t = dtt_ref[...] * (-jnp.exp(alc_ref[...]))
    ri = lax.broadcasted_iota(I32, (q, q), 0)
    ci = lax.broadcasted_iota(I32, (q, q), 1)
    tril = ri >= ci
    acol = _dot_exact(tril.astype(F32), a)
    arow = _dot_exact(a_t, (ri <= ci).astype(F32))

    hrow = lax.broadcasted_iota(I32, (hpg, w), 0)
    hlane = lax.broadcasted_iota(I32, (hpg, w), 1)
    expand = ((hlane >= hrow * p) & (hlane < (hrow + 1) * p)).astype(F32)
    xdt = xs * _dot_exact(dt_ref[...], expand)
    eacol = _dot_exact(jnp.exp(acol), expand)
    to_end = _dot_exact(jnp.exp(acol[q - 1:q, :] - acol), expand)

    bm16 = bm.astype(BF16)
    cm16 = cm.astype(BF16)
    cb = lax.dot_general(cm16, bm16, (((1,), (1,)), ((), ())), preferred_element_type=F32)
    st = st_ref[...]
    y = _dot(cm16, st.astype(BF16)) * eacol
    lane = lax.broadcasted_iota(I32, (q, w), 1)
    for h in range(hpg):
        seg = acol[:, h:h + 1] - arow[h:h + 1, :]
        m = (cb * jnp.exp(jnp.where(tril, seg, _NEG_BIG))).astype(BF16)
        head = (lane >= h * p) & (lane < (h + 1) * p)
        y = y + _dot(m, jnp.where(head, xdt, 0.0).astype(BF16))
    y = y + dsk_ref[...] * xs
    hz = y * _silu(z_ref[...])
    y_ref[...] = (hz * lax.rsqrt(jnp.mean(hz * hz, axis=-1, keepdims=True) + LN_EPS) * ng_ref[...]).astype(BF16)

    upd = lax.dot_general(bm16, (xdt * to_end).astype(BF16), (((0,), (0,)), ((), ())),
                          preferred_element_type=F32)
    st_ref[...] = st * eacol[q - 1:q, :] + upd

    @pl.when(c == nc - 1)
    def _():
        sout_ref[...] = st_ref[...].T.reshape(hpg, p, n)


def _ssd_call(proj, dt, conv_buf, state0, conv_w, conv_b, a_log, d_skip, norm_g, *, bsz, seq, d_inner, off_z,
              off_xbc, q):
    heads, p, n = state0.shape[1:]
    cdim = conv_w.shape[1]
    kc = conv_w.shape[0]
    groups = (cdim - d_inner) // (2 * n)
    hpg = heads // groups
    w = hpg * p
    nc = seq // q
    t = bsz * seq
    assert w * groups == d_inner and seq % q == 0 and q >= kc - 1
    assert off_z % w == 0 and off_xbc % w == 0 and (off_xbc + d_inner) % n == 0

    dt4 = dt[:, :heads].reshape(bsz, seq, groups, hpg).transpose(0, 2, 1, 3)
    dtt4 = dt4.transpose(0, 1, 3, 2)
    alr = a_log.reshape(groups, 1, hpg)
    alc = a_log.reshape(groups, hpg, 1)
    dsk = jnp.repeat(d_skip, p).reshape(1, d_inner)
    ng = norm_g.reshape(1, d_inner)
    ccb = conv_b.reshape(1, cdim)

    zb, xb = off_z // w, off_xbc // w
    bb, cb_ = (off_xbc + d_inner) // n, (off_xbc + d_inner + groups * n) // n
    row = lambda b, g, c: b * nc + c
    in_specs = [
        pl.BlockSpec((q, w), lambda b, g, c: (row(b, g, c), xb + g)),
        pl.BlockSpec((q, n), lambda b, g, c: (row(b, g, c), bb + g)),
        pl.BlockSpec((q, n), lambda b, g, c: (row(b, g, c), cb_ + g)),
        pl.BlockSpec((q, w), lambda b, g, c: (row(b, g, c), zb + g)),
        pl.BlockSpec((None, None, q, hpg), lambda b, g, c: (b, g, c, 0)),
        pl.BlockSpec((None, None, hpg, q), lambda b, g, c: (b, g, 0, c)),
        pl.BlockSpec((None, kc - 1, w), lambda b, g, c: (b, 0, g)),
        pl.BlockSpec((None, kc - 1, n), lambda b, g, c: (b, 0, d_inner // n + g)),
        pl.BlockSpec((None, kc - 1, n), lambda b, g, c: (b, 0, d_inner // n + groups + g)),
        pl.BlockSpec((None, hpg, p, n), lambda b, g, c: (b, g, 0, 0)),
        pl.BlockSpec((kc, w), lambda b, g, c: (0, g)),
        pl.BlockSpec((kc, n), lambda b, g, c: (0, d_inner // n + g)),
        pl.BlockSpec((kc, n), lambda b, g, c: (0, d_inner // n + groups + g)),
        pl.BlockSpec((1, w), lambda b, g, c: (0, g)),
        pl.BlockSpec((1, n), lambda b, g, c: (0, d_inner // n + g)),
        pl.BlockSpec((1, n), lambda b, g, c: (0, d_inner // n + groups + g)),
        pl.BlockSpec((None, 1, hpg), lambda b, g, c: (g, 0, 0)),
        pl.BlockSpec((None, hpg, 1), lambda b, g, c: (g, 0, 0)),
        pl.BlockSpec((1, w), lambda b, g, c: (0, g)),
        pl.BlockSpec((1, w), lambda b, g, c: (0, g)),
    ]
    out_specs = [
        pl.BlockSpec((q, w), lambda b, g, c: (row(b, g, c), g)),
        pl.BlockSpec((None, hpg, p, n), lambda b, g, c: (b, g, 0, 0)),
    ]
    kern = functools.partial(_ssd_kernel, q=q, hpg=hpg, p=p, n=n, kc=kc, nc=nc)
    return pl.pallas_call(
        kern, grid=(bsz, groups, nc), in_specs=in_specs, out_specs=out_specs,
        out_shape=[jax.ShapeDtypeStruct((t, d_inner), BF16), jax.ShapeDtypeStruct(state0.shape, F32)],
        scratch_shapes=[pltpu.VMEM((_CONV_PAD + q, w + 2 * n), F32), pltpu.VMEM((n, w), F32)],
        compiler_params=_params("parallel", "parallel", "arbitrary"), name="ssd",
    )(proj, proj, proj, proj, dt4, dtt4, conv_buf, conv_buf, conv_buf, state0,
      conv_w, conv_w, conv_w, ccb, ccb, ccb, alr, alc, dsk, ng)


def _ssd_cm_kernel(xs_ref, bm_ref, cm_ref, z_ref, dtt_ref, act_ref, ac_ref, cbx_ref, cbb_ref, cbc_ref, s0_ref,
                   cwx_ref, cwb_ref, cwc_ref, ccx_ref, ccb_ref, ccc_ref, dsk_ref, ng_ref,
                   y_ref, sout_ref, xpad_ref, st_ref, *, q, gs, hpg, p, n, kc, nc):
    c = pl.program_id(2)
    w = hpg * p
    tail = kc - 1
    t0 = _CONV_PAD - tail
    ob, oc = gs * w, gs * (w + n)

    @pl.when(c == 0)
    def _():
        xpad_ref[t0:_CONV_PAD, 0:ob] = cbx_ref[...]
        xpad_ref[t0:_CONV_PAD, ob:oc] = cbb_ref[...]
        xpad_ref[t0:_CONV_PAD, oc:oc + gs * n] = cbc_ref[...]
        st_ref[...] = s0_ref[...].reshape(gs * w, n)

    xpad_ref[_CONV_PAD:_CONV_PAD + q, 0:ob] = xs_ref[...]
    xpad_ref[_CONV_PAD:_CONV_PAD + q, ob:oc] = bm_ref[...]
    xpad_ref[_CONV_PAD:_CONV_PAD + q, oc:oc + gs * n] = cm_ref[...]

    def conv(lo, hi, cw_ref, cc_ref, wlo, whi):
        acc = cc_ref[:, wlo:whi]
        for k in range(kc):
            acc = acc + xpad_ref[t0 + k:t0 + k + q, lo:hi] * cw_ref[k:k + 1, wlo:whi]
        return _silu(acc)

    def rows(v):
        return jnp.concatenate([jnp.broadcast_to(v[h:h + 1, :], (p, v.shape[1])) for h in range(hpg)], axis=0)

    si = lax.broadcasted_iota(I32, (q, q), 0)
    li = lax.broadcasted_iota(I32, (q, q), 1)
    keep = si <= li
    nt = (((1,), (1,)), ((), ()))
    for g in range(gs):
        xs = conv(g * w, (g + 1) * w, cwx_ref, ccx_ref, g * w, (g + 1) * w)
        bm16 = conv(ob + g * n, ob + (g + 1) * n, cwb_ref, ccb_ref, g * n, (g + 1) * n).astype(BF16)
        cm16 = conv(oc + g * n, oc + (g + 1) * n, cwc_ref, ccc_ref, g * n, (g + 1) * n).astype(BF16)
        arow = act_ref[g]
        acol = ac_ref[g]
        ear = jnp.exp(arow)
        te = jnp.exp(arow[:, q - 1:q] - arow)
        xdt_t = xs.T * rows(dtt_ref[g])
        xdt16 = xdt_t.astype(BF16)
        cb_t = lax.dot_general(bm16, cm16, nt, preferred_element_type=F32)
        ys = []
        for h in range(hpg):
            seg = arow[h:h + 1, :] - acol[:, h:h + 1]
            m_t = (cb_t * jnp.exp(jnp.where(keep, seg, _NEG_BIG))).astype(BF16)
            ys.append(_dot(xdt16[h * p:(h + 1) * p, :], m_t))
        st = st_ref[g * w:(g + 1) * w, :]
        y_t = jnp.concatenate(ys, axis=0)
        y_t = y_t + lax.dot_general(st.astype(BF16), cm16, nt, preferred_element_type=F32) * rows(ear)
        y = y_t.T + dsk_ref[:, g * w:(g + 1) * w] * xs
        hz = y * _silu(z_ref[:, g * w:(g + 1) * w])
        y_ref[:, g * w:(g + 1) * w] = (hz * lax.rsqrt(jnp.mean(hz * hz, axis=-1, keepdims=True) + LN_EPS)
                                       * ng_ref[:, g * w:(g + 1) * w]).astype(BF16)
        upd = _dot((xdt_t * rows(te)).astype(BF16), bm16)
        st_ref[g * w:(g + 1) * w, :] = st * rows(ear[:, q - 1:q]) + upd
    xpad_ref[t0:_CONV_PAD, :] = xpad_ref[t0 + q:_CONV_PAD + q, :]

    @pl.when(c == nc - 1)
    def _():
        sout_ref[...] = st_ref[...].reshape(gs * hpg, p, n)


def _ssd_cm_call(proj, dt, acum, conv_buf, state0, conv_w, conv_b, d_skip, norm_g, *, bsz, seq, d_inner, off_z,
                 off_xbc, q, gs):
    heads, p, n = state0.shape[1:]
    cdim = conv_w.shape[1]
    kc = conv_w.shape[0]
    groups = (cdim - d_inner) // (2 * n)
    hpg = heads // groups
    w = hpg * p
    nc = seq // q
    t = bsz * seq
    gw, gn = gs * w, gs * n
    assert w * groups == d_inner and seq % q == 0 and q % LANE == 0 and groups % gs == 0
    assert off_z % gw == 0 and off_xbc % gw == 0 and (off_xbc + d_inner) % gn == 0 and (groups * n) % gn == 0

    def heads_major(v):
        return v[:, :heads].reshape(bsz, seq, groups, hpg).transpose(0, 2, 3, 1)

    dtt4 = heads_major(dt)
    act4 = heads_major(acum)
    ac4 = act4.transpose(0, 1, 3, 2)
    dsk = jnp.repeat(d_skip, p).reshape(1, d_inner)
    ng = norm_g.reshape(1, d_inner)
    ccb = conv_b.reshape(1, cdim)

    zb, xb = off_z // gw, off_xbc // gw
    bb, cb_ = (off_xbc + d_inner) // gn, (off_xbc + d_inner + groups * n) // gn
    wb, wc = d_inner // gn, (d_inner + groups * n) // gn
    row = lambda b, g, c: b * nc + c
    in_specs = [
        pl.BlockSpec((q, gw), lambda b, g, c: (row(b, g, c), xb + g)),
        pl.BlockSpec((q, gn), lambda b, g, c: (row(b, g, c), bb + g)),
        pl.BlockSpec((q, gn), lambda b, g, c: (row(b, g, c), cb_ + g)),
        pl.BlockSpec((q, gw), lambda b, g, c: (row(b, g, c), zb + g)),
        pl.BlockSpec((None, gs, hpg, q), lambda b, g, c: (b, g, 0, c)),
        pl.BlockSpec((None, gs, hpg, q), lambda b, g, c: (b, g, 0, c)),
        pl.BlockSpec((None, gs, q, hpg), lambda b, g, c: (b, g, c, 0)),
        pl.BlockSpec((None, kc - 1, gw), lambda b, g, c: (b, 0, g)),
        pl.BlockSpec((None, kc - 1, gn), lambda b, g, c: (b, 0, wb + g)),
        pl.BlockSpec((None, kc - 1, gn), lambda b, g, c: (b, 0, wc + g)),
        pl.BlockSpec((None, gs * hpg, p, n), lambda b, g, c: (b, g, 0, 0)),
        pl.BlockSpec((kc, gw), lambda b, g, c: (0, g)),
        pl.BlockSpec((kc, gn), lambda b, g, c: (0, wb + g)),
        pl.BlockSpec((kc, gn), lambda b, g, c: (0, wc + g)),
        pl.BlockSpec((1, gw), lambda b, g, c: (0, g)),
        pl.BlockSpec((1, gn), lambda b, g, c: (0, wb + g)),
        pl.BlockSpec((1, gn), lambda b, g, c: (0, wc + g)),
        pl.BlockSpec((1, gw), lambda b, g, c: (0, g)),
        pl.BlockSpec((1, gw), lambda b, g, c: (0, g)),
    ]
    out_specs = [
        pl.BlockSpec((q, gw), lambda b, g, c: (row(b, g, c), g)),
        pl.BlockSpec((None, gs * hpg, p, n), lambda b, g, c: (b, g, 0, 0)),
    ]
    kern = functools.partial(_ssd_cm_kernel, q=q, gs=gs, hpg=hpg, p=p, n=n, kc=kc, nc=nc)
    return pl.pallas_call(
        kern, grid=(bsz, groups // gs, nc), in_specs=in_specs, out_specs=out_specs,
        out_shape=[jax.ShapeDtypeStruct((t, d_inner), BF16), jax.ShapeDtypeStruct(state0.shape, F32)],
        scratch_shapes=[pltpu.VMEM((_CONV_PAD + q, gs * (w + 2 * n)), F32), pltpu.VMEM((gs * w, n), F32)],
        compiler_params=_params("parallel", "parallel", "arbitrary"), name="ssd_cm",
    )(proj, proj, proj, proj, dtt4, act4, ac4, conv_buf, conv_buf, conv_buf, state0,
      conv_w, conv_w, conv_w, ccb, ccb, ccb, dsk, ng)


def _gmlp_kernel(u_ref, v_ref, lg_ref, lb_ref, ws_ref, bs_ref, yb_ref, vn_ref, *, groups, q):
    u = _gelu(u_ref[...])
    vn = _ln_rows(_gelu(v_ref[...]), lg_ref[...], lb_ref[...])
    vn_ref[...] = vn
    vn16 = vn.astype(BF16)
    d = vn.shape[1] // groups
    ri = lax.broadcasted_iota(I32, (q, q), 0)
    ci = lax.broadcasted_iota(I32, (q, q), 1)
    for g in range(groups):
        wg = jnp.where(ri >= ci, ws_ref[g], 0.0).astype(BF16)
        s = _dot(wg, vn16[:, g * d:(g + 1) * d]) + bs_ref[g]
        yb_ref[:, g * d:(g + 1) * d] = (u[:, g * d:(g + 1) * d] * s).astype(BF16)


def _gmlp_call(proj, ln_g, ln_b, w_s, b_s, *, bsz, seq, gd, off_u, q):
    groups = w_s.shape[0]
    t = bsz * seq
    assert off_u % gd == 0 and seq % q == 0 and (gd // groups) % LANE == 0
    ub = off_u // gd
    ws = w_s[:, :q, :q]
    bs = b_s[:, :q, None]
    row = pl.BlockSpec((q, gd), lambda i: (i, 0))
    kern = functools.partial(_gmlp_kernel, groups=groups, q=q)
    return pl.pallas_call(
        kern, grid=(t // q,),
        in_specs=[pl.BlockSpec((q, gd), lambda i: (i, ub)), pl.BlockSpec((q, gd), lambda i: (i, ub + 1)),
                  pl.BlockSpec((1, gd), lambda i: (0, 0)), pl.BlockSpec((1, gd), lambda i: (0, 0)),
                  pl.BlockSpec((groups, q, q), lambda i: (0, 0, 0)),
                  pl.BlockSpec((groups, q, 1), lambda i: (0, 0, 0))],
        out_specs=[row, row],
        out_shape=[jax.ShapeDtypeStruct((t, gd), BF16), jax.ShapeDtypeStruct((t, gd), F32)],
        compiler_params=_params("parallel"), name="gmlp",
    )(proj, proj, ln_g.reshape(1, gd), ln_b.reshape(1, gd), ws, bs)


def _merge_kernel(ya_ref, yb_ref, wa_ref, wb_ref, ga_ref, gb_ref, bga_ref, bgb_ref, o_ref):
    a = _dot(ya_ref[...], wa_ref[...])
    b = _dot(yb_ref[...], wb_ref[...])
    o = _sigmoid(ga_ref[...] + bga_ref[...]) * a + _sigmoid(gb_ref[...] + bgb_ref[...]) * b
    o_ref[...] = o.astype(o_ref.dtype)


def _merge_call(ya, yb, proj, w_a, w_b, b_gate, *, off_gate):
    t, ka = ya.shape
    kb = yb.shape[1]
    d = w_a.shape[1]
    tm, tn = _pick(t, 1024), _pick(d, 512)
    assert off_gate % tn == 0
    gb0 = off_gate // tn
    nd = d // tn
    bg = b_gate.reshape(1, 2 * d)
    return pl.pallas_call(
        _merge_kernel, grid=(t // tm, nd),
        in_specs=[pl.BlockSpec((tm, ka), lambda i, j: (i, 0)), pl.BlockSpec((tm, kb), lambda i, j: (i, 0)),
                  pl.BlockSpec((ka, tn), lambda i, j: (0, j)), pl.BlockSpec((kb, tn), lambda i, j: (0, j)),
                  pl.BlockSpec((tm, tn), lambda i, j: (i, gb0 + j)),
                  pl.BlockSpec((tm, tn), lambda i, j: (i, gb0 + nd + j)),
                  pl.BlockSpec((1, tn), lambda i, j: (0, j)), pl.BlockSpec((1, tn), lambda i, j: (0, nd + j))],
        out_specs=pl.BlockSpec((tm, tn), lambda i, j: (i, j)),
        out_shape=jax.ShapeDtypeStruct((t, d), BF16),
        compiler_params=_params("parallel", "parallel"), name="merge",
    )(ya, yb, w_a, w_b, proj, proj, bg, bg)


def _mm_res_ln_kernel(a_ref, w_ref, x_ref, g_ref, b_ref, o32_ref, o16_ref, acc_ref, *, alpha, nk):
    k = pl.program_id(1)

    @pl.when(k == 0)
    def _():
        acc_ref[...] = jnp.zeros_like(acc_ref)

    acc_ref[...] += _dot(a_ref[...], w_ref[...])

    @pl.when(k == nk - 1)
    def _():
        y = _ln_rows(alpha * x_ref[...] + acc_ref[...], g_ref[...], b_ref[...])
        o32_ref[...] = y
        o16_ref[...] = y.astype(BF16)


def _k_tile(k, cap):
    if k % LANE != 0:
        return k
    best = LANE
    for m in range(1, k // LANE + 1):
        tk = m * LANE
        if k % tk == 0 and tk <= cap:
            best = tk
    return best


def _mm_res_ln_call(a, w, x, g, b, alpha, name):
    t, k = a.shape
    d = w.shape[1]
    tm = _pick(t, 512)
    tk = _k_tile(k, 2048)
    nk = k // tk
    row = pl.BlockSpec((tm, d), lambda i, kk: (i, 0))
    vec = pl.BlockSpec((1, d), lambda i, kk: (0, 0))
    kern = functools.partial(_mm_res_ln_kernel, alpha=alpha, nk=nk)
    return pl.pallas_call(
        kern, grid=(t // tm, nk),
        in_specs=[pl.BlockSpec((tm, tk), lambda i, kk: (i, kk)), pl.BlockSpec((tk, d), lambda i, kk: (kk, 0)),
                  row, vec, vec],
        out_specs=[row, row],
        out_shape=[jax.ShapeDtypeStruct((t, d), F32), jax.ShapeDtypeStruct((t, d), BF16)],
        scratch_shapes=[pltpu.VMEM((tm, d), F32)],
        compiler_params=_params("parallel", "arbitrary"), name=name,
    )(a, w, x, g.reshape(1, d), b.reshape(1, d))


def _swiglu_up_kernel(meta_ref, x_ref, wg_ref, wu_ref, o_ref):
    live = pl.program_id(0) < meta_ref[0]

    @pl.when(live)
    def _():
        x = x_ref[...].astype(BF16)
        g = _dot(x, wg_ref[...])
        u = _dot(x, wu_ref[...])
        o_ref[...] = (_silu(g) * u).astype(o_ref.dtype)

    @pl.when(jnp.logical_not(live))
    def _():
        o_ref[...] = jnp.zeros_like(o_ref)


def _swiglu_up_call(meta, x, wg, wu, tm, name):
    t, k = x.shape
    n = wg.shape[2]
    tn = _pick(n, 1024)
    nn = n // tn

    def live(i, m):
        return jnp.minimum(i, m[0] - 1)

    def col(i, j, m):
        return jnp.where(i < m[0], j, nn - 1)

    wspec = pl.BlockSpec((None, k, tn), lambda i, j, m: (m[1 + live(i, m)], 0, col(i, j, m)))
    grid_spec = pltpu.PrefetchScalarGridSpec(
        num_scalar_prefetch=1, grid=(t // tm, nn),
        in_specs=[pl.BlockSpec((tm, k), lambda i, j, m: (live(i, m), 0)), wspec, wspec],
        out_specs=pl.BlockSpec((tm, tn), lambda i, j, m: (i, j)))
    return pl.pallas_call(
        _swiglu_up_kernel, grid_spec=grid_spec, out_shape=jax.ShapeDtypeStruct((t, n), BF16),
        compiler_params=_params("arbitrary", "arbitrary"), name=name,
    )(meta, x, wg, wu)


def _moe_down_kernel(meta_ref, h_ref, w_ref, o_ref, acc_ref, *, nk):
    k = pl.program_id(1)
    live = pl.program_id(0) < meta_ref[0]

    @pl.when(live)
    def _():
        @pl.when(k == 0)
        def _():
            acc_ref[...] = jnp.zeros_like(acc_ref)

        acc_ref[...] += _dot(h_ref[...], w_ref[...])

        @pl.when(k == nk - 1)
        def _():
            o_ref[...] = acc_ref[...]

    @pl.when(jnp.logical_not(live) & (k == nk - 1))
    def _():
        o_ref[...] = jnp.zeros_like(o_ref)


def _moe_down_call(meta, h, wd, tm):
    t, k = h.shape
    d = wd.shape[2]
    tk = _k_tile(k, 2048)
    nk = k // tk

    def live(i, m):
        return jnp.minimum(i, m[0] - 1)

    def kk(i, k_, m):
        return jnp.where(i < m[0], k_, nk - 1)

    grid_spec = pltpu.PrefetchScalarGridSpec(
        num_scalar_prefetch=1, grid=(t // tm, nk),
        in_specs=[pl.BlockSpec((tm, tk), lambda i, k_, m: (live(i, m), kk(i, k_, m))),
                  pl.BlockSpec((None, tk, d), lambda i, k_, m: (m[1 + live(i, m)], kk(i, k_, m), 0))],
        out_specs=pl.BlockSpec((tm, d), lambda i, k_, m: (i, 0)),
        scratch_shapes=[pltpu.VMEM((tm, d), F32)])
    return pl.pallas_call(
        functools.partial(_moe_down_kernel, nk=nk), grid_spec=grid_spec,
        out_shape=jax.ShapeDtypeStruct((t, d), F32),
        compiler_params=_params("arbitrary", "arbitrary"), name="moe_down",
    )(meta, h, wd)


def _router_kernel(x_ref, wr_ref, e_ref, w_ref, r_ref, cnt_ref, carry_ref, *, n_exp, tm):
    @pl.when(pl.program_id(0) == 0)
    def _():
        carry_ref[...] = jnp.zeros_like(carry_ref)

    logits = _dot_exact(x_ref[...], wr_ref[...])
    lane = lax.broadcasted_iota(I32, logits.shape, 1)
    lg = jnp.where(lane < n_exp, logits, -jnp.inf)
    m1 = jnp.max(lg, axis=-1, keepdims=True)
    i1 = jnp.min(jnp.where(lg == m1, lane, LANE), axis=-1, keepdims=True)
    lg2 = jnp.where(lane == i1, -jnp.inf, lg)
    m2 = jnp.max(lg2, axis=-1, keepdims=True)
    i2 = jnp.min(jnp.where(lg2 == m2, lane, LANE), axis=-1, keepdims=True)
    ex = jnp.exp(m2 - m1)
    w1 = 1.0 / (1.0 + ex)
    w2 = ex / (1.0 + ex)

    oh1 = (lane == i1).astype(F32)
    oh2 = (lane == i2).astype(F32)
    both = oh1 + oh2
    ri = lax.broadcasted_iota(I32, (tm, tm), 0)
    ci = lax.broadcasted_iota(I32, (tm, tm), 1)
    before = _dot((ri > ci).astype(BF16), both.astype(BF16)) + carry_ref[0:1, :]
    r1 = jnp.sum(before * oh1, axis=-1, keepdims=True)
    r2 = jnp.sum(before * oh2, axis=-1, keepdims=True)
    carry_ref[...] = carry_ref[...] + jnp.sum(both, axis=0, keepdims=True)

    e_ref[...] = jnp.where(lane == 0, i1, jnp.where(lane == 1, i2, 0))
    w_ref[...] = jnp.where(lane == 0, w1, jnp.where(lane == 1, w2, 0.0))
    r_ref[...] = jnp.where(lane == 0, r1, jnp.where(lane == 1, r2, 0.0)).astype(I32)
    cnt_ref[...] = carry_ref[...].astype(I32)


def _router_call(x, w_router):
    t, d = x.shape
    n_exp = w_router.shape[1]
    wr = jnp.zeros((d, LANE), F32).at[:, :n_exp].set(w_router)
    tm = _pick(t, 512)
    row = pl.BlockSpec((tm, LANE), lambda i: (i, 0))
    kern = functools.partial(_router_kernel, n_exp=n_exp, tm=tm)
    return pl.pallas_call(
        kern, grid=(t // tm,),
        in_specs=[pl.BlockSpec((tm, d), lambda i: (i, 0)), pl.BlockSpec((d, LANE), lambda i: (0, 0))],
        out_specs=[row, row, row, pl.BlockSpec((SUBLANE, LANE), lambda i: (0, 0))],
        out_shape=[jax.ShapeDtypeStruct((t, LANE), I32), jax.ShapeDtypeStruct((t, LANE), F32),
                   jax.ShapeDtypeStruct((t, LANE), I32), jax.ShapeDtypeStruct((SUBLANE, LANE), I32)],
        scratch_shapes=[pltpu.VMEM((SUBLANE, LANE), F32)],
        compiler_params=_params("arbitrary"), name="router",
    )(x, wr)


_DMA_UNROLL = 8


def _row_copy(src_hbm, dst_vmem, src_row, dst_row, sem):
    return pltpu.make_async_copy(src_hbm.at[pl.ds(src_row, 1)], dst_vmem.at[pl.ds(dst_row, 1)], sem)


def _gather_kernel(tok_ref, x_hbm, o_ref, sem, *, rows):
    def start(r, carry):
        _row_copy(x_hbm, o_ref, tok_ref[0, r], r, sem).start()
        return carry

    lax.fori_loop(0, rows, start, 0, unroll=_DMA_UNROLL)
    pltpu.make_async_copy(x_hbm.at[pl.ds(0, rows)], o_ref, sem).wait()


def _gather_call(x, tok, rows):
    cap = tok.shape[0]
    d = x.shape[1]
    tok3 = tok.reshape(cap // rows, 1, rows)
    return pl.pallas_call(
        functools.partial(_gather_kernel, rows=rows), grid=(cap // rows,),
        in_specs=[pl.BlockSpec((None, 1, rows), lambda i: (i, 0, 0), memory_space=pltpu.SMEM),
                  pl.BlockSpec(memory_space=pl.ANY)],
        out_specs=pl.BlockSpec((rows, d), lambda i: (i, 0)),
        out_shape=jax.ShapeDtypeStruct((cap, d), x.dtype),
        scratch_shapes=[pltpu.SemaphoreType.DMA(())],
        compiler_params=_params("arbitrary"), name="moe_gather",
    )(tok3, x)


def _combine_ln_kernel(d0_ref, d1_ref, o_hbm, w_ref, x_ref, g_ref, b_ref, o32_ref, o16_ref, buf0, buf1, sem,
                       *, alpha, rows):
    def start(r, carry):
        _row_copy(o_hbm, buf0, d0_ref[0, r], r, sem).start()
        _row_copy(o_hbm, buf1, d1_ref[0, r], r, sem).start()
        return carry

    lax.fori_loop(0, rows, start, 0, unroll=_DMA_UNROLL)
    pltpu.make_async_copy(o_hbm.at[pl.ds(0, rows)], buf0, sem).wait()
    pltpu.make_async_copy(o_hbm.at[pl.ds(0, rows)], buf1, sem).wait()
    wts = w_ref[...]
    ffn = wts[:, 0:1] * buf0[...] + wts[:, 1:2] * buf1[...]
    y = _ln_rows(alpha * x_ref[...] + ffn, g_ref[...], b_ref[...])
    o32_ref[...] = y
    o16_ref[...] = y.astype(BF16)


def _combine_ln_call(dest, o_sorted, wts, x, g, b, alpha):
    t, d = x.shape
    rows = _pick(t, 256)
    d0 = dest[:, 0].reshape(t // rows, 1, rows)
    d1 = dest[:, 1].reshape(t // rows, 1, rows)
    idx = pl.BlockSpec((None, 1, rows), lambda i: (i, 0, 0), memory_space=pltpu.SMEM)
    row = pl.BlockSpec((rows, d), lambda i: (i, 0))
    vec = pl.BlockSpec((1, d), lambda i: (0, 0))
    kern = functools.partial(_combine_ln_kernel, alpha=alpha, rows=rows)
    return pl.pallas_call(
        kern, grid=(t // rows,),
        in_specs=[idx, idx, pl.BlockSpec(memory_space=pl.ANY), pl.BlockSpec((rows, LANE), lambda i: (i, 0)),
                  row, vec, vec],
        out_specs=[row, row],
        out_shape=[jax.ShapeDtypeStruct((t, d), F32), jax.ShapeDtypeStruct((t, d), BF16)],
        scratch_shapes=[pltpu.VMEM((rows, d), F32), pltpu.VMEM((rows, d), F32), pltpu.SemaphoreType.DMA(())],
        compiler_params=_params("arbitrary"), name="moe_combine_ln",
    )(d0, d1, o_sorted, wts, x, g.reshape(1, d), b.reshape(1, d))


def _moe_layer(xf, w_router, wg, wu, wd, ln_g, ln_b, alpha):
    t, d = xf.shape
    n_exp = wg.shape[0]
    e_out, w_out, r_out, cnt = _router_call(xf, w_router)
    e = e_out[:, :MOE_TOP_K]
    counts = cnt[0, :n_exp]

    tm = 512 if t * MOE_TOP_K >= 8 * 512 else 128
    n_tiles = -(-(t * MOE_TOP_K) // tm) + n_exp
    cap = n_tiles * tm
    padded = (counts + tm - 1) // tm * tm
    pad_end = jnp.cumsum(padded)
    pad_start = pad_end - padded
    dest = (pad_start[e] + r_out[:, :MOE_TOP_K]).astype(I32)
    tok = jnp.repeat(jnp.arange(t, dtype=I32), MOE_TOP_K)
    slot_tok = jnp.zeros((cap,), I32).at[dest.reshape(-1)].set(tok)
    tile_expert = jnp.minimum(jnp.searchsorted(pad_end, jnp.arange(n_tiles, dtype=I32) * tm, side="right"),
                              n_exp - 1)
    meta = jnp.concatenate([(pad_end[-1:] // tm), tile_expert]).astype(I32)

    xs = _gather_call(xf, slot_tok, _pick(cap, 256))
    h = _swiglu_up_call(meta, xs, wg, wu, tm, "moe_up")
    o_sorted = _moe_down_call(meta, h, wd, tm)
    return _combine_ln_call(dest, o_sorted, w_out, xf, ln_g, ln_b, alpha)


def _trunk(x, conv_bufs, ssm_states, pr):
    bsz, seq, d = x.shape
    t = bsz * seq
    depth = pr["w_main"].shape[0]
    alpha = float((2 * depth) ** 0.25)
    d_inner = pr["w_br_a"].shape[1]
    cdim = pr["conv_w"].shape[2]
    gd = pr["w_br_b"].shape[1]
    kc = pr["conv_w"].shape[1]
    off_xbc = d_inner
    off_u = d_inner + cdim
    off_gate = off_u + 2 * gd
    q_ssd = _pick(seq, 128)
    n_groups = (cdim - d_inner) // (2 * ssm_states.shape[-1])
    gs_ssd = 2 if n_groups % 2 == 0 else 1
    q_gmlp = min(pr["w_s"].shape[2], seq)

    xf, xb = _ln_call(x.reshape(t, d), pr["ln_in_g"], pr["ln_in_b"])
    convs, ssms, vs = [], [], []
    for i in range(depth):
        proj = _mm_call(xb, pr["w_main"][i], F32, "in_proj")
        dt, acum = _dt_call(xf, pr["w_dt"][i], pr["dt_bias"][i], pr["a_log_pad"][i], q_ssd)
        if q_ssd % LANE == 0:
            ya, ssm_i = _ssd_cm_call(proj, dt, acum, conv_bufs[i], ssm_states[i], pr["conv_w"][i], pr["conv_b"][i],
                                     pr["d_skip"][i], pr["ssd_norm_g"][i], bsz=bsz, seq=seq, d_inner=d_inner,
                                     off_z=0, off_xbc=off_xbc, q=q_ssd, gs=gs_ssd)
        else:
            ya, ssm_i = _ssd_call(proj, dt, conv_bufs[i], ssm_states[i], pr["conv_w"][i], pr["conv_b"][i],
                                  pr["a_log"][i], pr["d_skip"][i], pr["ssd_norm_g"][i], bsz=bsz, seq=seq,
                                  d_inner=d_inner, off_z=0, off_xbc=off_xbc, q=q_ssd)
        new_rows = proj.reshape(bsz, seq, -1)[:, max(seq - (kc - 1), 0):, off_xbc:off_xbc + cdim]
        conv_i = jnp.concatenate([conv_bufs[i], new_rows], axis=1)[:, -(kc - 1):]
        yb, vn = _gmlp_call(proj, pr["gmlp_ln_g"][i], pr["gmlp_ln_b"][i], pr["w_s"][i], pr["b_s"][i],
                            bsz=bsz, seq=seq, gd=gd, off_u=off_u, q=q_gmlp)
        merged = _merge_call(ya, yb, proj, pr["w_br_a"][i], pr["w_br_b"][i], pr["b_gate"][i], off_gate=off_gate)
        xf, xb = _mm_res_ln_call(merged, pr["w_o"][i], xf, pr["ln1_g"][i], pr["ln1_b"][i], alpha, "out_proj_ln")
        j = i // 2
        if i % 2 == 0:
            tm = _pick(t, 1024)
            meta = jnp.concatenate([jnp.full((1,), t // tm, I32), jnp.zeros((t // tm,), I32)])
            h = _swiglu_up_call(meta, xb, pr["w_ff_gate"][j][None], pr["w_ff_up"][j][None], tm, "ffn_up")
            xf, xb = _mm_res_ln_call(h, pr["w_ff_down"][j], xf, pr["ln2_g"][i], pr["ln2_b"][i], alpha,
                                     "ffn_down_ln")
        else:
            xf, xb = _moe_layer(xf, pr["w_router"][j], pr["w_moe_gate"][j], pr["w_moe_up"][j],
                                pr["w_moe_down"][j], pr["ln2_g"][i], pr["ln2_b"][i], alpha)
        convs.append(conv_i)
        ssms.append(ssm_i)
        vs.append(vn.reshape(bsz, seq, gd))
    return xf.reshape(bsz, seq, d), jnp.stack(convs), jnp.stack(ssms), jnp.stack(vs)


def kernel(x_prompt, x_sample, cache_conv, state_ssm, ln_in_g, ln_in_b, w_in, conv_w, conv_b, dt_bias, a_log,
           d_skip, ssd_norm_g, gmlp_ln_g, gmlp_ln_b, w_s, b_s, b_gate, w_br_a, w_br_b, w_o, ln1_g, ln1_b,
           w_ff_gate, w_ff_up, w_ff_down, w_router, w_moe_gate, w_moe_up, w_moe_down, ln2_g, ln2_b):
    depth, d_model, _ = w_in.shape
    d_inner = w_br_a.shape[1]
    cdim = conv_w.shape[2]
    heads = a_log.shape[1]
    o_dt = d_inner + cdim
    w_in16 = w_in.astype(BF16)
    w_main = jnp.concatenate([w_in16[:, :, :o_dt], w_in16[:, :, o_dt + heads:]], axis=2)
    w_dt = jnp.zeros((depth, d_model, LANE), F32).at[:, :, :heads].set(w_in[:, :, o_dt:o_dt + heads])
    dt_b = jnp.zeros((depth, 1, LANE), F32).at[:, 0, :heads].set(dt_bias)
    al_pad = jnp.zeros((depth, 1, LANE), F32).at[:, 0, :heads].set(a_log)
    pr = dict(a_log_pad=al_pad,
        ln_in_g=ln_in_g, ln_in_b=ln_in_b, w_main=w_main, w_dt=w_dt, dt_bias=dt_b, conv_w=conv_w, conv_b=conv_b,
        a_log=a_log, d_skip=d_skip, ssd_norm_g=ssd_norm_g, gmlp_ln_g=gmlp_ln_g, gmlp_ln_b=gmlp_ln_b, w_s=w_s,
        b_s=b_s, b_gate=b_gate, w_br_a=w_br_a.astype(BF16), w_br_b=w_br_b.astype(BF16), w_o=w_o.astype(BF16),
        ln1_g=ln1_g, ln1_b=ln1_b, w_ff_gate=w_ff_gate.astype(BF16), w_ff_up=w_ff_up.astype(BF16),
        w_ff_down=w_ff_down.astype(BF16), w_router=w_router, w_moe_gate=w_moe_gate.astype(BF16),
        w_moe_up=w_moe_up.astype(BF16), w_moe_down=w_moe_down.astype(BF16), ln2_g=ln2_g, ln2_b=ln2_b)
    bp = x_prompt.shape[0]
    zero_conv = jnp.zeros((depth, bp) + cache_conv.shape[2:], x_prompt.dtype)
    zero_ssm = jnp.zeros((depth, bp) + state_ssm.shape[2:], state_ssm.dtype)
    y_prompt, prompt_conv, prompt_ssm, _ = _trunk(x_prompt, zero_conv, zero_ssm, pr)
    y_sample, sample_conv, sample_ssm, sample_v = _trunk(x_sample, cache_conv, state_ssm, pr)
    return (y_prompt, y_sample, prompt_conv, prompt_ssm, sample_conv, sample_ssm, sample_v)
```

```python
import functools

import jax
import jax.numpy as jnp
from jax import lax
from jax.experimental import pallas as pl
from jax.experimental.pallas import tpu as pltpu

F32 = jnp.float32
BF16 = jnp.bfloat16
I32 = jnp.int32
HIGHEST = lax.Precision.HIGHEST

LN_EPS = 1e-5
LANE = 128
SUBLANE = 8
V7X_VMEM_BYTES = 64 * 1024 * 1024
VMEM_LIMIT_BYTES = V7X_VMEM_BYTES - 8 * 1024 * 1024
MOE_TOP_K = 2
_NEG_BIG = -1e30
_TILE_CANDIDATES = (1024, 512, 256, 128, 64, 32, 16, 8)


def _pick(n, cap):
    for c in _TILE_CANDIDATES:
        if c <= cap and n % c == 0:
            return c
    raise ValueError(f"no tile for {n}")


def _params(*sem):
    return pltpu.CompilerParams(dimension_semantics=sem, vmem_limit_bytes=VMEM_LIMIT_BYTES)


def _ln_rows(x, g, b):
    mu = jnp.mean(x, axis=-1, keepdims=True)
    xc = x - mu
    var = jnp.mean(xc * xc, axis=-1, keepdims=True)
    return xc * lax.rsqrt(var + LN_EPS) * g + b


def _sigmoid(x):
    return 0.5 * jnp.tanh(0.5 * x) + 0.5


def _silu(x):
    return x * _sigmoid(x)


def _gelu(x):
    return 0.5 * x * (1.0 + lax.erf(x * (2.0 ** -0.5)))


def _dot(a, b):
    return jnp.dot(a, b, preferred_element_type=F32)


def _dot_exact(a, b):
    return jnp.dot(a, b, precision=HIGHEST, preferred_element_type=F32)


def _ln_kernel(x_ref, g_ref, b_ref, o32_ref, o16_ref):
    y = _ln_rows(x_ref[...], g_ref[...], b_ref[...])
    o32_ref[...] = y
    o16_ref[...] = y.astype(BF16)


def _ln_call(x, g, b):
    t, d = x.shape
    tm = _pick(t, 512)
    row = pl.BlockSpec((tm, d), lambda i: (i, 0))
    vec = pl.BlockSpec((1, d), lambda i: (0, 0))
    return pl.pallas_call(
        _ln_kernel, grid=(t // tm,), in_specs=[row, vec, vec], out_specs=[row, row],
        out_shape=[jax.ShapeDtypeStruct((t, d), F32), jax.ShapeDtypeStruct((t, d), BF16)],
        compiler_params=_params("parallel"), name="ln_in",
    )(x, g.reshape(1, d), b.reshape(1, d))


_CONV_PAD = SUBLANE

_ACTIVATIONS = {"none": lambda r: r, "silu": _silu, "gelu": _gelu}


def _proj_kernel(x_ref, w_ref, o_ref, *, act):
    o_ref[...] = _ACTIVATIONS[act](_dot(x_ref[...], w_ref[...])).astype(o_ref.dtype)


def _proj_gate_kernel(x_ref, w_ref, b_ref, o_ref):
    o_ref[...] = _sigmoid(_dot(x_ref[...], w_ref[...]) + b_ref[...]).astype(o_ref.dtype)


def _proj_conv_kernel(x_ref, w_ref, cw_ref, cc_ref, cbuf_ref, o_ref, tail_ref, halo_ref, *, kc, tiles_per_seq):
    i, j = pl.program_id(0), pl.program_id(1)
    tm = x_ref.shape[0]
    r = _dot(x_ref[...], w_ref[...])
    before = jnp.where(lax.rem(i, tiles_per_seq) == 0, cbuf_ref[...], halo_ref[j])
    nb = tm // _CONV_PAD
    blocks = r.reshape(nb, _CONV_PAD, r.shape[1])
    prev_blocks = jnp.concatenate([before[None], blocks[:nb - 1]], axis=0)
    sub = lax.broadcasted_iota(I32, blocks.shape, 1)
    acc = cc_ref[...] + r * cw_ref[kc - 1:kc, :]
    for s in range(1, kc):
        shifted = jnp.where(sub < s, pltpu.roll(prev_blocks, s, 1), pltpu.roll(blocks, s, 1))
        acc = acc + shifted.reshape(r.shape) * cw_ref[kc - 1 - s:kc - s, :]
    o_ref[...] = _silu(acc)
    last_rows = r[tm - _CONV_PAD:tm, :]
    tail_ref[...] = last_rows
    halo_ref[j] = last_rows


def _region_tiles(t, n, col0):
    tm = _pick(t, 1024)
    tn = next(c for c in _TILE_CANDIDATES if n % c == 0 and col0 % c == 0)
    return tm, tn


def _proj_call(x, w, layer, col0, n, act, out_dtype, name, bias=None):
    t, k = x.shape
    tm, tn = _region_tiles(t, n, col0)
    c0 = col0 // tn
    in_specs = [pl.BlockSpec((tm, k), lambda i, j: (i, 0)), pl.BlockSpec((None, k, tn), lambda i, j: (layer, 0, c0 + j))]
    args = [x, w]
    if bias is None:
        kern = functools.partial(_proj_kernel, act=act)
    else:
        kern = _proj_gate_kernel
        in_specs.append(pl.BlockSpec((1, tn), lambda i, j: (0, j)))
        args.append(bias.reshape(1, n))
    return pl.pallas_call(
        kern, grid=(t // tm, n // tn), in_specs=in_specs,
        out_specs=pl.BlockSpec((tm, tn), lambda i, j: (i, j)),
        out_shape=jax.ShapeDtypeStruct((t, n), out_dtype),
        compiler_params=_params("parallel", "parallel"), name=name,
    )(*args)


def _proj_conv_call(x, w, layer, col0, conv_w, conv_b, conv_buf, seq):
    t, k = x.shape
    kc, n = conv_w.shape
    tm, tn = _pick(seq, 1024), _region_tiles(t, n, col0)[1]
    assert t % tm == 0 and kc - 1 <= _CONV_PAD <= tm
    tiles_per_seq = seq // tm
    c0 = col0 // tn
    cbuf = jnp.pad(conv_buf, ((0, 0), (_CONV_PAD - (kc - 1), 0), (0, 0)))
    kern = functools.partial(_proj_conv_kernel, kc=kc, tiles_per_seq=tiles_per_seq)
    return pl.pallas_call(
        kern, grid=(t // tm, n // tn),
        in_specs=[pl.BlockSpec((tm, k), lambda i, j: (i, 0)),
                  pl.BlockSpec((None, k, tn), lambda i, j: (layer, 0, c0 + j)),
                  pl.BlockSpec((kc, tn), lambda i, j: (0, j)),
                  pl.BlockSpec((1, tn), lambda i, j: (0, j)),
                  pl.BlockSpec((None, _CONV_PAD, tn), lambda i, j: (i // tiles_per_seq, 0, j))],
        out_specs=[pl.BlockSpec((tm, tn), lambda i, j: (i, j)),
                   pl.BlockSpec((None, _CONV_PAD, tn), lambda i, j: (i, 0, j))],
        out_shape=[jax.ShapeDtypeStruct((t, n), F32), jax.ShapeDtypeStruct((t // tm, _CONV_PAD, n), F32)],
        scratch_shapes=[pltpu.VMEM((n // tn, _CONV_PAD, tn), F32)],
        compiler_params=_params("arbitrary", "arbitrary"), name="in_proj_conv",
    )(x, w, conv_w, conv_b.reshape(1, n), cbuf)


def _dt_kernel(x_ref, w_ref, b_ref, al_ref, dt_ref, ac_ref, *, q):
    raw = _dot_exact(x_ref[...], w_ref[...]) + b_ref[...]
    dt = jnp.maximum(raw, 0.0) + jnp.log1p(jnp.exp(-jnp.abs(raw)))
    dt_ref[...] = dt
    a = dt * (-jnp.exp(al_ref[...]))
    ri = lax.broadcasted_iota(I32, (q, q), 0)
    ci = lax.broadcasted_iota(I32, (q, q), 1)
    tril = (ri >= ci).astype(F32)
    for c in range(a.shape[0] // q):
        ac_ref[c * q:(c + 1) * q, :] = _dot_exact(tril, a[c * q:(c + 1) * q, :])


def _dt_call(x, w, bias, a_log, q):
    t, k = x.shape
    n = w.shape[1]
    tm = max(_pick(t, 512), q)
    assert tm % q == 0 and t % tm == 0
    row = pl.BlockSpec((tm, n), lambda i: (i, 0))
    vec = pl.BlockSpec((1, n), lambda i: (0, 0))
    return pl.pallas_call(
        functools.partial(_dt_kernel, q=q), grid=(t // tm,),
        in_specs=[pl.BlockSpec((tm, k), lambda i: (i, 0)), pl.BlockSpec((k, n), lambda i: (0, 0)), vec, vec],
        out_specs=[row, row],
        out_shape=[jax.ShapeDtypeStruct((t, n), F32), jax.ShapeDtypeStruct((t, n), F32)],
        compiler_params=_params("parallel"), name="dt_proj",
    )(x, w, bias, a_log)


def _ssd_kernel(xs_ref, bm_ref, cm_ref, sz_ref, dt_ref, dtt_ref, cbx_ref, cbb_ref, cbc_ref, s0_ref,
                cwx_ref, cwb_ref, cwc_ref, ccx_ref, ccb_ref, ccc_ref, alr_ref, alc_ref, dsk_ref, ng_ref,
                y_ref, sout_ref, xpad_ref, st_ref, *, q, hpg, p, n, kc, nc):
    c = pl.program_id(2)
    w = hpg * p
    tail = kc - 1
    t0 = _CONV_PAD - tail

    @pl.when(c == 0)
    def _():
        xpad_ref[t0:_CONV_PAD, 0:w] = cbx_ref[...]
        xpad_ref[t0:_CONV_PAD, w:w + n] = cbb_ref[...]
        xpad_ref[t0:_CONV_PAD, w + n:w + 2 * n] = cbc_ref[...]
        st_ref[...] = s0_ref[...].reshape(w, n).T

    xpad_ref[_CONV_PAD:_CONV_PAD + q, 0:w] = xs_ref[...]
    xpad_ref[_CONV_PAD:_CONV_PAD + q, w:w + n] = bm_ref[...]
    xpad_ref[_CONV_PAD:_CONV_PAD + q, w + n:w + 2 * n] = cm_ref[...]

    def conv(lo, hi, cw_ref, cc_ref):
        acc = cc_ref[...]
        for k in range(kc):
            acc = acc + xpad_ref[t0 + k:t0 + k + q, lo:hi] * cw_ref[k:k + 1, :]
        return _silu(acc)

    xs = conv(0, w, cwx_ref, ccx_ref)
    bm = conv(w, w + n, cwb_ref, ccb_ref)
    cm = conv(w + n, w + 2 * n, cwc_ref, ccc_ref)
    xpad_ref[t0:_CONV_PAD, :] = xpad_ref[t0 + q:_CONV_PAD + q, :]

    a = dt_ref[...] * (-jnp.exp(alr_ref[...]))
    a_t = dtt_ref[...] * (-jnp.exp(alc_ref[...]))
    ri = lax.broadcasted_iota(I32, (q, q), 0)
    ci = lax.broadcasted_iota(I32, (q, q), 1)
    tril = ri >= ci
    acol = _dot_exact(tril.astype(F32), a)
    arow = _dot_exact(a_t, (ri <= ci).astype(F32))

    hrow = lax.broadcasted_iota(I32, (hpg, w), 0)
    hlane = lax.broadcasted_iota(I32, (hpg, w), 1)
    expand = ((hlane >= hrow * p) & (hlane < (hrow + 1) * p)).astype(F32)
    xdt = xs * _dot_exact(dt_ref[...], expand)
    eacol = _dot_exact(jnp.exp(acol), expand)
    to_end = _dot_exact(jnp.exp(acol[q - 1:q, :] - acol), expand)

    bm16 = bm.astype(BF16)
    cm16 = cm.astype(BF16)
    cb = lax.dot_general(cm16, bm16, (((1,), (1,)), ((), ())), preferred_element_type=F32)
    st = st_ref[...]
    y = _dot(cm16, st.astype(BF16)) * eacol
    lane = lax.broadcasted_iota(I32, (q, w), 1)
    for h in range(hpg):
        seg = acol[:, h:h + 1] - arow[h:h + 1, :]
        m = (cb * jnp.exp(jnp.where(tril, seg, _NEG_BIG))).astype(BF16)
        head = (lane >= h * p) & (lane < (h + 1) * p)
        y = y + _dot(m, jnp.where(head, xdt, 0.0).astype(BF16))
    y = y + dsk_ref[...] * xs
    hz = y * sz_ref[...]
    y_ref[...] = (hz * lax.rsqrt(jnp.mean(hz * hz, axis=-1, keepdims=True) + LN_EPS) * ng_ref[...]).astype(BF16)

    upd = lax.dot_general(bm16, (xdt * to_end).astype(BF16), (((0,), (0,)), ((), ())),
                          preferred_element_type=F32)
    st_ref[...] = st * eacol[q - 1:q, :] + upd

    @pl.when(c == nc - 1)
    def _():
        sout_ref[...] = st_ref[...].T.reshape(hpg, p, n)


def _ssd_call(xbc, sz, dt, conv_buf, state0, conv_w, conv_b, a_log, d_skip, norm_g, *, bsz, seq, q):
    heads, p, n = state0.shape[1:]
    kc, cdim = conv_w.shape
    d_inner = sz.shape[1]
    groups = (cdim - d_inner) // (2 * n)
    hpg = heads // groups
    w = hpg * p
    nc = seq // q
    t = bsz * seq
    assert w * groups == d_inner and seq % q == 0 and q >= kc - 1 and d_inner % n == 0

    dt4 = dt[:, :heads].reshape(bsz, seq, groups, hpg).transpose(0, 2, 1, 3)
    dtt4 = dt4.transpose(0, 1, 3, 2)
    alr = a_log.reshape(groups, 1, hpg)
    alc = a_log.reshape(groups, hpg, 1)
    dsk = jnp.repeat(d_skip, p).reshape(1, d_inner)
    ng = norm_g.reshape(1, d_inner)
    ccb = conv_b.reshape(1, cdim)

    bb, cb_ = d_inner // n, d_inner // n + groups
    row = lambda b, g, c: b * nc + c
    in_specs = [
        pl.BlockSpec((q, w), lambda b, g, c: (row(b, g, c), g)),
        pl.BlockSpec((q, n), lambda b, g, c: (row(b, g, c), bb + g)),
        pl.BlockSpec((q, n), lambda b, g, c: (row(b, g, c), cb_ + g)),
        pl.BlockSpec((q, w), lambda b, g, c: (row(b, g, c), g)),
        pl.BlockSpec((None, None, q, hpg), lambda b, g, c: (b, g, c, 0)),
        pl.BlockSpec((None, None, hpg, q), lambda b, g, c: (b, g, 0, c)),
        pl.BlockSpec((None, kc - 1, w), lambda b, g, c: (b, 0, g)),
        pl.BlockSpec((None, kc - 1, n), lambda b, g, c: (b, 0, d_inner // n + g)),
        pl.BlockSpec((None, kc - 1, n), lambda b, g, c: (b, 0, d_inner // n + groups + g)),
        pl.BlockSpec((None, hpg, p, n), lambda b, g, c: (b, g, 0, 0)),
        pl.BlockSpec((kc, w), lambda b, g, c: (0, g)),
        pl.BlockSpec((kc, n), lambda b, g, c: (0, d_inner // n + g)),
        pl.BlockSpec((kc, n), lambda b, g, c: (0, d_inner // n + groups + g)),
        pl.BlockSpec((1, w), lambda b, g, c: (0, g)),
        pl.BlockSpec((1, n), lambda b, g, c: (0, d_inner // n + g)),
        pl.BlockSpec((1, n), lambda b, g, c: (0, d_inner // n + groups + g)),
        pl.BlockSpec((None, 1, hpg), lambda b, g, c: (g, 0, 0)),
        pl.BlockSpec((None, hpg, 1), lambda b, g, c: (g, 0, 0)),
        pl.BlockSpec((1, w), lambda b, g, c: (0, g)),
        pl.BlockSpec((1, w), lambda b, g, c: (0, g)),
    ]
    out_specs = [
        pl.BlockSpec((q, w), lambda b, g, c: (row(b, g, c), g)),
        pl.BlockSpec((None, hpg, p, n), lambda b, g, c: (b, g, 0, 0)),
    ]
    kern = functools.partial(_ssd_kernel, q=q, hpg=hpg, p=p, n=n, kc=kc, nc=nc)
    return pl.pallas_call(
        kern, grid=(bsz, groups, nc), in_specs=in_specs, out_specs=out_specs,
        out_shape=[jax.ShapeDtypeStruct((t, d_inner), BF16), jax.ShapeDtypeStruct(state0.shape, F32)],
        scratch_shapes=[pltpu.VMEM((_CONV_PAD + q, w + 2 * n), F32), pltpu.VMEM((n, w), F32)],
        compiler_params=_params("parallel", "parallel", "arbitrary"), name="ssd",
    )(xbc, xbc, xbc, sz, dt4, dtt4, conv_buf, conv_buf, conv_buf, state0,
      conv_w, conv_w, conv_w, ccb, ccb, ccb, alr, alc, dsk, ng)


def _ssd_cm_kernel(xs_ref, bm_ref, cm_ref, sz_ref, dtt_ref, act_ref, ac_ref, s0_ref, dsk_ref, ng_ref,
                   y_ref, sout_ref, st_ref, *, q, gs, hpg, p, n, nc):
    c = pl.program_id(2)
    w = hpg * p

    @pl.when(c == 0)
    def _():
        st_ref[...] = s0_ref[...].reshape(gs * w, n)

    def rows(v):
        return jnp.concatenate([jnp.broadcast_to(v[h:h + 1, :], (p, v.shape[1])) for h in range(hpg)], axis=0)

    si = lax.broadcasted_iota(I32, (q, q), 0)
    li = lax.broadcasted_iota(I32, (q, q), 1)
    keep = si <= li
    nt = (((1,), (1,)), ((), ()))
    for g in range(gs):
        xs = xs_ref[:, g * w:(g + 1) * w]
        bm16 = bm_ref[:, g * n:(g + 1) * n].astype(BF16)
        cm16 = cm_ref[:, g * n:(g + 1) * n].astype(BF16)
        arow = act_ref[g]
        acol = ac_ref[g]
        ear = jnp.exp(arow)
        te = jnp.exp(arow[:, q - 1:q] - arow)
        xdt_t = xs.T * rows(dtt_ref[g])
        xdt16 = xdt_t.astype(BF16)
        cb_t = lax.dot_general(bm16, cm16, nt, preferred_element_type=F32)
        ys = []
        for h in range(hpg):
            seg = arow[h:h + 1, :] - acol[:, h:h + 1]
            m_t = (cb_t * jnp.exp(jnp.where(keep, seg, _NEG_BIG))).astype(BF16)
            ys.append(_dot(xdt16[h * p:(h + 1) * p, :], m_t))
        st = st_ref[g * w:(g + 1) * w, :]
        y_t = jnp.concatenate(ys, axis=0)
        y_t = y_t + lax.dot_general(st.astype(BF16), cm16, nt, preferred_element_type=F32) * rows(ear)
        y = y_t.T + dsk_ref[:, g * w:(g + 1) * w] * xs
        hz = y * sz_ref[:, g * w:(g + 1) * w]
        y_ref[:, g * w:(g + 1) * w] = (hz * lax.rsqrt(jnp.mean(hz * hz, axis=-1, keepdims=True) + LN_EPS)
                                       * ng_ref[:, g * w:(g + 1) * w]).astype(BF16)
        upd = _dot((xdt_t * rows(te)).astype(BF16), bm16)
        st_ref[g * w:(g + 1) * w, :] = st * rows(ear[:, q - 1:q]) + upd

    @pl.when(c == nc - 1)
    def _():
        sout_ref[...] = st_ref[...].reshape(gs * hpg, p, n)


def _ssd_cm_call(xbc, sz, dt, acum, state0, d_skip, norm_g, *, bsz, seq, q, gs):
    heads, p, n = state0.shape[1:]
    cdim = xbc.shape[1]
    d_inner = sz.shape[1]
    groups = (cdim - d_inner) // (2 * n)
    hpg = heads // groups
    w = hpg * p
    nc = seq // q
    t = bsz * seq
    gw, gn = gs * w, gs * n
    assert w * groups == d_inner and seq % q == 0 and q % LANE == 0 and groups % gs == 0
    assert d_inner % gn == 0 and (groups * n) % gn == 0

    def heads_major(v):
        return v[:, :heads].reshape(bsz, seq, groups, hpg).transpose(0, 2, 3, 1)

    dtt4 = heads_major(dt)
    act4 = heads_major(acum)
    ac4 = act4.transpose(0, 1, 3, 2)
    dsk = jnp.repeat(d_skip, p).reshape(1, d_inner)
    ng = norm_g.reshape(1, d_inner)

    bb, cb_ = d_inner // gn, (d_inner + groups * n) // gn
    row = lambda b, g, c: b * nc + c
    in_specs = [
        pl.BlockSpec((q, gw), lambda b, g, c: (row(b, g, c), g)),
        pl.BlockSpec((q, gn), lambda b, g, c: (row(b, g, c), bb + g)),
        pl.BlockSpec((q, gn), lambda b, g, c: (row(b, g, c), cb_ + g)),
        pl.BlockSpec((q, gw), lambda b, g, c: (row(b, g, c), g)),
        pl.BlockSpec((None, gs, hpg, q), lambda b, g, c: (b, g, 0, c)),
        pl.BlockSpec((None, gs, hpg, q), lambda b, g, c: (b, g, 0, c)),
        pl.BlockSpec((None, gs, q, hpg), lambda b, g, c: (b, g, c, 0)),
        pl.BlockSpec((None, gs * hpg, p, n), lambda b, g, c: (b, g, 0, 0)),
        pl.BlockSpec((1, gw), lambda b, g, c: (0, g)),
        pl.BlockSpec((1, gw), lambda b, g, c: (0, g)),
    ]
    out_specs = [
        pl.BlockSpec((q, gw), lambda b, g, c: (row(b, g, c), g)),
        pl.BlockSpec((None, gs * hpg, p, n), lambda b, g, c: (b, g, 0, 0)),
    ]
    kern = functools.partial(_ssd_cm_kernel, q=q, gs=gs, hpg=hpg, p=p, n=n, nc=nc)
    return pl.pallas_call(
        kern, grid=(bsz, groups // gs, nc), in_specs=in_specs, out_specs=out_specs,
        out_shape=[jax.ShapeDtypeStruct((t, d_inner), BF16), jax.ShapeDtypeStruct(state0.shape, F32)],
        scratch_shapes=[pltpu.VMEM((gs * w, n), F32)],
        compiler_params=_params("parallel", "parallel", "arbitrary"), name="ssd_cm",
    )(xbc, xbc, xbc, sz, dtt4, act4, ac4, state0, dsk, ng)


def _gmlp_kernel(u_ref, v_ref, lg_ref, lb_ref, ws_ref, bs_ref, yb_ref, vn_ref, *, groups, q):
    u = u_ref[...]
    vn = _ln_rows(v_ref[...], lg_ref[...], lb_ref[...])
    vn_ref[...] = vn
    vn16 = vn.astype(BF16)
    d = vn.shape[1] // groups
    ri = lax.broadcasted_iota(I32, (q, q), 0)
    ci = lax.broadcasted_iota(I32, (q, q), 1)
    for g in range(groups):
        wg = jnp.where(ri >= ci, ws_ref[g], 0.0).astype(BF16)
        s = _dot(wg, vn16[:, g * d:(g + 1) * d]) + bs_ref[g]
        yb_ref[:, g * d:(g + 1) * d] = (u[:, g * d:(g + 1) * d] * s).astype(BF16)


def _gmlp_call(uv, ln_g, ln_b, w_s, b_s, *, bsz, seq, q):
    groups = w_s.shape[0]
    t = bsz * seq
    gd = uv.shape[1] // 2
    assert seq % q == 0 and (gd // groups) % LANE == 0
    ws = w_s[:, :q, :q]
    bs = b_s[:, :q, None]
    row = pl.BlockSpec((q, gd), lambda i: (i, 0))
    kern = functools.partial(_gmlp_kernel, groups=groups, q=q)
    return pl.pallas_call(
        kern, grid=(t // q,),
        in_specs=[pl.BlockSpec((q, gd), lambda i: (i, 0)), pl.BlockSpec((q, gd), lambda i: (i, 1)),
                  pl.BlockSpec((1, gd), lambda i: (0, 0)), pl.BlockSpec((1, gd), lambda i: (0, 0)),
                  pl.BlockSpec((groups, q, q), lambda i: (0, 0, 0)),
                  pl.BlockSpec((groups, q, 1), lambda i: (0, 0, 0))],
        out_specs=[row, row],
        out_shape=[jax.ShapeDtypeStruct((t, gd), BF16), jax.ShapeDtypeStruct((t, gd), F32)],
        compiler_params=_params("parallel"), name="gmlp",
    )(uv, uv, ln_g.reshape(1, gd), ln_b.reshape(1, gd), ws, bs)


def _merge_kernel(ya_ref, yb_ref, wa_ref, wb_ref, ga_ref, gb_ref, o_ref):
    a = _dot(ya_ref[...], wa_ref[...])
    b = _dot(yb_ref[...], wb_ref[...])
    o_ref[...] = (ga_ref[...] * a + gb_ref[...] * b).astype(o_ref.dtype)


def _merge_call(ya, yb, gates, w_a, w_b, layer):
    t, ka = ya.shape
    kb = yb.shape[1]
    d = w_a.shape[2]
    tm, tn = _pick(t, 1024), _pick(d, 512)
    nd = d // tn
    return pl.pallas_call(
        _merge_kernel, grid=(t // tm, nd),
        in_specs=[pl.BlockSpec((tm, ka), lambda i, j: (i, 0)), pl.BlockSpec((tm, kb), lambda i, j: (i, 0)),
                  pl.BlockSpec((None, ka, tn), lambda i, j: (layer, 0, j)),
                  pl.BlockSpec((None, kb, tn), lambda i, j: (layer, 0, j)),
                  pl.BlockSpec((tm, tn), lambda i, j: (i, j)), pl.BlockSpec((tm, tn), lambda i, j: (i, nd + j))],
        out_specs=pl.BlockSpec((tm, tn), lambda i, j: (i, j)),
        out_shape=jax.ShapeDtypeStruct((t, d), BF16),
        compiler_params=_params("parallel", "parallel"), name="merge",
    )(ya, yb, w_a, w_b, gates, gates)


def _mm_res_ln_kernel(a_ref, w_ref, x_ref, g_ref, b_ref, o32_ref, o16_ref, acc_ref, *, alpha, nk):
    k = pl.program_id(1)

    @pl.when(k == 0)
    def _():
        acc_ref[...] = jnp.zeros_like(acc_ref)

    acc_ref[...] += _dot(a_ref[...], w_ref[...])

    @pl.when(k == nk - 1)
    def _():
        y = _ln_rows(alpha * x_ref[...] + acc_ref[...], g_ref[...], b_ref[...])
        o32_ref[...] = y
        o16_ref[...] = y.astype(BF16)


def _k_tile(k, cap):
    if k % LANE != 0:
        return k
    best = LANE
    for m in range(1, k // LANE + 1):
        tk = m * LANE
        if k % tk == 0 and tk <= cap:
            best = tk
    return best


def _mm_res_ln_call(a, w, layer, x, g, b, alpha, name):
    t, k = a.shape
    d = w.shape[2]
    tm = _pick(t, 512)
    tk = _k_tile(k, 2048)
    nk = k // tk
    row = pl.BlockSpec((tm, d), lambda i, kk: (i, 0))
    vec = pl.BlockSpec((1, d), lambda i, kk: (0, 0))
    kern = functools.partial(_mm_res_ln_kernel, alpha=alpha, nk=nk)
    return pl.pallas_call(
        kern, grid=(t // tm, nk),
        in_specs=[pl.BlockSpec((tm, tk), lambda i, kk: (i, kk)),
                  pl.BlockSpec((None, tk, d), lambda i, kk: (layer, kk, 0)), row, vec, vec],
        out_specs=[row, row],
        out_shape=[jax.ShapeDtypeStruct((t, d), F32), jax.ShapeDtypeStruct((t, d), BF16)],
        scratch_shapes=[pltpu.VMEM((tm, d), F32)],
        compiler_params=_params("parallel", "arbitrary"), name=name,
    )(a, w, x, g.reshape(1, d), b.reshape(1, d))


def _swiglu_up_kernel(meta_ref, x_ref, wg_ref, wu_ref, o_ref):
    live = pl.program_id(0) < meta_ref[0]

    @pl.when(live)
    def _():
        x = x_ref[...].astype(BF16)
        g = _dot(x, wg_ref[...])
        u = _dot(x, wu_ref[...])
        o_ref[...] = (_silu(g) * u).astype(o_ref.dtype)

    @pl.when(jnp.logical_not(live))
    def _():
        o_ref[...] = jnp.zeros_like(o_ref)


def _swiglu_up_call(meta, x, wg, wu, layer, tm, name):
    t, k = x.shape
    n = wg.shape[3]
    tn = _pick(n, 1024)
    nn = n // tn

    def live(i, m):
        return jnp.minimum(i, m[0] - 1)

    def col(i, j, m):
        return jnp.where(i < m[0], j, nn - 1)

    wspec = pl.BlockSpec((None, None, k, tn), lambda i, j, m: (layer, m[1 + live(i, m)], 0, col(i, j, m)))
    grid_spec = pltpu.PrefetchScalarGridSpec(
        num_scalar_prefetch=1, grid=(t // tm, nn),
        in_specs=[pl.BlockSpec((tm, k), lambda i, j, m: (live(i, m), 0)), wspec, wspec],
        out_specs=pl.BlockSpec((tm, tn), lambda i, j, m: (i, j)))
    return pl.pallas_call(
        _swiglu_up_kernel, grid_spec=grid_spec, out_shape=jax.ShapeDtypeStruct((t, n), BF16),
        compiler_params=_params("arbitrary", "arbitrary"), name=name,
    )(meta, x, wg, wu)


def _moe_down_kernel(meta_ref, h_ref, w_ref, o_ref, acc_ref, *, nk):
    k = pl.program_id(1)
    live = pl.program_id(0) < meta_ref[0]

    @pl.when(live)
    def _():
        @pl.when(k == 0)
        def _():
            acc_ref[...] = jnp.zeros_like(acc_ref)

        acc_ref[...] += _dot(h_ref[...], w_ref[...])

        @pl.when(k == nk - 1)
        def _():
            o_ref[...] = acc_ref[...]

    @pl.when(jnp.logical_not(live) & (k == nk - 1))
    def _():
        o_ref[...] = jnp.zeros_like(o_ref)


def _moe_down_call(meta, h, wd, layer, tm):
    t, k = h.shape
    d = wd.shape[3]
    tk = _k_tile(k, 2048)
    nk = k // tk

    def live(i, m):
        return jnp.minimum(i, m[0] - 1)

    def kk(i, k_, m):
        return jnp.where(i < m[0], k_, nk - 1)

    grid_spec = pltpu.PrefetchScalarGridSpec(
        num_scalar_prefetch=1, grid=(t // tm, nk),
        in_specs=[pl.BlockSpec((tm, tk), lambda i, k_, m: (live(i, m), kk(i, k_, m))),
                  pl.BlockSpec((None, None, tk, d), lambda i, k_, m: (layer, m[1 + live(i, m)], kk(i, k_, m), 0))],
        out_specs=pl.BlockSpec((tm, d), lambda i, k_, m: (i, 0)),
        scratch_shapes=[pltpu.VMEM((tm, d), F32)])
    return pl.pallas_call(
        functools.partial(_moe_down_kernel, nk=nk), grid_spec=grid_spec,
        out_shape=jax.ShapeDtypeStruct((t, d), F32),
        compiler_params=_params("arbitrary", "arbitrary"), name="moe_down",
    )(meta, h, wd)


def _router_kernel(x_ref, wr_ref, e_ref, w_ref, r_ref, cnt_ref, carry_ref, *, n_exp, tm):
    @pl.when(pl.program_id(0) == 0)
    def _():
        carry_ref[...] = jnp.zeros_like(carry_ref)

    logits = _dot_exact(x_ref[...], wr_ref[...])
    lane = lax.broadcasted_iota(I32, logits.shape, 1)
    lg = jnp.where(lane < n_exp, logits, -jnp.inf)
    m1 = jnp.max(lg, axis=-1, keepdims=True)
    i1 = jnp.min(jnp.where(lg == m1, lane, LANE), axis=-1, keepdims=True)
    lg2 = jnp.where(lane == i1, -jnp.inf, lg)
    m2 = jnp.max(lg2, axis=-1, keepdims=True)
    i2 = jnp.min(jnp.where(lg2 == m2, lane, LANE), axis=-1, keepdims=True)
    ex = jnp.exp(m2 - m1)
    w1 = 1.0 / (1.0 + ex)
    w2 = ex / (1.0 + ex)

    oh1 = (lane == i1).astype(F32)
    oh2 = (lane == i2).astype(F32)
    both = oh1 + oh2
    ri = lax.broadcasted_iota(I32, (tm, tm), 0)
    ci = lax.broadcasted_iota(I32, (tm, tm), 1)
    before = _dot((ri > ci).astype(BF16), both.astype(BF16)) + carry_ref[0:1, :]
    r1 = jnp.sum(before * oh1, axis=-1, keepdims=True)
    r2 = jnp.sum(before * oh2, axis=-1, keepdims=True)
    carry_ref[...] = carry_ref[...] + jnp.sum(both, axis=0, keepdims=True)

    e_ref[...] = jnp.where(lane == 0, i1, jnp.where(lane == 1, i2, 0))
    w_ref[...] = jnp.where(lane == 0, w1, jnp.where(lane == 1, w2, 0.0))
    r_ref[...] = jnp.where(lane == 0, r1, jnp.where(lane == 1, r2, 0.0)).astype(I32)
    cnt_ref[...] = carry_ref[...].astype(I32)


def _router_call(x, w_router):
    t, d = x.shape
    n_exp = w_router.shape[1]
    wr = jnp.zeros((d, LANE), F32).at[:, :n_exp].set(w_router)
    tm = _pick(t, 512)
    row = pl.BlockSpec((tm, LANE), lambda i: (i, 0))
    kern = functools.partial(_router_kernel, n_exp=n_exp, tm=tm)
    return pl.pallas_call(
        kern, grid=(t // tm,),
        in_specs=[pl.BlockSpec((tm, d), lambda i: (i, 0)), pl.BlockSpec((d, LANE), lambda i: (0, 0))],
        out_specs=[row, row, row, pl.BlockSpec((SUBLANE, LANE), lambda i: (0, 0))],
        out_shape=[jax.ShapeDtypeStruct((t, LANE), I32), jax.ShapeDtypeStruct((t, LANE), F32),
                   jax.ShapeDtypeStruct((t, LANE), I32), jax.ShapeDtypeStruct((SUBLANE, LANE), I32)],
        scratch_shapes=[pltpu.VMEM((SUBLANE, LANE), F32)],
        compiler_params=_params("arbitrary"), name="router",
    )(x, wr)


_DMA_UNROLL = 8


def _row_copy(src_hbm, dst_vmem, src_row, dst_row, sem):
    return pltpu.make_async_copy(src_hbm.at[pl.ds(src_row, 1)], dst_vmem.at[pl.ds(dst_row, 1)], sem)


def _gather_kernel(tok_ref, x_hbm, o_ref, sem, *, rows):
    def start(r, carry):
        _row_copy(x_hbm, o_ref, tok_ref[0, r], r, sem).start()
        return carry

    lax.fori_loop(0, rows, start, 0, unroll=_DMA_UNROLL)
    pltpu.make_async_copy(x_hbm.at[pl.ds(0, rows)], o_ref, sem).wait()


def _gather_call(x, tok, rows):
    cap = tok.shape[0]
    d = x.shape[1]
    tok3 = tok.reshape(cap // rows, 1, rows)
    return pl.pallas_call(
        functools.partial(_gather_kernel, rows=rows), grid=(cap // rows,),
        in_specs=[pl.BlockSpec((None, 1, rows), lambda i: (i, 0, 0), memory_space=pltpu.SMEM),
                  pl.BlockSpec(memory_space=pl.ANY)],
        out_specs=pl.BlockSpec((rows, d), lambda i: (i, 0)),
        out_shape=jax.ShapeDtypeStruct((cap, d), x.dtype),
        scratch_shapes=[pltpu.SemaphoreType.DMA(())],
        compiler_params=_params("arbitrary"), name="moe_gather",
    )(tok3, x)


def _combine_ln_kernel(d0_ref, d1_ref, o_hbm, w_ref, x_ref, g_ref, b_ref, o32_ref, o16_ref, buf0, buf1, sem,
                       *, alpha, rows):
    def start(r, carry):
        _row_copy(o_hbm, buf0, d0_ref[0, r], r, sem).start()
        _row_copy(o_hbm, buf1, d1_ref[0, r], r, sem).start()
        return carry

    lax.fori_loop(0, rows, start, 0, unroll=_DMA_UNROLL)
    pltpu.make_async_copy(o_hbm.at[pl.ds(0, rows)], buf0, sem).wait()
    pltpu.make_async_copy(o_hbm.at[pl.ds(0, rows)], buf1, sem).wait()
    wts = w_ref[...]
    ffn = wts[:, 0:1] * buf0[...] + wts[:, 1:2] * buf1[...]
    y = _ln_rows(alpha * x_ref[...] + ffn, g_ref[...], b_ref[...])
    o32_ref[...] = y
    o16_ref[...] = y.astype(BF16)


def _combine_ln_call(dest, o_sorted, wts, x, g, b, alpha):
    t, d = x.shape
    rows = _pick(t, 256)
    d0 = dest[:, 0].reshape(t // rows, 1, rows)
    d1 = dest[:, 1].reshape(t // rows, 1, rows)
    idx = pl.BlockSpec((None, 1, rows), lambda i: (i, 0, 0), memory_space=pltpu.SMEM)
    row = pl.BlockSpec((rows, d), lambda i: (i, 0))
    vec = pl.BlockSpec((1, d), lambda i: (0, 0))
    kern = functools.partial(_combine_ln_kernel, alpha=alpha, rows=rows)
    return pl.pallas_call(
        kern, grid=(t // rows,),
        in_specs=[idx, idx, pl.BlockSpec(memory_space=pl.ANY), pl.BlockSpec((rows, LANE), lambda i: (i, 0)),
                  row, vec, vec],
        out_specs=[row, row],
        out_shape=[jax.ShapeDtypeStruct((t, d), F32), jax.ShapeDtypeStruct((t, d), BF16)],
        scratch_shapes=[pltpu.VMEM((rows, d), F32), pltpu.VMEM((rows, d), F32), pltpu.SemaphoreType.DMA(())],
        compiler_params=_params("arbitrary"), name="moe_combine_ln",
    )(d0, d1, o_sorted, wts, x, g.reshape(1, d), b.reshape(1, d))


def _moe_layer(xf, w_router, wg, wu, wd, layer, ln_g, ln_b, alpha):
    t, d = xf.shape
    n_exp = wg.shape[1]
    e_out, w_out, r_out, cnt = _router_call(xf, w_router)
    e = e_out[:, :MOE_TOP_K]
    counts = cnt[0, :n_exp]

    tm = 512 if t * MOE_TOP_K >= 8 * 512 else 128
    n_tiles = -(-(t * MOE_TOP_K) // tm) + n_exp
    cap = n_tiles * tm
    padded = (counts + tm - 1) // tm * tm
    pad_end = jnp.cumsum(padded)
    pad_start = pad_end - padded
    dest = (pad_start[e] + r_out[:, :MOE_TOP_K]).astype(I32)
    tok = jnp.repeat(jnp.arange(t, dtype=I32), MOE_TOP_K)
    slot_tok = jnp.zeros((cap,), I32).at[dest.reshape(-1)].set(tok)
    tile_expert = jnp.minimum(jnp.searchsorted(pad_end, jnp.arange(n_tiles, dtype=I32) * tm, side="right"),
                              n_exp - 1)
    meta = jnp.concatenate([(pad_end[-1:] // tm), tile_expert]).astype(I32)

    xs = _gather_call(xf, slot_tok, _pick(cap, 256))
    h = _swiglu_up_call(meta, xs, wg, wu, layer, tm, "moe_up")
    o_sorted = _moe_down_call(meta, h, wd, layer, tm)
    return _combine_ln_call(dest, o_sorted, w_out, xf, ln_g, ln_b, alpha)


def _trunk(x, conv_bufs, ssm_states, pr):
    bsz, seq, d = x.shape
    t = bsz * seq
    depth = pr["w_main"].shape[0]
    alpha = float((2 * depth) ** 0.25)
    d_inner = pr["w_br_a"].shape[1]
    cdim = pr["conv_w"].shape[2]
    gd = pr["w_br_b"].shape[1]
    kc = pr["conv_w"].shape[1]
    off_xbc = d_inner
    off_u = d_inner + cdim
    off_gate = off_u + 2 * gd
    q_ssd = _pick(seq, 128)
    n_groups = (cdim - d_inner) // (2 * ssm_states.shape[-1])
    gs_ssd = 2 if n_groups % 2 == 0 else 1
    q_gmlp = min(pr["w_s"].shape[2], seq)

    xf, xb = _ln_call(x.reshape(t, d), pr["ln_in_g"], pr["ln_in_b"])
    convs, ssms, vs = [], [], []
    w_main = pr["w_main"]
    lane_aligned = q_ssd % LANE == 0
    for i in range(depth):
        sz = _proj_call(xb, w_main, i, 0, d_inner, "silu", BF16, "in_proj_z")
        uv = _proj_call(xb, w_main, i, off_u, 2 * gd, "gelu", F32, "in_proj_uv")
        gates = _proj_call(xb, w_main, i, off_gate, 2 * d, "none", BF16, "in_proj_gate", bias=pr["b_gate"][i])
        dt, acum = _dt_call(xf, pr["w_dt"][i], pr["dt_bias"][i], pr["a_log_pad"][i], q_ssd)
        if lane_aligned:
            xbc, tails = _proj_conv_call(xb, w_main, i, off_xbc, pr["conv_w"][i], pr["conv_b"][i], conv_bufs[i], seq)
            ya, ssm_i = _ssd_cm_call(xbc, sz, dt, acum, ssm_states[i], pr["d_skip"][i], pr["ssd_norm_g"][i],
                                     bsz=bsz, seq=seq, q=q_ssd, gs=gs_ssd)
            new_rows = tails.reshape(bsz, -1, _CONV_PAD, cdim)[:, -1, _CONV_PAD - (kc - 1):]
        else:
            xbc = _proj_call(xb, w_main, i, off_xbc, cdim, "none", F32, "in_proj_xbc")
            ya, ssm_i = _ssd_call(xbc, sz, dt, conv_bufs[i], ssm_states[i], pr["conv_w"][i], pr["conv_b"][i],
                                  pr["a_log"][i], pr["d_skip"][i], pr["ssd_norm_g"][i], bsz=bsz, seq=seq, q=q_ssd)
            new_rows = xbc.reshape(bsz, seq, cdim)[:, max(seq - (kc - 1), 0):]
        conv_i = jnp.concatenate([conv_bufs[i], new_rows], axis=1)[:, -(kc - 1):]
        yb, vn = _gmlp_call(uv, pr["gmlp_ln_g"][i], pr["gmlp_ln_b"][i], pr["w_s"][i], pr["b_s"][i],
                            bsz=bsz, seq=seq, q=q_gmlp)
        merged = _merge_call(ya, yb, gates, pr["w_br_a"], pr["w_br_b"], i)
        xf, xb = _mm_res_ln_call(merged, pr["w_o"], i, xf, pr["ln1_g"][i], pr["ln1_b"][i], alpha, "out_proj_ln")
        j = i // 2
        if i % 2 == 0:
            tm = _pick(t, 1024)
            meta = jnp.concatenate([jnp.full((1,), t // tm, I32), jnp.zeros((t // tm,), I32)])
            h = _swiglu_up_call(meta, xb, pr["w_ff_gate"], pr["w_ff_up"], j, tm, "ffn_up")
            xf, xb = _mm_res_ln_call(h, pr["w_ff_down"], j, xf, pr["ln2_g"][i], pr["ln2_b"][i], alpha,
                                     "ffn_down_ln")
        else:
            xf, xb = _moe_layer(xf, pr["w_router"][j], pr["w_moe_gate"], pr["w_moe_up"], pr["w_moe_down"], j,
                                pr["ln2_g"][i], pr["ln2_b"][i], alpha)
        convs.append(conv_i)
        ssms.append(ssm_i)
        vs.append(vn.reshape(bsz, seq, gd))
    return xf.reshape(bsz, seq, d), jnp.stack(convs), jnp.stack(ssms), jnp.stack(vs)


def kernel(x_prompt, x_sample, cache_conv, state_ssm, ln_in_g, ln_in_b, w_in, conv_w, conv_b, dt_bias, a_log,
           d_skip, ssd_norm_g, gmlp_ln_g, gmlp_ln_b, w_s, b_s, b_gate, w_br_a, w_br_b, w_o, ln1_g, ln1_b,
           w_ff_gate, w_ff_up, w_ff_down, w_router, w_moe_gate, w_moe_up, w_moe_down, ln2_g, ln2_b):
    depth, d_model, _ = w_in.shape
    d_inner = w_br_a.shape[1]
    cdim = conv_w.shape[2]
    heads = a_log.shape[1]
    o_dt = d_inner + cdim
    w_in16 = w_in.astype(BF16)
    w_main = jnp.concatenate([w_in16[:, :, :o_dt], w_in16[:, :, o_dt + heads:]], axis=2)
    w_dt = jnp.zeros((depth, d_model, LANE), F32).at[:, :, :heads].set(w_in[:, :, o_dt:o_dt + heads])
    dt_b = jnp.zeros((depth, 1, LANE), F32).at[:, 0, :heads].set(dt_bias)
    al_pad = jnp.zeros((depth, 1, LANE), F32).at[:, 0, :heads].set(a_log)
    pr = dict(a_log_pad=al_pad,
        ln_in_g=ln_in_g, ln_in_b=ln_in_b, w_main=w_main, w_dt=w_dt, dt_bias=dt_b, conv_w=conv_w, conv_b=conv_b,
        a_log=a_log, d_skip=d_skip, ssd_norm_g=ssd_norm_g, gmlp_ln_g=gmlp_ln_g, gmlp_ln_b=gmlp_ln_b, w_s=w_s,
        b_s=b_s, b_gate=b_gate, w_br_a=w_br_a.astype(BF16), w_br_b=w_br_b.astype(BF16), w_o=w_o.astype(BF16),
        ln1_g=ln1_g, ln1_b=ln1_b, w_ff_gate=w_ff_gate.astype(BF16)[:, None], w_ff_up=w_ff_up.astype(BF16)[:, None],
        w_ff_down=w_ff_down.astype(BF16), w_router=w_router, w_moe_gate=w_moe_gate.astype(BF16),
        w_moe_up=w_moe_up.astype(BF16), w_moe_down=w_moe_down.astype(BF16), ln2_g=ln2_g, ln2_b=ln2_b)
    bp = x_prompt.shape[0]
    zero_conv = jnp.zeros((depth, bp) + cache_conv.shape[2:], x_prompt.dtype)
    zero_ssm = jnp.zeros((depth, bp) + state_ssm.shape[2:], state_ssm.dtype)
    y_prompt, prompt_conv, prompt_ssm, _ = _trunk(x_prompt, zero_conv, zero_ssm, pr)
    y_sample, sample_conv, sample_ssm, sample_v = _trunk(x_sample, cache_conv, state_ssm, pr)
    return (y_prompt, y_sample, prompt_conv, prompt_ssm, sample_conv, sample_ssm, sample_v)
```

```python
import functools

import jax
import jax.numpy as jnp
from jax import lax
from jax.experimental import pallas as pl
from jax.experimental.pallas import tpu as pltpu

F32 = jnp.float32
BF16 = jnp.bfloat16
I32 = jnp.int32
HIGHEST = lax.Precision.HIGHEST

LN_EPS = 1e-5
LANE = 128
SUBLANE = 8
V7X_VMEM_BYTES = 64 * 1024 * 1024
VMEM_LIMIT_BYTES = V7X_VMEM_BYTES - 8 * 1024 * 1024
MOE_TOP_K = 2
_NEG_BIG = -1e30
_TILE_CANDIDATES = (1024, 512, 256, 128, 64, 32, 16, 8)


def _pick(n, cap):
    for c in _TILE_CANDIDATES:
        if c <= cap and n % c == 0:
            return c
    raise ValueError(f"no tile for {n}")


def _params(*sem):
    return pltpu.CompilerParams(dimension_semantics=sem, vmem_limit_bytes=VMEM_LIMIT_BYTES)


def _ln_rows(x, g, b):
    mu = jnp.mean(x, axis=-1, keepdims=True)
    xc = x - mu
    var = jnp.mean(xc * xc, axis=-1, keepdims=True)
    return xc * lax.rsqrt(var + LN_EPS) * g + b


def _sigmoid(x):
    return 0.5 * jnp.tanh(0.5 * x) + 0.5


def _silu(x):
    return x * _sigmoid(x)


def _gelu(x):
    return 0.5 * x * (1.0 + lax.erf(x * (2.0 ** -0.5)))


def _dot(a, b):
    return jnp.dot(a, b, preferred_element_type=F32)


def _dot_exact(a, b):
    return jnp.dot(a, b, precision=HIGHEST, preferred_element_type=F32)


def _dot_split(a, b):
    a_hi = a.astype(BF16)
    b_hi = b.astype(BF16)
    a_lo = (a - a_hi.astype(F32)).astype(BF16)
    b_lo = (b - b_hi.astype(F32)).astype(BF16)
    return _dot(a_hi, b_hi) + (_dot(a_hi, b_lo) + _dot(a_lo, b_hi))


def _ln_kernel(x_ref, g_ref, b_ref, o32_ref, o16_ref):
    y = _ln_rows(x_ref[...], g_ref[...], b_ref[...])
    o32_ref[...] = y
    o16_ref[...] = y.astype(BF16)


def _ln_call(x, g, b):
    t, d = x.shape
    tm = _pick(t, 512)
    row = pl.BlockSpec((tm, d), lambda i: (i, 0))
    vec = pl.BlockSpec((1, d), lambda i: (0, 0))
    return pl.pallas_call(
        _ln_kernel, grid=(t // tm,), in_specs=[row, vec, vec], out_specs=[row, row],
        out_shape=[jax.ShapeDtypeStruct((t, d), F32), jax.ShapeDtypeStruct((t, d), BF16)],
        compiler_params=_params("parallel"), name="ln_in",
    )(x, g.reshape(1, d), b.reshape(1, d))


_CONV_PAD = SUBLANE

_ACTIVATIONS = {"none": lambda r: r, "silu": _silu, "gelu": _gelu}


def _proj_kernel(x_ref, w_ref, o_ref, *, act):
    o_ref[...] = _ACTIVATIONS[act](_dot(x_ref[...], w_ref[...])).astype(o_ref.dtype)


def _proj_gate_kernel(x_ref, w_ref, b_ref, o_ref):
    o_ref[...] = _sigmoid(_dot(x_ref[...], w_ref[...]) + b_ref[...]).astype(o_ref.dtype)


def _proj_conv_kernel(x_ref, w_ref, cw_ref, cc_ref, cbuf_ref, o_ref, tail_ref, halo_ref, *, kc, tiles_per_seq):
    i, j = pl.program_id(0), pl.program_id(1)
    tm = x_ref.shape[0]
    r = _dot(x_ref[...], w_ref[...])
    before = jnp.where(lax.rem(i, tiles_per_seq) == 0, cbuf_ref[...], halo_ref[j])
    nb = tm // _CONV_PAD
    blocks = r.reshape(nb, _CONV_PAD, r.shape[1])
    prev_blocks = jnp.concatenate([before[None], blocks[:nb - 1]], axis=0)
    sub = lax.broadcasted_iota(I32, blocks.shape, 1)
    acc = cc_ref[...] + r * cw_ref[kc - 1:kc, :]
    for s in range(1, kc):
        shifted = jnp.where(sub < s, pltpu.roll(prev_blocks, s, 1), pltpu.roll(blocks, s, 1))
        acc = acc + shifted.reshape(r.shape) * cw_ref[kc - 1 - s:kc - s, :]
    o_ref[...] = _silu(acc)
    last_rows = r[tm - _CONV_PAD:tm, :]
    tail_ref[...] = last_rows
    halo_ref[j] = last_rows


def _region_tiles(t, n, col0):
    tm = _pick(t, 1024)
    tn = next(c for c in _TILE_CANDIDATES if n % c == 0 and col0 % c == 0)
    return tm, tn


def _proj_call(x, w, layer, col0, n, act, out_dtype, name, bias=None):
    t, k = x.shape
    tm, tn = _region_tiles(t, n, col0)
    c0 = col0 // tn
    in_specs = [pl.BlockSpec((tm, k), lambda i, j: (i, 0)), pl.BlockSpec((None, k, tn), lambda i, j: (layer, 0, c0 + j))]
    args = [x, w]
    if bias is None:
        kern = functools.partial(_proj_kernel, act=act)
    else:
        kern = _proj_gate_kernel
        in_specs.append(pl.BlockSpec((1, tn), lambda i, j: (0, j)))
        args.append(bias.reshape(1, n))
    return pl.pallas_call(
        kern, grid=(t // tm, n // tn), in_specs=in_specs,
        out_specs=pl.BlockSpec((tm, tn), lambda i, j: (i, j)),
        out_shape=jax.ShapeDtypeStruct((t, n), out_dtype),
        compiler_params=_params("parallel", "parallel"), name=name,
    )(*args)


def _proj_conv_call(x, w, layer, col0, conv_w, conv_b, conv_buf, seq):
    t, k = x.shape
    kc, n = conv_w.shape
    tm, tn = _pick(seq, 1024), _region_tiles(t, n, col0)[1]
    assert t % tm == 0 and kc - 1 <= _CONV_PAD <= tm
    tiles_per_seq = seq // tm
    c0 = col0 // tn
    cbuf = jnp.pad(conv_buf, ((0, 0), (_CONV_PAD - (kc - 1), 0), (0, 0)))
    kern = functools.partial(_proj_conv_kernel, kc=kc, tiles_per_seq=tiles_per_seq)
    return pl.pallas_call(
        kern, grid=(t // tm, n // tn),
        in_specs=[pl.BlockSpec((tm, k), lambda i, j: (i, 0)),
                  pl.BlockSpec((None, k, tn), lambda i, j: (layer, 0, c0 + j)),
                  pl.BlockSpec((kc, tn), lambda i, j: (0, j)),
                  pl.BlockSpec((1, tn), lambda i, j: (0, j)),
                  pl.BlockSpec((None, _CONV_PAD, tn), lambda i, j: (i // tiles_per_seq, 0, j))],
        out_specs=[pl.BlockSpec((tm, tn), lambda i, j: (i, j)),
                   pl.BlockSpec((None, _CONV_PAD, tn), lambda i, j: (i, 0, j))],
        out_shape=[jax.ShapeDtypeStruct((t, n), F32), jax.ShapeDtypeStruct((t // tm, _CONV_PAD, n), F32)],
        scratch_shapes=[pltpu.VMEM((n // tn, _CONV_PAD, tn), F32)],
        compiler_params=_params("arbitrary", "arbitrary"), name="in_proj_conv",
    )(x, w, conv_w, conv_b.reshape(1, n), cbuf)


def _dt_kernel(x_ref, w_ref, b_ref, al_ref, dt_ref, ac_ref, *, q):
    raw = _dot_split(x_ref[...], w_ref[...]) + b_ref[...]
    dt = jnp.maximum(raw, 0.0) + jnp.log1p(jnp.exp(-jnp.abs(raw)))
    dt_ref[...] = dt
    a = dt * (-jnp.exp(al_ref[...]))
    ri = lax.broadcasted_iota(I32, (q, q), 0)
    ci = lax.broadcasted_iota(I32, (q, q), 1)
    tril = (ri >= ci).astype(F32)
    for c in range(a.shape[0] // q):
        ac_ref[c * q:(c + 1) * q, :] = _dot_exact(tril, a[c * q:(c + 1) * q, :])


def _dt_call(x, w, bias, a_log, q):
    t, k = x.shape
    n = w.shape[1]
    tm = max(_pick(t, 512), q)
    assert tm % q == 0 and t % tm == 0
    row = pl.BlockSpec((tm, n), lambda i: (i, 0))
    vec = pl.BlockSpec((1, n), lambda i: (0, 0))
    return pl.pallas_call(
        functools.partial(_dt_kernel, q=q), grid=(t // tm,),
        in_specs=[pl.BlockSpec((tm, k), lambda i: (i, 0)), pl.BlockSpec((k, n), lambda i: (0, 0)), vec, vec],
        out_specs=[row, row],
        out_shape=[jax.ShapeDtypeStruct((t, n), F32), jax.ShapeDtypeStruct((t, n), F32)],
        compiler_params=_params("parallel"), name="dt_proj",
    )(x, w, bias, a_log)


def _ssd_kernel(xs_ref, bm_ref, cm_ref, sz_ref, dt_ref, dtt_ref, cbx_ref, cbb_ref, cbc_ref, s0_ref,
                cwx_ref, cwb_ref, cwc_ref, ccx_ref, ccb_ref, ccc_ref, alr_ref, alc_ref, dsk_ref, ng_ref,
                y_ref, sout_ref, xpad_ref, st_ref, *, q, hpg, p, n, kc, nc):
    c = pl.program_id(2)
    w = hpg * p
    tail = kc - 1
    t0 = _CONV_PAD - tail

    @pl.when(c == 0)
    def _():
        xpad_ref[t0:_CONV_PAD, 0:w] = cbx_ref[...]
        xpad_ref[t0:_CONV_PAD, w:w + n] = cbb_ref[...]
        xpad_ref[t0:_CONV_PAD, w + n:w + 2 * n] = cbc_ref[...]
        st_ref[...] = s0_ref[...].reshape(w, n).T

    xpad_ref[_CONV_PAD:_CONV_PAD + q, 0:w] = xs_ref[...]
    xpad_ref[_CONV_PAD:_CONV_PAD + q, w:w + n] = bm_ref[...]
    xpad_ref[_CONV_PAD:_CONV_PAD + q, w + n:w + 2 * n] = cm_ref[...]

    def conv(lo, hi, cw_ref, cc_ref):
        acc = cc_ref[...]
        for k in range(kc):
            acc = acc + xpad_ref[t0 + k:t0 + k + q, lo:hi] * cw_ref[k:k + 1, :]
        return _silu(acc)

    xs = conv(0, w, cwx_ref, ccx_ref)
    bm = conv(w, w + n, cwb_ref, ccb_ref)
    cm = conv(w + n, w + 2 * n, cwc_ref, ccc_ref)
    xpad_ref[t0:_CONV_PAD, :] = xpad_ref[t0 + q:_CONV_PAD + q, :]

    a = dt_ref[...] * (-jnp.exp(alr_ref[...]))
    a_t = dtt_ref[...] * (-jnp.exp(alc_ref[...]))
    ri = lax.broadcasted_iota(I32, (q, q), 0)
    ci = lax.broadcasted_iota(I32, (q, q), 1)
    tril = ri >= ci
    acol = _dot_exact(tril.astype(F32), a)
    arow = _dot_exact(a_t, (ri <= ci).astype(F32))

    hrow = lax.broadcasted_iota(I32, (hpg, w), 0)
    hlane = lax.broadcasted_iota(I32, (hpg, w), 1)
    expand = ((hlane >= hrow * p) & (hlane < (hrow + 1) * p)).astype(F32)
    xdt = xs * _dot_exact(dt_ref[...], expand)
    eacol = _dot_exact(jnp.exp(acol), expand)
    to_end = _dot_exact(jnp.exp(acol[q - 1:q, :] - acol), expand)

    bm16 = bm.astype(BF16)
    cm16 = cm.astype(BF16)
    cb = lax.dot_general(cm16, bm16, (((1,), (1,)), ((), ())), preferred_element_type=F32)
    st = st_ref[...]
    y = _dot(cm16, st.astype(BF16)) * eacol
    lane = lax.broadcasted_iota(I32, (q, w), 1)
    for h in range(hpg):
        seg = acol[:, h:h + 1] - arow[h:h + 1, :]
        m = (cb * jnp.exp(jnp.where(tril, seg, _NEG_BIG))).astype(BF16)
        head = (lane >= h * p) & (lane < (h + 1) * p)
        y = y + _dot(m, jnp.where(head, xdt, 0.0).astype(BF16))
    y = y + dsk_ref[...] * xs
    hz = y * sz_ref[...]
    y_ref[...] = (hz * lax.rsqrt(jnp.mean(hz * hz, axis=-1, keepdims=True) + LN_EPS) * ng_ref[...]).astype(BF16)

    upd = lax.dot_general(bm16, (xdt * to_end).astype(BF16), (((0,), (0,)), ((), ())),
                          preferred_element_type=F32)
    st_ref[...] = st * eacol[q - 1:q, :] + upd

    @pl.when(c == nc - 1)
    def _():
        sout_ref[...] = st_ref[...].T.reshape(hpg, p, n)


def _ssd_call(xbc, sz, dt, conv_buf, state0, conv_w, conv_b, a_log, d_skip, norm_g, *, bsz, seq, q):
    heads, p, n = state0.shape[1:]
    kc, cdim = conv_w.shape
    d_inner = sz.shape[1]
    groups = (cdim - d_inner) // (2 * n)
    hpg = heads // groups
    w = hpg * p
    nc = seq // q
    t = bsz * seq
    assert w * groups == d_inner and seq % q == 0 and q >= kc - 1 and d_inner % n == 0

    dt4 = dt[:, :heads].reshape(bsz, seq, groups, hpg).transpose(0, 2, 1, 3)
    dtt4 = dt4.transpose(0, 1, 3, 2)
    alr = a_log.reshape(groups, 1, hpg)
    alc = a_log.reshape(groups, hpg, 1)
    dsk = jnp.repeat(d_skip, p).reshape(1, d_inner)
    ng = norm_g.reshape(1, d_inner)
    ccb = conv_b.reshape(1, cdim)

    bb, cb_ = d_inner // n, d_inner // n + groups
    row = lambda b, g, c: b * nc + c
    in_specs = [
        pl.BlockSpec((q, w), lambda b, g, c: (row(b, g, c), g)),
        pl.BlockSpec((q, n), lambda b, g, c: (row(b, g, c), bb + g)),
        pl.BlockSpec((q, n), lambda b, g, c: (row(b, g, c), cb_ + g)),
        pl.BlockSpec((q, w), lambda b, g, c: (row(b, g, c), g)),
        pl.BlockSpec((None, None, q, hpg), lambda b, g, c: (b, g, c, 0)),
        pl.BlockSpec((None, None, hpg, q), lambda b, g, c: (b, g, 0, c)),
        pl.BlockSpec((None, kc - 1, w), lambda b, g, c: (b, 0, g)),
        pl.BlockSpec((None, kc - 1, n), lambda b, g, c: (b, 0, d_inner // n + g)),
        pl.BlockSpec((None, kc - 1, n), lambda b, g, c: (b, 0, d_inner // n + groups + g)),
        pl.BlockSpec((None, hpg, p, n), lambda b, g, c: (b, g, 0, 0)),
        pl.BlockSpec((kc, w), lambda b, g, c: (0, g)),
        pl.BlockSpec((kc, n), lambda b, g, c: (0, d_inner // n + g)),
        pl.BlockSpec((kc, n), lambda b, g, c: (0, d_inner // n + groups + g)),
        pl.BlockSpec((1, w), lambda b, g, c: (0, g)),
        pl.BlockSpec((1, n), lambda b, g, c: (0, d_inner // n + g)),
        pl.BlockSpec((1, n), lambda b, g, c: (0, d_inner // n + groups + g)),
        pl.BlockSpec((None, 1, hpg), lambda b, g, c: (g, 0, 0)),
        pl.BlockSpec((None, hpg, 1), lambda b, g, c: (g, 0, 0)),
        pl.BlockSpec((1, w), lambda b, g, c: (0, g)),
        pl.BlockSpec((1, w), lambda b, g, c: (0, g)),
    ]
    out_specs = [
        pl.BlockSpec((q, w), lambda b, g, c: (row(b, g, c), g)),
        pl.BlockSpec((None, hpg, p, n), lambda b, g, c: (b, g, 0, 0)),
    ]
    kern = functools.partial(_ssd_kernel, q=q, hpg=hpg, p=p, n=n, kc=kc, nc=nc)
    return pl.pallas_call(
        kern, grid=(bsz, groups, nc), in_specs=in_specs, out_specs=out_specs,
        out_shape=[jax.ShapeDtypeStruct((t, d_inner), BF16), jax.ShapeDtypeStruct(state0.shape, F32)],
        scratch_shapes=[pltpu.VMEM((_CONV_PAD + q, w + 2 * n), F32), pltpu.VMEM((n, w), F32)],
        compiler_params=_params("parallel", "parallel", "arbitrary"), name="ssd",
    )(xbc, xbc, xbc, sz, dt4, dtt4, conv_buf, conv_buf, conv_buf, state0,
      conv_w, conv_w, conv_w, ccb, ccb, ccb, alr, alc, dsk, ng)


def _ssd_cm_kernel(xs_ref, bm_ref, cm_ref, sz_ref, dtt_ref, act_ref, ac_ref, s0_ref, dsk_ref, ng_ref,
                   y_ref, sout_ref, st_ref, *, q, gs, hpg, p, n, nc):
    c = pl.program_id(2)
    w = hpg * p

    @pl.when(c == 0)
    def _():
        st_ref[...] = s0_ref[...].reshape(gs * w, n)

    def rows(v):
        return jnp.concatenate([jnp.broadcast_to(v[h:h + 1, :], (p, v.shape[1])) for h in range(hpg)], axis=0)

    si = lax.broadcasted_iota(I32, (q, q), 0)
    li = lax.broadcasted_iota(I32, (q, q), 1)
    keep = si <= li
    nt = (((1,), (1,)), ((), ()))
    for g in range(gs):
        xs = xs_ref[:, g * w:(g + 1) * w]
        bm16 = bm_ref[:, g * n:(g + 1) * n].astype(BF16)
        cm16 = cm_ref[:, g * n:(g + 1) * n].astype(BF16)
        arow = act_ref[g]
        acol = ac_ref[g]
        ear = jnp.exp(arow)
        te = jnp.exp(arow[:, q - 1:q] - arow)
        xdt_t = xs.T * rows(dtt_ref[g])
        xdt16 = xdt_t.astype(BF16)
        cb_t = lax.dot_general(bm16, cm16, nt, preferred_element_type=F32)
        ys = []
        for h in range(hpg):
            seg = arow[h:h + 1, :] - acol[:, h:h + 1]
            m_t = (cb_t * jnp.exp(jnp.where(keep, seg, _NEG_BIG))).astype(BF16)
            ys.append(_dot(xdt16[h * p:(h + 1) * p, :], m_t))
        st = st_ref[g * w:(g + 1) * w, :]
        y_t = jnp.concatenate(ys, axis=0)
        y_t = y_t + lax.dot_general(st.astype(BF16), cm16, nt, preferred_element_type=F32) * rows(ear)
        y = y_t.T + dsk_ref[:, g * w:(g + 1) * w] * xs
        hz = y * sz_ref[:, g * w:(g + 1) * w]
        y_ref[:, g * w:(g + 1) * w] = (hz * lax.rsqrt(jnp.mean(hz * hz, axis=-1, keepdims=True) + LN_EPS)
                                       * ng_ref[:, g * w:(g + 1) * w]).astype(BF16)
        upd = _dot((xdt_t * rows(te)).astype(BF16), bm16)
        st_ref[g * w:(g + 1) * w, :] = st * rows(ear[:, q - 1:q]) + upd

    @pl.when(c == nc - 1)
    def _():
        sout_ref[...] = st_ref[...].reshape(gs * hpg, p, n)


def _ssd_cm_call(xbc, sz, dt, acum, state0, d_skip, norm_g, *, bsz, seq, q, gs):
    heads, p, n = state0.shape[1:]
    cdim = xbc.shape[1]
    d_inner = sz.shape[1]
    groups = (cdim - d_inner) // (2 * n)
    hpg = heads // groups
    w = hpg * p
    nc = seq // q
    t = bsz * seq
    gw, gn = gs * w, gs * n
    assert w * groups == d_inner and seq % q == 0 and q % LANE == 0 and groups % gs == 0
    assert d_inner % gn == 0 and (groups * n) % gn == 0

    def heads_major(v):
        return v[:, :heads].reshape(bsz, seq, groups, hpg).transpose(0, 2, 3, 1)

    dtt4 = heads_major(dt)
    act4 = heads_major(acum)
    ac4 = act4.transpose(0, 1, 3, 2)
    dsk = jnp.repeat(d_skip, p).reshape(1, d_inner)
    ng = norm_g.reshape(1, d_inner)

    bb, cb_ = d_inner // gn, (d_inner + groups * n) // gn
    row = lambda b, g, c: b * nc + c
    in_specs = [
        pl.BlockSpec((q, gw), lambda b, g, c: (row(b, g, c), g)),
        pl.BlockSpec((q, gn), lambda b, g, c: (row(b, g, c), bb + g)),
        pl.BlockSpec((q, gn), lambda b, g, c: (row(b, g, c), cb_ + g)),
        pl.BlockSpec((q, gw), lambda b, g, c: (row(b, g, c), g)),
        pl.BlockSpec((None, gs, hpg, q), lambda b, g, c: (b, g, 0, c)),
        pl.BlockSpec((None, gs, hpg, q), lambda b, g, c: (b, g, 0, c)),
        pl.BlockSpec((None, gs, q, hpg), lambda b, g, c: (b, g, c, 0)),
        pl.BlockSpec((None, gs * hpg, p, n), lambda b, g, c: (b, g, 0, 0)),
        pl.BlockSpec((1, gw), lambda b, g, c: (0, g)),
        pl.BlockSpec((1, gw), lambda b, g, c: (0, g)),
    ]
    out_specs = [
        pl.BlockSpec((q, gw), lambda b, g, c: (row(b, g, c), g)),
        pl.BlockSpec((None, gs * hpg, p, n), lambda b, g, c: (b, g, 0, 0)),
    ]
    kern = functools.partial(_ssd_cm_kernel, q=q, gs=gs, hpg=hpg, p=p, n=n, nc=nc)
    return pl.pallas_call(
        kern, grid=(bsz, groups // gs, nc), in_specs=in_specs, out_specs=out_specs,
        out_shape=[jax.ShapeDtypeStruct((t, d_inner), BF16), jax.ShapeDtypeStruct(state0.shape, F32)],
        scratch_shapes=[pltpu.VMEM((gs * w, n), F32)],
        compiler_params=_params("parallel", "parallel", "arbitrary"), name="ssd_cm",
    )(xbc, xbc, xbc, sz, dtt4, act4, ac4, state0, dsk, ng)


def _gmlp_kernel(u_ref, v_ref, lg_ref, lb_ref, ws_ref, bs_ref, yb_ref, *vn_refs, groups, q):
    u = u_ref[...]
    vn = _ln_rows(v_ref[...], lg_ref[...], lb_ref[...])
    for vn_ref in vn_refs:
        vn_ref[...] = vn
    vn16 = vn.astype(BF16)
    d = vn.shape[1] // groups
    ri = lax.broadcasted_iota(I32, (q, q), 0)
    ci = lax.broadcasted_iota(I32, (q, q), 1)
    for g in range(groups):
        wg = jnp.where(ri >= ci, ws_ref[g], 0.0).astype(BF16)
        s = _dot(wg, vn16[:, g * d:(g + 1) * d]) + bs_ref[g]
        yb_ref[:, g * d:(g + 1) * d] = (u[:, g * d:(g + 1) * d] * s).astype(BF16)


def _gmlp_call(uv, ln_g, ln_b, w_s, b_s, *, bsz, seq, q, want_v):
    groups = w_s.shape[0]
    t = bsz * seq
    gd = uv.shape[1] // 2
    assert seq % q == 0 and (gd // groups) % LANE == 0
    ws = w_s[:, :q, :q]
    bs = b_s[:, :q, None]
    row = pl.BlockSpec((q, gd), lambda i: (i, 0))
    kern = functools.partial(_gmlp_kernel, groups=groups, q=q)
    out_shape = [jax.ShapeDtypeStruct((t, gd), BF16)] + ([jax.ShapeDtypeStruct((t, gd), F32)] if want_v else [])
    return pl.pallas_call(
        kern, grid=(t // q,),
        in_specs=[pl.BlockSpec((q, gd), lambda i: (i, 0)), pl.BlockSpec((q, gd), lambda i: (i, 1)),
                  pl.BlockSpec((1, gd), lambda i: (0, 0)), pl.BlockSpec((1, gd), lambda i: (0, 0)),
                  pl.BlockSpec((groups, q, q), lambda i: (0, 0, 0)),
                  pl.BlockSpec((groups, q, 1), lambda i: (0, 0, 0))],
        out_specs=[row] * len(out_shape), out_shape=out_shape,
        compiler_params=_params("parallel"), name="gmlp",
    )(uv, uv, ln_g.reshape(1, gd), ln_b.reshape(1, gd), ws, bs)


def _merge_kernel(ya_ref, yb_ref, wa_ref, wb_ref, ga_ref, gb_ref, o_ref):
    a = _dot(ya_ref[...], wa_ref[...])
    b = _dot(yb_ref[...], wb_ref[...])
    o_ref[...] = (ga_ref[...] * a + gb_ref[...] * b).astype(o_ref.dtype)


def _merge_call(ya, yb, gates, w_a, w_b, layer):
    t, ka = ya.shape
    kb = yb.shape[1]
    d = w_a.shape[2]
    tm, tn = _pick(t, 1024), _pick(d, 512)
    nd = d // tn
    return pl.pallas_call(
        _merge_kernel, grid=(t // tm, nd),
        in_specs=[pl.BlockSpec((tm, ka), lambda i, j: (i, 0)), pl.BlockSpec((tm, kb), lambda i, j: (i, 0)),
                  pl.BlockSpec((None, ka, tn), lambda i, j: (layer, 0, j)),
                  pl.BlockSpec((None, kb, tn), lambda i, j: (layer, 0, j)),
                  pl.BlockSpec((tm, tn), lambda i, j: (i, j)), pl.BlockSpec((tm, tn), lambda i, j: (i, nd + j))],
        out_specs=pl.BlockSpec((tm, tn), lambda i, j: (i, j)),
        out_shape=jax.ShapeDtypeStruct((t, d), BF16),
        compiler_params=_params("parallel", "parallel"), name="merge",
    )(ya, yb, w_a, w_b, gates, gates)


def _mm_res_ln_kernel(a_ref, w_ref, x_ref, g_ref, b_ref, o32_ref, o16_ref, acc_ref, *, alpha, nk):
    k = pl.program_id(1)

    @pl.when(k == 0)
    def _():
        acc_ref[...] = jnp.zeros_like(acc_ref)

    acc_ref[...] += _dot(a_ref[...], w_ref[...])

    @pl.when(k == nk - 1)
    def _():
        y = _ln_rows(alpha * x_ref[...] + acc_ref[...], g_ref[...], b_ref[...])
        o32_ref[...] = y
        o16_ref[...] = y.astype(BF16)


def _k_tile(k, cap):
    if k % LANE != 0:
        return k
    best = LANE
    for m in range(1, k // LANE + 1):
        tk = m * LANE
        if k % tk == 0 and tk <= cap:
            best = tk
    return best


def _mm_res_ln_call(a, w, layer, x, g, b, alpha, name):
    t, k = a.shape
    d = w.shape[2]
    tm = _pick(t, 512)
    tk = _k_tile(k, 2048)
    nk = k // tk
    row = pl.BlockSpec((tm, d), lambda i, kk: (i, 0))
    vec = pl.BlockSpec((1, d), lambda i, kk: (0, 0))
    kern = functools.partial(_mm_res_ln_kernel, alpha=alpha, nk=nk)
    return pl.pallas_call(
        kern, grid=(t // tm, nk),
        in_specs=[pl.BlockSpec((tm, tk), lambda i, kk: (i, kk)),
                  pl.BlockSpec((None, tk, d), lambda i, kk: (layer, kk, 0)), row, vec, vec],
        out_specs=[row, row],
        out_shape=[jax.ShapeDtypeStruct((t, d), F32), jax.ShapeDtypeStruct((t, d), BF16)],
        scratch_shapes=[pltpu.VMEM((tm, d), F32)],
        compiler_params=_params("parallel", "arbitrary"), name=name,
    )(a, w, x, g.reshape(1, d), b.reshape(1, d))


def _swiglu_up_kernel(meta_ref, x_ref, wg_ref, wu_ref, o_ref):
    live = pl.program_id(0) < meta_ref[0]

    @pl.when(live)
    def _():
        x = x_ref[...].astype(BF16)
        g = _dot(x, wg_ref[...])
        u = _dot(x, wu_ref[...])
        o_ref[...] = (_silu(g) * u).astype(o_ref.dtype)

    @pl.when(jnp.logical_not(live))
    def _():
        o_ref[...] = jnp.zeros_like(o_ref)


def _swiglu_up_call(meta, x, wg, wu, layer, tm, name):
    t, k = x.shape
    n = wg.shape[3]
    tn = _pick(n, 1024)
    nn = n // tn

    def live(i, m):
        return jnp.minimum(i, m[0] - 1)

    def col(i, j, m):
        return jnp.where(i < m[0], j, nn - 1)

    wspec = pl.BlockSpec((None, None, k, tn), lambda i, j, m: (layer, m[1 + live(i, m)], 0, col(i, j, m)))
    grid_spec = pltpu.PrefetchScalarGridSpec(
        num_scalar_prefetch=1, grid=(t // tm, nn),
        in_specs=[pl.BlockSpec((tm, k), lambda i, j, m: (live(i, m), 0)), wspec, wspec],
        out_specs=pl.BlockSpec((tm, tn), lambda i, j, m: (i, j)))
    return pl.pallas_call(
        _swiglu_up_kernel, grid_spec=grid_spec, out_shape=jax.ShapeDtypeStruct((t, n), BF16),
        compiler_params=_params("arbitrary", "arbitrary"), name=name,
    )(meta, x, wg, wu)


def _moe_down_kernel(meta_ref, h_ref, w_ref, o_ref, acc_ref, *, nk):
    k = pl.program_id(1)
    live = pl.program_id(0) < meta_ref[0]

    @pl.when(live)
    def _():
        @pl.when(k == 0)
        def _():
            acc_ref[...] = jnp.zeros_like(acc_ref)

        acc_ref[...] += _dot(h_ref[...], w_ref[...])

        @pl.when(k == nk - 1)
        def _():
            o_ref[...] = acc_ref[...]

    @pl.when(jnp.logical_not(live) & (k == nk - 1))
    def _():
        o_ref[...] = jnp.zeros_like(o_ref)


def _moe_down_call(meta, h, wd, layer, tm):
    t, k = h.shape
    d = wd.shape[3]
    tk = _k_tile(k, 2048)
    nk = k // tk

    def live(i, m):
        return jnp.minimum(i, m[0] - 1)

    def kk(i, k_, m):
        return jnp.where(i < m[0], k_, nk - 1)

    grid_spec = pltpu.PrefetchScalarGridSpec(
        num_scalar_prefetch=1, grid=(t // tm, nk),
        in_specs=[pl.BlockSpec((tm, tk), lambda i, k_, m: (live(i, m), kk(i, k_, m))),
                  pl.BlockSpec((None, None, tk, d), lambda i, k_, m: (layer, m[1 + live(i, m)], kk(i, k_, m), 0))],
        out_specs=pl.BlockSpec((tm, d), lambda i, k_, m: (i, 0)),
        scratch_shapes=[pltpu.VMEM((tm, d), F32)])
    return pl.pallas_call(
        functools.partial(_moe_down_kernel, nk=nk), grid_spec=grid_spec,
        out_shape=jax.ShapeDtypeStruct((t, d), F32),
        compiler_params=_params("arbitrary", "arbitrary"), name="moe_down",
    )(meta, h, wd)


def _router_kernel(x_ref, wr_ref, e_ref, w_ref, r_ref, cnt_ref, carry_ref, *, n_exp, tm):
    @pl.when(pl.program_id(0) == 0)
    def _():
        carry_ref[...] = jnp.zeros_like(carry_ref)

    logits = _dot_split(x_ref[...], wr_ref[...])
    lane = lax.broadcasted_iota(I32, logits.shape, 1)
    lg = jnp.where(lane < n_exp, logits, -jnp.inf)
    m1 = jnp.max(lg, axis=-1, keepdims=True)
    i1 = jnp.min(jnp.where(lg == m1, lane, LANE), axis=-1, keepdims=True)
    lg2 = jnp.where(lane == i1, -jnp.inf, lg)
    m2 = jnp.max(lg2, axis=-1, keepdims=True)
    i2 = jnp.min(jnp.where(lg2 == m2, lane, LANE), axis=-1, keepdims=True)
    ex = jnp.exp(m2 - m1)
    w1 = 1.0 / (1.0 + ex)
    w2 = ex / (1.0 + ex)

    oh1 = (lane == i1).astype(F32)
    oh2 = (lane == i2).astype(F32)
    both = oh1 + oh2
    ri = lax.broadcasted_iota(I32, (tm, tm), 0)
    ci = lax.broadcasted_iota(I32, (tm, tm), 1)
    before = _dot((ri > ci).astype(BF16), both.astype(BF16)) + carry_ref[0:1, :]
    r1 = jnp.sum(before * oh1, axis=-1, keepdims=True)
    r2 = jnp.sum(before * oh2, axis=-1, keepdims=True)
    carry_ref[...] = carry_ref[...] + jnp.sum(both, axis=0, keepdims=True)

    e_ref[...] = jnp.where(lane == 0, i1, jnp.where(lane == 1, i2, 0))
    w_ref[...] = jnp.where(lane == 0, w1, jnp.where(lane == 1, w2, 0.0))
    r_ref[...] = jnp.where(lane == 0, r1, jnp.where(lane == 1, r2, 0.0)).astype(I32)
    cnt_ref[...] = carry_ref[...].astype(I32)


def _router_call(x, w_router):
    t, d = x.shape
    n_exp = w_router.shape[1]
    wr = jnp.zeros((d, LANE), F32).at[:, :n_exp].set(w_router)
    tm = _pick(t, 512)
    row = pl.BlockSpec((tm, LANE), lambda i: (i, 0))
    kern = functools.partial(_router_kernel, n_exp=n_exp, tm=tm)
    return pl.pallas_call(
        kern, grid=(t // tm,),
        in_specs=[pl.BlockSpec((tm, d), lambda i: (i, 0)), pl.BlockSpec((d, LANE), lambda i: (0, 0))],
        out_specs=[row, row, row, pl.BlockSpec((SUBLANE, LANE), lambda i: (0, 0))],
        out_shape=[jax.ShapeDtypeStruct((t, LANE), I32), jax.ShapeDtypeStruct((t, LANE), F32),
                   jax.ShapeDtypeStruct((t, LANE), I32), jax.ShapeDtypeStruct((SUBLANE, LANE), I32)],
        scratch_shapes=[pltpu.VMEM((SUBLANE, LANE), F32)],
        compiler_params=_params("arbitrary"), name="router",
    )(x, wr)


_DMA_UNROLL = 8


def _row_copy(src_hbm, dst_vmem, src_row, dst_row, sem):
    return pltpu.make_async_copy(src_hbm.at[pl.ds(src_row, 1)], dst_vmem.at[pl.ds(dst_row, 1)], sem)


def _gather_kernel(tok_ref, x_hbm, o_ref, sem, *, rows):
    def start(r, carry):
        _row_copy(x_hbm, o_ref, tok_ref[0, r], r, sem).start()
        return carry

    lax.fori_loop(0, rows, start, 0, unroll=_DMA_UNROLL)
    pltpu.make_async_copy(x_hbm.at[pl.ds(0, rows)], o_ref, sem).wait()


def _gather_call(x, tok, rows):
    cap = tok.shape[0]
    d = x.shape[1]
    tok3 = tok.reshape(cap // rows, 1, rows)
    return pl.pallas_call(
        functools.partial(_gather_kernel, rows=rows), grid=(cap // rows,),
        in_specs=[pl.BlockSpec((None, 1, rows), lambda i: (i, 0, 0), memory_space=pltpu.SMEM),
                  pl.BlockSpec(memory_space=pl.ANY)],
        out_specs=pl.BlockSpec((rows, d), lambda i: (i, 0)),
        out_shape=jax.ShapeDtypeStruct((cap, d), x.dtype),
        scratch_shapes=[pltpu.SemaphoreType.DMA(())],
        compiler_params=_params("arbitrary"), name="moe_gather",
    )(tok3, x)


def _combine_ln_kernel(d0_ref, d1_ref, o_hbm, w_ref, x_ref, g_ref, b_ref, o32_ref, o16_ref, buf0, buf1, sem,
                       *, alpha, rows):
    def start(r, carry):
        _row_copy(o_hbm, buf0, d0_ref[0, r], r, sem).start()
        _row_copy(o_hbm, buf1, d1_ref[0, r], r, sem).start()
        return carry

    lax.fori_loop(0, rows, start, 0, unroll=_DMA_UNROLL)
    pltpu.make_async_copy(o_hbm.at[pl.ds(0, rows)], buf0, sem).wait()
    pltpu.make_async_copy(o_hbm.at[pl.ds(0, rows)], buf1, sem).wait()
    wts = w_ref[...]
    ffn = wts[:, 0:1] * buf0[...] + wts[:, 1:2] * buf1[...]
    y = _ln_rows(alpha * x_ref[...] + ffn, g_ref[...], b_ref[...])
    o32_ref[...] = y
    o16_ref[...] = y.astype(BF16)


def _combine_ln_call(dest, o_sorted, wts, x, g, b, alpha):
    t, d = x.shape
    rows = _pick(t, 512)
    d0 = dest[:, 0].reshape(t // rows, 1, rows)
    d1 = dest[:, 1].reshape(t // rows, 1, rows)
    idx = pl.BlockSpec((None, 1, rows), lambda i: (i, 0, 0), memory_space=pltpu.SMEM)
    row = pl.BlockSpec((rows, d), lambda i: (i, 0))
    vec = pl.BlockSpec((1, d), lambda i: (0, 0))
    kern = functools.partial(_combine_ln_kernel, alpha=alpha, rows=rows)
    return pl.pallas_call(
        kern, grid=(t // rows,),
        in_specs=[idx, idx, pl.BlockSpec(memory_space=pl.ANY), pl.BlockSpec((rows, LANE), lambda i: (i, 0)),
                  row, vec, vec],
        out_specs=[row, row],
        out_shape=[jax.ShapeDtypeStruct((t, d), F32), jax.ShapeDtypeStruct((t, d), BF16)],
        scratch_shapes=[pltpu.VMEM((rows, d), F32), pltpu.VMEM((rows, d), F32), pltpu.SemaphoreType.DMA(())],
        compiler_params=_params("arbitrary"), name="moe_combine_ln",
    )(d0, d1, o_sorted, wts, x, g.reshape(1, d), b.reshape(1, d))


def _moe_layer(xf, w_router, wg, wu, wd, layer, ln_g, ln_b, alpha):
    t, d = xf.shape
    n_exp = wg.shape[1]
    e_out, w_out, r_out, cnt = _router_call(xf, w_router)
    e = e_out[:, :MOE_TOP_K]
    counts = cnt[0, :n_exp]

    tm = 512 if t * MOE_TOP_K >= 8 * 512 else 128
    n_tiles = -(-(t * MOE_TOP_K) // tm) + n_exp
    cap = n_tiles * tm
    padded = (counts + tm - 1) // tm * tm
    pad_end = jnp.cumsum(padded)
    pad_start = pad_end - padded
    dest = (pad_start[e] + r_out[:, :MOE_TOP_K]).astype(I32)
    tok = jnp.repeat(jnp.arange(t, dtype=I32), MOE_TOP_K)
    slot_tok = jnp.zeros((cap,), I32).at[dest.reshape(-1)].set(tok)
    tile_expert = jnp.minimum(jnp.searchsorted(pad_end, jnp.arange(n_tiles, dtype=I32) * tm, side="right"),
                              n_exp - 1)
    meta = jnp.concatenate([(pad_end[-1:] // tm), tile_expert]).astype(I32)

    xs = _gather_call(xf, slot_tok, _pick(cap, 1024))
    h = _swiglu_up_call(meta, xs, wg, wu, layer, tm, "moe_up")
    o_sorted = _moe_down_call(meta, h, wd, layer, tm)
    return _combine_ln_call(dest, o_sorted, w_out, xf, ln_g, ln_b, alpha)


def _trunk(x, conv_bufs, ssm_states, pr, want_v):
    bsz, seq, d = x.shape
    t = bsz * seq
    depth = pr["w_main"].shape[0]
    alpha = float((2 * depth) ** 0.25)
    d_inner = pr["w_br_a"].shape[1]
    cdim = pr["conv_w"].shape[2]
    gd = pr["w_br_b"].shape[1]
    kc = pr["conv_w"].shape[1]
    off_xbc = d_inner
    off_u = d_inner + cdim
    off_gate = off_u + 2 * gd
    q_ssd = _pick(seq, 128)
    n_groups = (cdim - d_inner) // (2 * ssm_states.shape[-1])
    gs_ssd = next(c for c in (4, 2, 1) if n_groups % c == 0)
    q_gmlp = min(pr["w_s"].shape[2], seq)

    xf, xb = _ln_call(x.reshape(t, d), pr["ln_in_g"], pr["ln_in_b"])
    convs, ssms, vs = [], [], []
    w_main = pr["w_main"]
    lane_aligned = q_ssd % LANE == 0
    for i in range(depth):
        sz = _proj_call(xb, w_main, i, 0, d_inner, "silu", BF16, "in_proj_z")
        uv = _proj_call(xb, w_main, i, off_u, 2 * gd, "gelu", F32, "in_proj_uv")
        gates = _proj_call(xb, w_main, i, off_gate, 2 * d, "none", BF16, "in_proj_gate", bias=pr["b_gate"][i])
        dt, acum = _dt_call(xf, pr["w_dt"][i], pr["dt_bias"][i], pr["a_log_pad"][i], q_ssd)
        if lane_aligned:
            xbc, tails = _proj_conv_call(xb, w_main, i, off_xbc, pr["conv_w"][i], pr["conv_b"][i], conv_bufs[i], seq)
            ya, ssm_i = _ssd_cm_call(xbc, sz, dt, acum, ssm_states[i], pr["d_skip"][i], pr["ssd_norm_g"][i],
                                     bsz=bsz, seq=seq, q=q_ssd, gs=gs_ssd)
            new_rows = tails.reshape(bsz, -1, _CONV_PAD, cdim)[:, -1, _CONV_PAD - (kc - 1):]
        else:
            xbc = _proj_call(xb, w_main, i, off_xbc, cdim, "none", F32, "in_proj_xbc")
            ya, ssm_i = _ssd_call(xbc, sz, dt, conv_bufs[i], ssm_states[i], pr["conv_w"][i], pr["conv_b"][i],
                                  pr["a_log"][i], pr["d_skip"][i], pr["ssd_norm_g"][i], bsz=bsz, seq=seq, q=q_ssd)
            new_rows = xbc.reshape(bsz, seq, cdim)[:, max(seq - (kc - 1), 0):]
        conv_i = jnp.concatenate([conv_bufs[i], new_rows], axis=1)[:, -(kc - 1):]
        yb, *vn = _gmlp_call(uv, pr["gmlp_ln_g"][i], pr["gmlp_ln_b"][i], pr["w_s"][i], pr["b_s"][i],
                             bsz=bsz, seq=seq, q=q_gmlp, want_v=want_v)
        merged = _merge_call(ya, yb, gates, pr["w_br_a"], pr["w_br_b"], i)
        xf, xb = _mm_res_ln_call(merged, pr["w_o"], i, xf, pr["ln1_g"][i], pr["ln1_b"][i], alpha, "out_proj_ln")
        j = i // 2
        if i % 2 == 0:
            tm = _pick(t, 1024)
            meta = jnp.concatenate([jnp.full((1,), t // tm, I32), jnp.zeros((t // tm,), I32)])
            h = _swiglu_up_call(meta, xb, pr["w_ff_gate"], pr["w_ff_up"], j, tm, "ffn_up")
            xf, xb = _mm_res_ln_call(h, pr["w_ff_down"], j, xf, pr["ln2_g"][i], pr["ln2_b"][i], alpha,
                                     "ffn_down_ln")
        else:
            xf, xb = _moe_layer(xf, pr["w_router"][j], pr["w_moe_gate"], pr["w_moe_up"], pr["w_moe_down"], j,
                                pr["ln2_g"][i], pr["ln2_b"][i], alpha)
        convs.append(conv_i)
        ssms.append(ssm_i)
        vs.extend(v.reshape(bsz, seq, gd) for v in vn)
    return xf.reshape(bsz, seq, d), jnp.stack(convs), jnp.stack(ssms), (jnp.stack(vs) if want_v else None)


def kernel(x_prompt, x_sample, cache_conv, state_ssm, ln_in_g, ln_in_b, w_in, conv_w, conv_b, dt_bias, a_log,
           d_skip, ssd_norm_g, gmlp_ln_g, gmlp_ln_b, w_s, b_s, b_gate, w_br_a, w_br_b, w_o, ln1_g, ln1_b,
           w_ff_gate, w_ff_up, w_ff_down, w_router, w_moe_gate, w_moe_up, w_moe_down, ln2_g, ln2_b):
    depth, d_model, _ = w_in.shape
    d_inner = w_br_a.shape[1]
    cdim = conv_w.shape[2]
    heads = a_log.shape[1]
    o_dt = d_inner + cdim
    w_in16 = w_in.astype(BF16)
    w_main = jnp.concatenate([w_in16[:, :, :o_dt], w_in16[:, :, o_dt + heads:]], axis=2)
    w_dt = jnp.zeros((depth, d_model, LANE), F32).at[:, :, :heads].set(w_in[:, :, o_dt:o_dt + heads])
    dt_b = jnp.zeros((depth, 1, LANE), F32).at[:, 0, :heads].set(dt_bias)
    al_pad = jnp.zeros((depth, 1, LANE), F32).at[:, 0, :heads].set(a_log)
    pr = dict(a_log_pad=al_pad,
        ln_in_g=ln_in_g, ln_in_b=ln_in_b, w_main=w_main, w_dt=w_dt, dt_bias=dt_b, conv_w=conv_w, conv_b=conv_b,
        a_log=a_log, d_skip=d_skip, ssd_norm_g=ssd_norm_g, gmlp_ln_g=gmlp_ln_g, gmlp_ln_b=gmlp_ln_b, w_s=w_s,
        b_s=b_s, b_gate=b_gate, w_br_a=w_br_a.astype(BF16), w_br_b=w_br_b.astype(BF16), w_o=w_o.astype(BF16),
        ln1_g=ln1_g, ln1_b=ln1_b, w_ff_gate=w_ff_gate.astype(BF16)[:, None], w_ff_up=w_ff_up.astype(BF16)[:, None],
        w_ff_down=w_ff_down.astype(BF16), w_router=w_router, w_moe_gate=w_moe_gate.astype(BF16),
        w_moe_up=w_moe_up.astype(BF16), w_moe_down=w_moe_down.astype(BF16), ln2_g=ln2_g, ln2_b=ln2_b)
    bp = x_prompt.shape[0]
    zero_conv = jnp.zeros((depth, bp) + cache_conv.shape[2:], x_prompt.dtype)
    zero_ssm = jnp.zeros((depth, bp) + state_ssm.shape[2:], state_ssm.dtype)
    y_prompt, prompt_conv, prompt_ssm, _ = _trunk(x_prompt, zero_conv, zero_ssm, pr, want_v=False)
    y_sample, sample_conv, sample_ssm, sample_v = _trunk(x_sample, cache_conv, state_ssm, pr, want_v=True)
    return (y_prompt, y_sample, prompt_conv, prompt_ssm, sample_conv, sample_ssm, sample_v)
```

```python
import functools

import jax
import jax.numpy as jnp
from jax import lax
from jax.experimental import pallas as pl
from jax.experimental.pallas import tpu as pltpu

F32 = jnp.float32
BF16 = jnp.bfloat16
I32 = jnp.int32
HIGHEST = lax.Precision.HIGHEST

LN_EPS = 1e-5
LANE = 128
SUBLANE = 8
V7X_VMEM_BYTES = 64 * 1024 * 1024
VMEM_LIMIT_BYTES = V7X_VMEM_BYTES - 8 * 1024 * 1024
MOE_TOP_K = 2
_NEG_BIG = -1e30
_TILE_CANDIDATES = (1024, 512, 256, 128, 64, 32, 16, 8)


def _pick(n, cap):
    for c in _TILE_CANDIDATES:
        if c <= cap and n % c == 0:
            return c
    raise ValueError(f"no tile for {n}")


def _params(*sem):
    return pltpu.CompilerParams(dimension_semantics=sem, vmem_limit_bytes=VMEM_LIMIT_BYTES)


def _ln_rows(x, g, b):
    mu = jnp.mean(x, axis=-1, keepdims=True)
    xc = x - mu
    var = jnp.mean(xc * xc, axis=-1, keepdims=True)
    return xc * lax.rsqrt(var + LN_EPS) * g + b


def _sigmoid(x):
    return 0.5 * jnp.tanh(0.5 * x) + 0.5


def _silu(x):
    return x * _sigmoid(x)


def _gelu(x):
    return 0.5 * x * (1.0 + lax.erf(x * (2.0 ** -0.5)))


def _dot(a, b):
    return jnp.dot(a, b, preferred_element_type=F32)


def _dot_exact(a, b):
    return jnp.dot(a, b, precision=HIGHEST, preferred_element_type=F32)


def _dot_split(a, b):
    a_hi = a.astype(BF16)
    b_hi = b.astype(BF16)
    a_lo = (a - a_hi.astype(F32)).astype(BF16)
    b_lo = (b - b_hi.astype(F32)).astype(BF16)
    return _dot(a_hi, b_hi) + (_dot(a_hi, b_lo) + _dot(a_lo, b_hi))


def _ln_kernel(x_ref, g_ref, b_ref, o32_ref, o16_ref):
    y = _ln_rows(x_ref[...], g_ref[...], b_ref[...])
    o32_ref[...] = y
    o16_ref[...] = y.astype(BF16)


def _ln_call(x, g, b):
    t, d = x.shape
    tm = _pick(t, 512)
    row = pl.BlockSpec((tm, d), lambda i: (i, 0))
    vec = pl.BlockSpec((1, d), lambda i: (0, 0))
    return pl.pallas_call(
        _ln_kernel, grid=(t // tm,), in_specs=[row, vec, vec], out_specs=[row, row],
        out_shape=[jax.ShapeDtypeStruct((t, d), F32), jax.ShapeDtypeStruct((t, d), BF16)],
        compiler_params=_params("parallel"), name="ln_in",
    )(x, g.reshape(1, d), b.reshape(1, d))


_CONV_PAD = SUBLANE

_ACTIVATIONS = {"none": lambda r: r, "silu": _silu, "gelu": _gelu}


def _proj_kernel(x_ref, w_ref, o_ref, *, act):
    o_ref[...] = _ACTIVATIONS[act](_dot(x_ref[...], w_ref[...])).astype(o_ref.dtype)


def _proj_gate_kernel(x_ref, w_ref, b_ref, o_ref):
    o_ref[...] = _sigmoid(_dot(x_ref[...], w_ref[...]) + b_ref[...]).astype(o_ref.dtype)


def _proj_conv_kernel(x_ref, w_ref, cw_ref, cc_ref, cbuf_ref, o_ref, tail_ref, halo_ref, *, kc, tiles_per_seq):
    i, j = pl.program_id(0), pl.program_id(1)
    tm = x_ref.shape[0]
    r = _dot(x_ref[...], w_ref[...])
    before = jnp.where(lax.rem(i, tiles_per_seq) == 0, cbuf_ref[...], halo_ref[j])
    nb = tm // _CONV_PAD
    blocks = r.reshape(nb, _CONV_PAD, r.shape[1])
    prev_blocks = jnp.concatenate([before[None], blocks[:nb - 1]], axis=0)
    sub = lax.broadcasted_iota(I32, blocks.shape, 1)
    acc = cc_ref[...] + r * cw_ref[kc - 1:kc, :]
    for s in range(1, kc):
        shifted = jnp.where(sub < s, pltpu.roll(prev_blocks, s, 1), pltpu.roll(blocks, s, 1))
        acc = acc + shifted.reshape(r.shape) * cw_ref[kc - 1 - s:kc - s, :]
    o_ref[...] = _silu(acc)
    last_rows = r[tm - _CONV_PAD:tm, :]
    tail_ref[...] = last_rows
    halo_ref[j] = last_rows


def _region_tiles(t, n, col0):
    tm = _pick(t, 1024)
    tn = next(c for c in _TILE_CANDIDATES if n % c == 0 and col0 % c == 0)
    return tm, tn


def _proj_call(x, w, layer, col0, n, act, out_dtype, name, bias=None):
    t, k = x.shape
    tm, tn = _region_tiles(t, n, col0)
    c0 = col0 // tn
    in_specs = [pl.BlockSpec((tm, k), lambda i, j: (i, 0)), pl.BlockSpec((None, k, tn), lambda i, j: (layer, 0, c0 + j))]
    args = [x, w]
    if bias is None:
        kern = functools.partial(_proj_kernel, act=act)
    else:
        kern = _proj_gate_kernel
        in_specs.append(pl.BlockSpec((1, tn), lambda i, j: (0, j)))
        args.append(bias.reshape(1, n))
    return pl.pallas_call(
        kern, grid=(t // tm, n // tn), in_specs=in_specs,
        out_specs=pl.BlockSpec((tm, tn), lambda i, j: (i, j)),
        out_shape=jax.ShapeDtypeStruct((t, n), out_dtype),
        compiler_params=_params("parallel", "parallel"), name=name,
    )(*args)


def _proj_conv_call(x, w, layer, col0, conv_w, conv_b, conv_buf, seq):
    t, k = x.shape
    kc, n = conv_w.shape
    tm, tn = _pick(seq, 1024), _region_tiles(t, n, col0)[1]
    assert t % tm == 0 and kc - 1 <= _CONV_PAD <= tm
    tiles_per_seq = seq // tm
    c0 = col0 // tn
    cbuf = jnp.pad(conv_buf, ((0, 0), (_CONV_PAD - (kc - 1), 0), (0, 0)))
    kern = functools.partial(_proj_conv_kernel, kc=kc, tiles_per_seq=tiles_per_seq)
    return pl.pallas_call(
        kern, grid=(t // tm, n // tn),
        in_specs=[pl.BlockSpec((tm, k), lambda i, j: (i, 0)),
                  pl.BlockSpec((None, k, tn), lambda i, j: (layer, 0, c0 + j)),
                  pl.BlockSpec((kc, tn), lambda i, j: (0, j)),
                  pl.BlockSpec((1, tn), lambda i, j: (0, j)),
                  pl.BlockSpec((None, _CONV_PAD, tn), lambda i, j: (i // tiles_per_seq, 0, j))],
        out_specs=[pl.BlockSpec((tm, tn), lambda i, j: (i, j)),
                   pl.BlockSpec((None, _CONV_PAD, tn), lambda i, j: (i, 0, j))],
        out_shape=[jax.ShapeDtypeStruct((t, n), F32), jax.ShapeDtypeStruct((t // tm, _CONV_PAD, n), F32)],
        scratch_shapes=[pltpu.VMEM((n // tn, _CONV_PAD, tn), F32)],
        compiler_params=_params("arbitrary", "arbitrary"), name="in_proj_conv",
    )(x, w, conv_w, conv_b.reshape(1, n), cbuf)


def _dt_kernel(x_ref, w_ref, b_ref, al_ref, dt_ref, ac_ref, *, q):
    raw = _dot_split(x_ref[...], w_ref[...]) + b_ref[...]
    dt = jnp.maximum(raw, 0.0) + jnp.log1p(jnp.exp(-jnp.abs(raw)))
    dt_ref[...] = dt
    a = dt * (-jnp.exp(al_ref[...]))
    ri = lax.broadcasted_iota(I32, (q, q), 0)
    ci = lax.broadcasted_iota(I32, (q, q), 1)
    tril = (ri >= ci).astype(F32)
    for c in range(a.shape[0] // q):
        ac_ref[c * q:(c + 1) * q, :] = _dot_exact(tril, a[c * q:(c + 1) * q, :])


def _dt_call(x, w, bias, a_log, q):
    t, k = x.shape
    n = w.shape[1]
    tm = max(_pick(t, 512), q)
    assert tm % q == 0 and t % tm == 0
    row = pl.BlockSpec((tm, n), lambda i: (i, 0))
    vec = pl.BlockSpec((1, n), lambda i: (0, 0))
    return pl.pallas_call(
        functools.partial(_dt_kernel, q=q), grid=(t // tm,),
        in_specs=[pl.BlockSpec((tm, k), lambda i: (i, 0)), pl.BlockSpec((k, n), lambda i: (0, 0)), vec, vec],
        out_specs=[row, row],
        out_shape=[jax.ShapeDtypeStruct((t, n), F32), jax.ShapeDtypeStruct((t, n), F32)],
        compiler_params=_params("parallel"), name="dt_proj",
    )(x, w, bias, a_log)


def _ssd_kernel(xs_ref, bm_ref, cm_ref, sz_ref, dt_ref, dtt_ref, cbx_ref, cbb_ref, cbc_ref, s0_ref,
                cwx_ref, cwb_ref, cwc_ref, ccx_ref, ccb_ref, ccc_ref, alr_ref, alc_ref, dsk_ref, ng_ref,
                y_ref, sout_ref, xpad_ref, st_ref, *, q, hpg, p, n, kc, nc):
    c = pl.program_id(2)
    w = hpg * p
    tail = kc - 1
    t0 = _CONV_PAD - tail

    @pl.when(c == 0)
    def _():
        xpad_ref[t0:_CONV_PAD, 0:w] = cbx_ref[...]
        xpad_ref[t0:_CONV_PAD, w:w + n] = cbb_ref[...]
        xpad_ref[t0:_CONV_PAD, w + n:w + 2 * n] = cbc_ref[...]
        st_ref[...] = s0_ref[...].reshape(w, n).T

    xpad_ref[_CONV_PAD:_CONV_PAD + q, 0:w] = xs_ref[...]
    xpad_ref[_CONV_PAD:_CONV_PAD + q, w:w + n] = bm_ref[...]
    xpad_ref[_CONV_PAD:_CONV_PAD + q, w + n:w + 2 * n] = cm_ref[...]

    def conv(lo, hi, cw_ref, cc_ref):
        acc = cc_ref[...]
        for k in range(kc):
            acc = acc + xpad_ref[t0 + k:t0 + k + q, lo:hi] * cw_ref[k:k + 1, :]
        return _silu(acc)

    xs = conv(0, w, cwx_ref, ccx_ref)
    bm = conv(w, w + n, cwb_ref, ccb_ref)
    cm = conv(w + n, w + 2 * n, cwc_ref, ccc_ref)
    xpad_ref[t0:_CONV_PAD, :] = xpad_ref[t0 + q:_CONV_PAD + q, :]

    a = dt_ref[...] * (-jnp.exp(alr_ref[...]))
    a_t = dtt_ref[...] * (-jnp.exp(alc_ref[...]))
    ri = lax.broadcasted_iota(I32, (q, q), 0)
    ci = lax.broadcasted_iota(I32, (q, q), 1)
    tril = ri >= ci
    acol = _dot_exact(tril.astype(F32), a)
    arow = _dot_exact(a_t, (ri <= ci).astype(F32))

    hrow = lax.broadcasted_iota(I32, (hpg, w), 0)
    hlane = lax.broadcasted_iota(I32, (hpg, w), 1)
    expand = ((hlane >= hrow * p) & (hlane < (hrow + 1) * p)).astype(F32)
    xdt = xs * _dot_exact(dt_ref[...], expand)
    eacol = _dot_exact(jnp.exp(acol), expand)
    to_end = _dot_exact(jnp.exp(acol[q - 1:q, :] - acol), expand)

    bm16 = bm.astype(BF16)
    cm16 = cm.astype(BF16)
    cb = lax.dot_general(cm16, bm16, (((1,), (1,)), ((), ())), preferred_element_type=F32)
    st = st_ref[...]
    y = _dot(cm16, st.astype(BF16)) * eacol
    lane = lax.broadcasted_iota(I32, (q, w), 1)
    for h in range(hpg):
        seg = acol[:, h:h + 1] - arow[h:h + 1, :]
        m = (cb * jnp.exp(jnp.where(tril, seg, _NEG_BIG))).astype(BF16)
        head = (lane >= h * p) & (lane < (h + 1) * p)
        y = y + _dot(m, jnp.where(head, xdt, 0.0).astype(BF16))
    y = y + dsk_ref[...] * xs
    hz = y * sz_ref[...]
    y_ref[...] = (hz * lax.rsqrt(jnp.mean(hz * hz, axis=-1, keepdims=True) + LN_EPS) * ng_ref[...]).astype(BF16)

    upd = lax.dot_general(bm16, (xdt * to_end).astype(BF16), (((0,), (0,)), ((), ())),
                          preferred_element_type=F32)
    st_ref[...] = st * eacol[q - 1:q, :] + upd

    @pl.when(c == nc - 1)
    def _():
        sout_ref[...] = st_ref[...].T.reshape(hpg, p, n)


def _ssd_call(xbc, sz, dt, conv_buf, state0, conv_w, conv_b, a_log, d_skip, norm_g, *, bsz, seq, q):
    heads, p, n = state0.shape[1:]
    kc, cdim = conv_w.shape
    d_inner = sz.shape[1]
    groups = (cdim - d_inner) // (2 * n)
    hpg = heads // groups
    w = hpg * p
    nc = seq // q
    t = bsz * seq
    assert w * groups == d_inner and seq % q == 0 and q >= kc - 1 and d_inner % n == 0

    dt4 = dt[:, :heads].reshape(bsz, seq, groups, hpg).transpose(0, 2, 1, 3)
    dtt4 = dt4.transpose(0, 1, 3, 2)
    alr = a_log.reshape(groups, 1, hpg)
    alc = a_log.reshape(groups, hpg, 1)
    dsk = jnp.repeat(d_skip, p).reshape(1, d_inner)
    ng = norm_g.reshape(1, d_inner)
    ccb = conv_b.reshape(1, cdim)

    bb, cb_ = d_inner // n, d_inner // n + groups
    row = lambda b, g, c: b * nc + c
    in_specs = [
        pl.BlockSpec((q, w), lambda b, g, c: (row(b, g, c), g)),
        pl.BlockSpec((q, n), lambda b, g, c: (row(b, g, c), bb + g)),
        pl.BlockSpec((q, n), lambda b, g, c: (row(b, g, c), cb_ + g)),
        pl.BlockSpec((q, w), lambda b, g, c: (row(b, g, c), g)),
        pl.BlockSpec((None, None, q, hpg), lambda b, g, c: (b, g, c, 0)),
        pl.BlockSpec((None, None, hpg, q), lambda b, g, c: (b, g, 0, c)),
        pl.BlockSpec((None, kc - 1, w), lambda b, g, c: (b, 0, g)),
        pl.BlockSpec((None, kc - 1, n), lambda b, g, c: (b, 0, d_inner // n + g)),
        pl.BlockSpec((None, kc - 1, n), lambda b, g, c: (b, 0, d_inner // n + groups + g)),
        pl.BlockSpec((None, hpg, p, n), lambda b, g, c: (b, g, 0, 0)),
        pl.BlockSpec((kc, w), lambda b, g, c: (0, g)),
        pl.BlockSpec((kc, n), lambda b, g, c: (0, d_inner // n + g)),
        pl.BlockSpec((kc, n), lambda b, g, c: (0, d_inner // n + groups + g)),
        pl.BlockSpec((1, w), lambda b, g, c: (0, g)),
        pl.BlockSpec((1, n), lambda b, g, c: (0, d_inner // n + g)),
        pl.BlockSpec((1, n), lambda b, g, c: (0, d_inner // n + groups + g)),
        pl.BlockSpec((None, 1, hpg), lambda b, g, c: (g, 0, 0)),
        pl.BlockSpec((None, hpg, 1), lambda b, g, c: (g, 0, 0)),
        pl.BlockSpec((1, w), lambda b, g, c: (0, g)),
        pl.BlockSpec((1, w), lambda b, g, c: (0, g)),
    ]
    out_specs = [
        pl.BlockSpec((q, w), lambda b, g, c: (row(b, g, c), g)),
        pl.BlockSpec((None, hpg, p, n), lambda b, g, c: (b, g, 0, 0)),
    ]
    kern = functools.partial(_ssd_kernel, q=q, hpg=hpg, p=p, n=n, kc=kc, nc=nc)
    return pl.pallas_call(
        kern, grid=(bsz, groups, nc), in_specs=in_specs, out_specs=out_specs,
        out_shape=[jax.ShapeDtypeStruct((t, d_inner), BF16), jax.ShapeDtypeStruct(state0.shape, F32)],
        scratch_shapes=[pltpu.VMEM((_CONV_PAD + q, w + 2 * n), F32), pltpu.VMEM((n, w), F32)],
        compiler_params=_params("parallel", "parallel", "arbitrary"), name="ssd",
    )(xbc, xbc, xbc, sz, dt4, dtt4, conv_buf, conv_buf, conv_buf, state0,
      conv_w, conv_w, conv_w, ccb, ccb, ccb, alr, alc, dsk, ng)


def _ssd_cm_kernel(xs_ref, bm_ref, cm_ref, sz_ref, dtt_ref, act_ref, ac_ref, s0_ref, dsk_ref, ng_ref,
                   y_ref, sout_ref, st_ref, *, q, gs, hpg, p, n, nc):
    c = pl.program_id(2)
    w = hpg * p

    @pl.when(c == 0)
    def _():
        st_ref[...] = s0_ref[...].reshape(gs * w, n)

    def rows(v):
        return jnp.concatenate([jnp.broadcast_to(v[h:h + 1, :], (p, v.shape[1])) for h in range(hpg)], axis=0)

    si = lax.broadcasted_iota(I32, (q, q), 0)
    li = lax.broadcasted_iota(I32, (q, q), 1)
    keep = si <= li
    nt = (((1,), (1,)), ((), ()))
    for g in range(gs):
        xs = xs_ref[:, g * w:(g + 1) * w]
        bm16 = bm_ref[:, g * n:(g + 1) * n].astype(BF16)
        cm16 = cm_ref[:, g * n:(g + 1) * n].astype(BF16)
        arow = act_ref[g]
        acol = ac_ref[g]
        ear = jnp.exp(arow)
        te = jnp.exp(arow[:, q - 1:q] - arow)
        xdt_t = xs.T * rows(dtt_ref[g])
        xdt16 = xdt_t.astype(BF16)
        cb_t = lax.dot_general(bm16, cm16, nt, preferred_element_type=F32)
        ys = []
        for h in range(hpg):
            seg = arow[h:h + 1, :] - acol[:, h:h + 1]
            m_t = (cb_t * jnp.exp(jnp.where(keep, seg, _NEG_BIG))).astype(BF16)
            ys.append(_dot(xdt16[h * p:(h + 1) * p, :], m_t))
        st = st_ref[g * w:(g + 1) * w, :]
        y_t = jnp.concatenate(ys, axis=0)
        y_t = y_t + lax.dot_general(st.astype(BF16), cm16, nt, preferred_element_type=F32) * rows(ear)
        y = y_t.T + dsk_ref[:, g * w:(g + 1) * w] * xs
        hz = y * sz_ref[:, g * w:(g + 1) * w]
        y_ref[:, g * w:(g + 1) * w] = (hz * lax.rsqrt(jnp.mean(hz * hz, axis=-1, keepdims=True) + LN_EPS)
                                       * ng_ref[:, g * w:(g + 1) * w]).astype(BF16)
        upd = _dot((xdt_t * rows(te)).astype(BF16), bm16)
        st_ref[g * w:(g + 1) * w, :] = st * rows(ear[:, q - 1:q]) + upd

    @pl.when(c == nc - 1)
    def _():
        sout_ref[...] = st_ref[...].reshape(gs * hpg, p, n)


def _ssd_cm_call(xbc, sz, dt, acum, state0, d_skip, norm_g, *, bsz, seq, q, gs):
    heads, p, n = state0.shape[1:]
    cdim = xbc.shape[1]
    d_inner = sz.shape[1]
    groups = (cdim - d_inner) // (2 * n)
    hpg = heads // groups
    w = hpg * p
    nc = seq // q
    t = bsz * seq
    gw, gn = gs * w, gs * n
    assert w * groups == d_inner and seq % q == 0 and q % LANE == 0 and groups % gs == 0
    assert d_inner % gn == 0 and (groups * n) % gn == 0

    def heads_major(v):
        return v[:, :heads].reshape(bsz, seq, groups, hpg).transpose(0, 2, 3, 1)

    dtt4 = heads_major(dt)
    act4 = heads_major(acum)
    ac4 = act4.transpose(0, 1, 3, 2)
    dsk = jnp.repeat(d_skip, p).reshape(1, d_inner)
    ng = norm_g.reshape(1, d_inner)

    bb, cb_ = d_inner // gn, (d_inner + groups * n) // gn
    row = lambda b, g, c: b * nc + c
    in_specs = [
        pl.BlockSpec((q, gw), lambda b, g, c: (row(b, g, c), g)),
        pl.BlockSpec((q, gn), lambda b, g, c: (row(b, g, c), bb + g)),
        pl.BlockSpec((q, gn), lambda b, g, c: (row(b, g, c), cb_ + g)),
        pl.BlockSpec((q, gw), lambda b, g, c: (row(b, g, c), g)),
        pl.BlockSpec((None, gs, hpg, q), lambda b, g, c: (b, g, 0, c)),
        pl.BlockSpec((None, gs, hpg, q), lambda b, g, c: (b, g, 0, c)),
        pl.BlockSpec((None, gs, q, hpg), lambda b, g, c: (b, g, c, 0)),
        pl.BlockSpec((None, gs * hpg, p, n), lambda b, g, c: (b, g, 0, 0)),
        pl.BlockSpec((1, gw), lambda b, g, c: (0, g)),
        pl.BlockSpec((1, gw), lambda b, g, c: (0, g)),
    ]
    out_specs = [
        pl.BlockSpec((q, gw), lambda b, g, c: (row(b, g, c), g)),
        pl.BlockSpec((None, gs * hpg, p, n), lambda b, g, c: (b, g, 0, 0)),
    ]
    kern = functools.partial(_ssd_cm_kernel, q=q, gs=gs, hpg=hpg, p=p, n=n, nc=nc)
    return pl.pallas_call(
        kern, grid=(bsz, groups // gs, nc), in_specs=in_specs, out_specs=out_specs,
        out_shape=[jax.ShapeDtypeStruct((t, d_inner), BF16), jax.ShapeDtypeStruct(state0.shape, F32)],
        scratch_shapes=[pltpu.VMEM((gs * w, n), F32)],
        compiler_params=_params("parallel", "parallel", "arbitrary"), name="ssd_cm",
    )(xbc, xbc, xbc, sz, dtt4, act4, ac4, state0, dsk, ng)


def _gmlp_kernel(u_ref, v_ref, lg_ref, lb_ref, ws_ref, bs_ref, yb_ref, *vn_refs, groups, q):
    u = u_ref[...]
    vn = _ln_rows(v_ref[...], lg_ref[...], lb_ref[...])
    for vn_ref in vn_refs:
        vn_ref[...] = vn
    vn16 = vn.astype(BF16)
    d = vn.shape[1] // groups
    ri = lax.broadcasted_iota(I32, (q, q), 0)
    ci = lax.broadcasted_iota(I32, (q, q), 1)
    for g in range(groups):
        wg = jnp.where(ri >= ci, ws_ref[g], 0.0).astype(BF16)
        s = _dot(wg, vn16[:, g * d:(g + 1) * d]) + bs_ref[g]
        yb_ref[:, g * d:(g + 1) * d] = (u[:, g * d:(g + 1) * d] * s).astype(BF16)


def _gmlp_call(uv, ln_g, ln_b, w_s, b_s, *, bsz, seq, q, want_v):
    groups = w_s.shape[0]
    t = bsz * seq
    gd = uv.shape[1] // 2
    assert seq % q == 0 and (gd // groups) % LANE == 0
    ws = w_s[:, :q, :q]
    bs = b_s[:, :q, None]
    row = pl.BlockSpec((q, gd), lambda i: (i, 0))
    kern = functools.partial(_gmlp_kernel, groups=groups, q=q)
    out_shape = [jax.ShapeDtypeStruct((t, gd), BF16)] + ([jax.ShapeDtypeStruct((t, gd), F32)] if want_v else [])
    return pl.pallas_call(
        kern, grid=(t // q,),
        in_specs=[pl.BlockSpec((q, gd), lambda i: (i, 0)), pl.BlockSpec((q, gd), lambda i: (i, 1)),
                  pl.BlockSpec((1, gd), lambda i: (0, 0)), pl.BlockSpec((1, gd), lambda i: (0, 0)),
                  pl.BlockSpec((groups, q, q), lambda i: (0, 0, 0)),
                  pl.BlockSpec((groups, q, 1), lambda i: (0, 0, 0))],
        out_specs=[row] * len(out_shape), out_shape=out_shape,
        compiler_params=_params("parallel"), name="gmlp",
    )(uv, uv, ln_g.reshape(1, gd), ln_b.reshape(1, gd), ws, bs)


def _merge_kernel(ya_ref, yb_ref, wa_ref, wb_ref, ga_ref, gb_ref, o_ref):
    a = _dot(ya_ref[...], wa_ref[...])
    b = _dot(yb_ref[...], wb_ref[...])
    o_ref[...] = (ga_ref[...] * a + gb_ref[...] * b).astype(o_ref.dtype)


def _merge_call(ya, yb, gates, w_a, w_b, layer):
    t, ka = ya.shape
    kb = yb.shape[1]
    d = w_a.shape[2]
    tm, tn = _pick(t, 1024), _pick(d, 512)
    nd = d // tn
    return pl.pallas_call(
        _merge_kernel, grid=(t // tm, nd),
        in_specs=[pl.BlockSpec((tm, ka), lambda i, j: (i, 0)), pl.BlockSpec((tm, kb), lambda i, j: (i, 0)),
                  pl.BlockSpec((None, ka, tn), lambda i, j: (layer, 0, j)),
                  pl.BlockSpec((None, kb, tn), lambda i, j: (layer, 0, j)),
                  pl.BlockSpec((tm, tn), lambda i, j: (i, j)), pl.BlockSpec((tm, tn), lambda i, j: (i, nd + j))],
        out_specs=pl.BlockSpec((tm, tn), lambda i, j: (i, j)),
        out_shape=jax.ShapeDtypeStruct((t, d), BF16),
        compiler_params=_params("parallel", "parallel"), name="merge",
    )(ya, yb, w_a, w_b, gates, gates)


def _mm_res_ln_kernel(a_ref, w_ref, x_ref, g_ref, b_ref, o32_ref, o16_ref, acc_ref, *, alpha, nk):
    k = pl.program_id(1)

    def finish(total):
        y = _ln_rows(alpha * x_ref[...] + total, g_ref[...], b_ref[...])
        o32_ref[...] = y
        o16_ref[...] = y.astype(BF16)

    if nk == 1:
        finish(_dot(a_ref[...], w_ref[...]))
        return

    @pl.when(k == 0)
    def _():
        acc_ref[...] = _dot(a_ref[...], w_ref[...])

    @pl.when((k > 0) & (k < nk - 1))
    def _():
        acc_ref[...] += _dot(a_ref[...], w_ref[...])

    @pl.when(k == nk - 1)
    def _():
        finish(acc_ref[...] + _dot(a_ref[...], w_ref[...]))


def _k_tile(k, cap):
    if k % LANE != 0:
        return k
    best = LANE
    for m in range(1, k // LANE + 1):
        tk = m * LANE
        if k % tk == 0 and tk <= cap:
            best = tk
    return best


def _mm_res_ln_call(a, w, layer, x, g, b, alpha, name):
    t, k = a.shape
    d = w.shape[2]
    tm = _pick(t, 512)
    tk = _k_tile(k, 2048)
    nk = k // tk
    row = pl.BlockSpec((tm, d), lambda i, kk: (i, 0))
    vec = pl.BlockSpec((1, d), lambda i, kk: (0, 0))
    kern = functools.partial(_mm_res_ln_kernel, alpha=alpha, nk=nk)
    return pl.pallas_call(
        kern, grid=(t // tm, nk),
        in_specs=[pl.BlockSpec((tm, tk), lambda i, kk: (i, kk)),
                  pl.BlockSpec((None, tk, d), lambda i, kk: (layer, kk, 0)), row, vec, vec],
        out_specs=[row, row],
        out_shape=[jax.ShapeDtypeStruct((t, d), F32), jax.ShapeDtypeStruct((t, d), BF16)],
        scratch_shapes=[pltpu.VMEM((tm, d), F32)],
        compiler_params=_params("parallel", "arbitrary"), name=name,
    )(a, w, x, g.reshape(1, d), b.reshape(1, d))


def _swiglu_up_kernel(meta_ref, x_ref, wg_ref, wu_ref, o_ref):
    live = pl.program_id(0) < meta_ref[0]

    @pl.when(live)
    def _():
        x = x_ref[...].astype(BF16)
        g = _dot(x, wg_ref[...])
        u = _dot(x, wu_ref[...])
        o_ref[...] = (_silu(g) * u).astype(o_ref.dtype)

    @pl.when(jnp.logical_not(live))
    def _():
        o_ref[...] = jnp.zeros_like(o_ref)


def _swiglu_up_call(meta, x, wg, wu, layer, tm, name):
    t, k = x.shape
    n = wg.shape[3]
    tn = _pick(n, 1024)
    nn = n // tn

    def live(i, m):
        return jnp.minimum(i, m[0] - 1)

    def col(i, j, m):
        return jnp.where(i < m[0], j, nn - 1)

    wspec = pl.BlockSpec((None, None, k, tn), lambda i, j, m: (layer, m[1 + live(i, m)], 0, col(i, j, m)))
    grid_spec = pltpu.PrefetchScalarGridSpec(
        num_scalar_prefetch=1, grid=(t // tm, nn),
        in_specs=[pl.BlockSpec((tm, k), lambda i, j, m: (live(i, m), 0)), wspec, wspec],
        out_specs=pl.BlockSpec((tm, tn), lambda i, j, m: (i, j)))
    return pl.pallas_call(
        _swiglu_up_kernel, grid_spec=grid_spec, out_shape=jax.ShapeDtypeStruct((t, n), BF16),
        compiler_params=_params("arbitrary", "arbitrary"), name=name,
    )(meta, x, wg, wu)


def _moe_down_kernel(meta_ref, h_ref, w_ref, o_ref, acc_ref, *, nk):
    k = pl.program_id(1)
    live = pl.program_id(0) < meta_ref[0]

    if nk == 1:
        @pl.when(live)
        def _():
            o_ref[...] = _dot(h_ref[...], w_ref[...])
    else:
        @pl.when(live & (k == 0))
        def _():
            acc_ref[...] = _dot(h_ref[...], w_ref[...])

        @pl.when(live & (k > 0) & (k < nk - 1))
        def _():
            acc_ref[...] += _dot(h_ref[...], w_ref[...])

        @pl.when(live & (k == nk - 1))
        def _():
            o_ref[...] = acc_ref[...] + _dot(h_ref[...], w_ref[...])

    @pl.when(jnp.logical_not(live) & (k == nk - 1))
    def _():
        o_ref[...] = jnp.zeros_like(o_ref)


def _moe_down_call(meta, h, wd, layer, tm):
    t, k = h.shape
    d = wd.shape[3]
    tk = _k_tile(k, 2048)
    nk = k // tk

    def live(i, m):
        return jnp.minimum(i, m[0] - 1)

    def kk(i, k_, m):
        return jnp.where(i < m[0], k_, nk - 1)

    grid_spec = pltpu.PrefetchScalarGridSpec(
        num_scalar_prefetch=1, grid=(t // tm, nk),
        in_specs=[pl.BlockSpec((tm, tk), lambda i, k_, m: (live(i, m), kk(i, k_, m))),
                  pl.BlockSpec((None, None, tk, d), lambda i, k_, m: (layer, m[1 + live(i, m)], kk(i, k_, m), 0))],
        out_specs=pl.BlockSpec((tm, d), lambda i, k_, m: (i, 0)),
        scratch_shapes=[pltpu.VMEM((tm, d), F32)])
    return pl.pallas_call(
        functools.partial(_moe_down_kernel, nk=nk), grid_spec=grid_spec,
        out_shape=jax.ShapeDtypeStruct((t, d), F32),
        compiler_params=_params("arbitrary", "arbitrary"), name="moe_down",
    )(meta, h, wd)


def _router_kernel(x_ref, wr_ref, e_ref, w_ref, r_ref, cnt_ref, carry_ref, *, n_exp, tm):
    @pl.when(pl.program_id(0) == 0)
    def _():
        carry_ref[...] = jnp.zeros_like(carry_ref)

    logits = _dot_split(x_ref[...], wr_ref[...])
    lane = lax.broadcasted_iota(I32, logits.shape, 1)
    lg = jnp.where(lane < n_exp, logits, -jnp.inf)
    m1 = jnp.max(lg, axis=-1, keepdims=True)
    i1 = jnp.min(jnp.where(lg == m1, lane, LANE), axis=-1, keepdims=True)
    lg2 = jnp.where(lane == i1, -jnp.inf, lg)
    m2 = jnp.max(lg2, axis=-1, keepdims=True)
    i2 = jnp.min(jnp.where(lg2 == m2, lane, LANE), axis=-1, keepdims=True)
    ex = jnp.exp(m2 - m1)
    w1 = 1.0 / (1.0 + ex)
    w2 = ex / (1.0 + ex)

    oh1 = (lane == i1).astype(F32)
    oh2 = (lane == i2).astype(F32)
    both = oh1 + oh2
    ri = lax.broadcasted_iota(I32, (tm, tm), 0)
    ci = lax.broadcasted_iota(I32, (tm, tm), 1)
    before = _dot((ri > ci).astype(BF16), both.astype(BF16)) + carry_ref[0:1, :]
    r1 = jnp.sum(before * oh1, axis=-1, keepdims=True)
    r2 = jnp.sum(before * oh2, axis=-1, keepdims=True)
    carry_ref[...] = carry_ref[...] + jnp.sum(both, axis=0, keepdims=True)

    e_ref[...] = jnp.where(lane == 0, i1, jnp.where(lane == 1, i2, 0))
    w_ref[...] = jnp.where(lane == 0, w1, jnp.where(lane == 1, w2, 0.0))
    r_ref[...] = jnp.where(lane == 0, r1, jnp.where(lane == 1, r2, 0.0)).astype(I32)
    cnt_ref[...] = carry_ref[...].astype(I32)


def _router_call(x, w_router):
    t, d = x.shape
    n_exp = w_router.shape[1]
    wr = jnp.zeros((d, LANE), F32).at[:, :n_exp].set(w_router)
    tm = _pick(t, 512)
    row = pl.BlockSpec((tm, LANE), lambda i: (i, 0))
    kern = functools.partial(_router_kernel, n_exp=n_exp, tm=tm)
    return pl.pallas_call(
        kern, grid=(t // tm,),
        in_specs=[pl.BlockSpec((tm, d), lambda i: (i, 0)), pl.BlockSpec((d, LANE), lambda i: (0, 0))],
        out_specs=[row, row, row, pl.BlockSpec((SUBLANE, LANE), lambda i: (0, 0))],
        out_shape=[jax.ShapeDtypeStruct((t, LANE), I32), jax.ShapeDtypeStruct((t, LANE), F32),
                   jax.ShapeDtypeStruct((t, LANE), I32), jax.ShapeDtypeStruct((SUBLANE, LANE), I32)],
        scratch_shapes=[pltpu.VMEM((SUBLANE, LANE), F32)],
        compiler_params=_params("arbitrary"), name="router",
    )(x, wr)


_DMA_UNROLL = 8
_DMA_PRIORITIES = 2


def _row_copy(src_hbm, dst_vmem, src_row, dst_row, sem):
    return pltpu.make_async_copy(src_hbm.at[pl.ds(src_row, 1)], dst_vmem.at[pl.ds(dst_row, 1)], sem)


def _gather_kernel(tok_ref, x_hbm, o_ref, sem, *, rows):
    def start(pair, carry):
        for prio in range(_DMA_PRIORITIES):
            r = pair * _DMA_PRIORITIES + prio
            _row_copy(x_hbm, o_ref, tok_ref[0, r], r, sem).start(priority=prio)
        return carry

    lax.fori_loop(0, rows // _DMA_PRIORITIES, start, 0, unroll=_DMA_UNROLL // _DMA_PRIORITIES)
    pltpu.make_async_copy(x_hbm.at[pl.ds(0, rows)], o_ref, sem).wait()


def _gather_call(x, tok, rows):
    cap = tok.shape[0]
    d = x.shape[1]
    tok3 = tok.reshape(cap // rows, 1, rows)
    return pl.pallas_call(
        functools.partial(_gather_kernel, rows=rows), grid=(cap // rows,),
        in_specs=[pl.BlockSpec((None, 1, rows), lambda i: (i, 0, 0), memory_space=pltpu.SMEM),
                  pl.BlockSpec(memory_space=pl.ANY)],
        out_specs=pl.BlockSpec((rows, d), lambda i: (i, 0)),
        out_shape=jax.ShapeDtypeStruct((cap, d), x.dtype),
        scratch_shapes=[pltpu.SemaphoreType.DMA(())],
        compiler_params=_params("arbitrary"), name="moe_gather",
    )(tok3, x)


def _combine_ln_kernel(d0_ref, d1_ref, o_hbm, w_ref, x_ref, g_ref, b_ref, o32_ref, o16_ref, buf0, buf1, sem,
                       *, alpha, rows):
    def start(r, carry):
        _row_copy(o_hbm, buf0, d0_ref[0, r], r, sem).start(priority=0)
        _row_copy(o_hbm, buf1, d1_ref[0, r], r, sem).start(priority=1)
        return carry

    lax.fori_loop(0, rows, start, 0, unroll=_DMA_UNROLL)
    pltpu.make_async_copy(o_hbm.at[pl.ds(0, rows)], buf0, sem).wait()
    pltpu.make_async_copy(o_hbm.at[pl.ds(0, rows)], buf1, sem).wait()
    wts = w_ref[...]
    ffn = wts[:, 0:1] * buf0[...] + wts[:, 1:2] * buf1[...]
    y = _ln_rows(alpha * x_ref[...] + ffn, g_ref[...], b_ref[...])
    o32_ref[...] = y
    o16_ref[...] = y.astype(BF16)


def _combine_ln_call(dest, o_sorted, wts, x, g, b, alpha):
    t, d = x.shape
    rows = _pick(t, 512)
    d0 = dest[:, 0].reshape(t // rows, 1, rows)
    d1 = dest[:, 1].reshape(t // rows, 1, rows)
    idx = pl.BlockSpec((None, 1, rows), lambda i: (i, 0, 0), memory_space=pltpu.SMEM)
    row = pl.BlockSpec((rows, d), lambda i: (i, 0))
    vec = pl.BlockSpec((1, d), lambda i: (0, 0))
    kern = functools.partial(_combine_ln_kernel, alpha=alpha, rows=rows)
    return pl.pallas_call(
        kern, grid=(t // rows,),
        in_specs=[idx, idx, pl.BlockSpec(memory_space=pl.ANY), pl.BlockSpec((rows, LANE), lambda i: (i, 0)),
                  row, vec, vec],
        out_specs=[row, row],
        out_shape=[jax.ShapeDtypeStruct((t, d), F32), jax.ShapeDtypeStruct((t, d), BF16)],
        scratch_shapes=[pltpu.VMEM((rows, d), F32), pltpu.VMEM((rows, d), F32), pltpu.SemaphoreType.DMA(())],
        compiler_params=_params("arbitrary"), name="moe_combine_ln",
    )(d0, d1, o_sorted, wts, x, g.reshape(1, d), b.reshape(1, d))


def _moe_layer(xf, w_router, wg, wu, wd, layer, ln_g, ln_b, alpha):
    t, d = xf.shape
    n_exp = wg.shape[1]
    e_out, w_out, r_out, cnt = _router_call(xf, w_router)
    e = e_out[:, :MOE_TOP_K]
    counts = cnt[0, :n_exp]

    tm = 512 if t * MOE_TOP_K >= 8 * 512 else 128
    n_tiles = -(-(t * MOE_TOP_K) // tm) + n_exp
    cap = n_tiles * tm
    padded = (counts + tm - 1) // tm * tm
    pad_end = jnp.cumsum(padded)
    pad_start = pad_end - padded
    dest = (pad_start[e] + r_out[:, :MOE_TOP_K]).astype(I32)
    tok = jnp.repeat(jnp.arange(t, dtype=I32), MOE_TOP_K)
    slot_tok = jnp.zeros((cap,), I32).at[dest.reshape(-1)].set(tok)
    tile_expert = jnp.minimum(jnp.searchsorted(pad_end, jnp.arange(n_tiles, dtype=I32) * tm, side="right"),
                              n_exp - 1)
    meta = jnp.concatenate([(pad_end[-1:] // tm), tile_expert]).astype(I32)

    xs = _gather_call(xf, slot_tok, _pick(cap, 1024))
    h = _swiglu_up_call(meta, xs, wg, wu, layer, tm, "moe_up")
    o_sorted = _moe_down_call(meta, h, wd, layer, tm)
    return _combine_ln_call(dest, o_sorted, w_out, xf, ln_g, ln_b, alpha)


def _trunk(x, conv_bufs, ssm_states, pr, want_v):
    bsz, seq, d = x.shape
    t = bsz * seq
    depth = pr["w_zx"].shape[0]
    alpha = float((2 * depth) ** 0.25)
    d_inner = pr["w_br_a"].shape[1]
    cdim = pr["conv_w"].shape[2]
    gd = pr["w_br_b"].shape[1]
    kc = pr["conv_w"].shape[1]
    off_xbc = d_inner
    off_gate = 2 * gd
    q_ssd = _pick(seq, 128)
    n_groups = (cdim - d_inner) // (2 * ssm_states.shape[-1])
    gs_ssd = next(c for c in (4, 2, 1) if n_groups % c == 0)
    q_gmlp = min(pr["w_s"].shape[2], seq)

    xf, xb = _ln_call(x.reshape(t, d), pr["ln_in_g"], pr["ln_in_b"])
    convs, ssms, vs = [], [], []
    w_main, w_ug = pr["w_zx"], pr["w_ug"]
    lane_aligned = q_ssd % LANE == 0
    for i in range(depth):
        sz = _proj_call(xb, w_main, i, 0, d_inner, "silu", BF16, "in_proj_z")
        uv = _proj_call(xb, w_ug, i, 0, 2 * gd, "gelu", F32, "in_proj_uv")
        gates = _proj_call(xb, w_ug, i, off_gate, 2 * d, "none", BF16, "in_proj_gate", bias=pr["b_gate"][i])
        dt, acum = _dt_call(xf, pr["w_dt"][i], pr["dt_bias"][i], pr["a_log_pad"][i], q_ssd)
        if lane_aligned:
            xbc, tails = _proj_conv_call(xb, w_main, i, off_xbc, pr["conv_w"][i], pr["conv_b"][i], conv_bufs[i], seq)
            ya, ssm_i = _ssd_cm_call(xbc, sz, dt, acum, ssm_states[i], pr["d_skip"][i], pr["ssd_norm_g"][i],
                                     bsz=bsz, seq=seq, q=q_ssd, gs=gs_ssd)
            new_rows = tails.reshape(bsz, -1, _CONV_PAD, cdim)[:, -1, _CONV_PAD - (kc - 1):]
        else:
            xbc = _proj_call(xb, w_main, i, off_xbc, cdim, "none", F32, "in_proj_xbc")
            ya, ssm_i = _ssd_call(xbc, sz, dt, conv_bufs[i], ssm_states[i], pr["conv_w"][i], pr["conv_b"][i],
                                  pr["a_log"][i], pr["d_skip"][i], pr["ssd_norm_g"][i], bsz=bsz, seq=seq, q=q_ssd)
            new_rows = xbc.reshape(bsz, seq, cdim)[:, max(seq - (kc - 1), 0):]
        conv_i = jnp.concatenate([conv_bufs[i], new_rows], axis=1)[:, -(kc - 1):]
        yb, *vn = _gmlp_call(uv, pr["gmlp_ln_g"][i], pr["gmlp_ln_b"][i], pr["w_s"][i], pr["b_s"][i],
                             bsz=bsz, seq=seq, q=q_gmlp, want_v=want_v)
        merged = _merge_call(ya, yb, gates, pr["w_br_a"], pr["w_br_b"], i)
        xf, xb = _mm_res_ln_call(merged, pr["w_o"], i, xf, pr["ln1_g"][i], pr["ln1_b"][i], alpha, "out_proj_ln")
        j = i // 2
        if i % 2 == 0:
            tm = _pick(t, 1024)
            meta = jnp.concatenate([jnp.full((1,), t // tm, I32), jnp.zeros((t // tm,), I32)])
            h = _swiglu_up_call(meta, xb, pr["w_ff_gate"], pr["w_ff_up"], j, tm, "ffn_up")
            xf, xb = _mm_res_ln_call(h, pr["w_ff_down"], j, xf, pr["ln2_g"][i], pr["ln2_b"][i], alpha,
                                     "ffn_down_ln")
        else:
            xf, xb = _moe_layer(xf, pr["w_router"][j], pr["w_moe_gate"], pr["w_moe_up"], pr["w_moe_down"], j,
                                pr["ln2_g"][i], pr["ln2_b"][i], alpha)
        convs.append(conv_i)
        ssms.append(ssm_i)
        vs.extend(v.reshape(bsz, seq, gd) for v in vn)
    return xf.reshape(bsz, seq, d), jnp.stack(convs), jnp.stack(ssms), (jnp.stack(vs) if want_v else None)


def kernel(x_prompt, x_sample, cache_conv, state_ssm, ln_in_g, ln_in_b, w_in, conv_w, conv_b, dt_bias, a_log,
           d_skip, ssd_norm_g, gmlp_ln_g, gmlp_ln_b, w_s, b_s, b_gate, w_br_a, w_br_b, w_o, ln1_g, ln1_b,
           w_ff_gate, w_ff_up, w_ff_down, w_router, w_moe_gate, w_moe_up, w_moe_down, ln2_g, ln2_b):
    depth, d_model, _ = w_in.shape
    d_inner = w_br_a.shape[1]
    cdim = conv_w.shape[2]
    heads = a_log.shape[1]
    o_dt = d_inner + cdim
    w_in16 = w_in.astype(BF16)
    w_ug = w_in16[:, :, o_dt + heads:]
    w_dt = jnp.zeros((depth, d_model, LANE), F32).at[:, :, :heads].set(w_in[:, :, o_dt:o_dt + heads])
    dt_b = jnp.zeros((depth, 1, LANE), F32).at[:, 0, :heads].set(dt_bias)
    al_pad = jnp.zeros((depth, 1, LANE), F32).at[:, 0, :heads].set(a_log)
    pr = dict(a_log_pad=al_pad,
        ln_in_g=ln_in_g, ln_in_b=ln_in_b, w_zx=w_in16, w_ug=w_ug, w_dt=w_dt, dt_bias=dt_b, conv_w=conv_w, conv_b=conv_b,
        a_log=a_log, d_skip=d_skip, ssd_norm_g=ssd_norm_g, gmlp_ln_g=gmlp_ln_g, gmlp_ln_b=gmlp_ln_b, w_s=w_s,
        b_s=b_s, b_gate=b_gate, w_br_a=w_br_a.astype(BF16), w_br_b=w_br_b.astype(BF16), w_o=w_o.astype(BF16),
        ln1_g=ln1_g, ln1_b=ln1_b, w_ff_gate=w_ff_gate.astype(BF16)[:, None], w_ff_up=w_ff_up.astype(BF16)[:, None],
        w_ff_down=w_ff_down.astype(BF16), w_router=w_router, w_moe_gate=w_moe_gate.astype(BF16),
        w_moe_up=w_moe_up.astype(BF16), w_moe_down=w_moe_down.astype(BF16), ln2_g=ln2_g, ln2_b=ln2_b)
    bp = x_prompt.shape[0]
    zero_conv = jnp.zeros((depth, bp) + cache_conv.shape[2:], x_prompt.dtype)
    zero_ssm = jnp.zeros((depth, bp) + state_ssm.shape[2:], state_ssm.dtype)
    y_prompt, prompt_conv, prompt_ssm, _ = _trunk(x_prompt, zero_conv, zero_ssm, pr, want_v=False)
    y_sample, sample_conv, sample_ssm, sample_v = _trunk(x_sample, cache_conv, state_ssm, pr, want_v=True)
    return (y_prompt, y_sample, prompt_conv, prompt_ssm, sample_conv, sample_ssm, sample_v)
```

```python
import functools

import jax
import jax.numpy as jnp
from jax import lax
from jax.experimental import pallas as pl
from jax.experimental.pallas import tpu as pltpu

F32 = jnp.float32
BF16 = jnp.bfloat16
I32 = jnp.int32
HIGHEST = lax.Precision.HIGHEST

LN_EPS = 1e-5
LANE = 128
SUBLANE = 8
V7X_VMEM_BYTES = 64 * 1024 * 1024
VMEM_LIMIT_BYTES = V7X_VMEM_BYTES - 8 * 1024 * 1024
MOE_TOP_K = 2
_NEG_BIG = -1e30
_TILE_CANDIDATES = (1024, 512, 256, 128, 64, 32, 16, 8)


def _pick(n, cap):
    for c in _TILE_CANDIDATES:
        if c <= cap and n % c == 0:
            return c
    raise ValueError(f"no tile for {n}")


def _params(*sem):
    return pltpu.CompilerParams(dimension_semantics=sem, vmem_limit_bytes=VMEM_LIMIT_BYTES)


def _ln_rows(x, g, b):
    mu = jnp.mean(x, axis=-1, keepdims=True)
    xc = x - mu
    var = jnp.mean(xc * xc, axis=-1, keepdims=True)
    return xc * lax.rsqrt(var + LN_EPS) * g + b


def _sigmoid(x):
    return 0.5 * jnp.tanh(0.5 * x) + 0.5


def _silu(x):
    return x * _sigmoid(x)


def _gelu(x):
    return 0.5 * x * (1.0 + lax.erf(x * (2.0 ** -0.5)))


def _dot(a, b):
    return jnp.dot(a, b, preferred_element_type=F32)


def _dot_exact(a, b):
    return jnp.dot(a, b, precision=HIGHEST, preferred_element_type=F32)


def _ln_kernel(x_ref, g_ref, b_ref, o32_ref, o16_ref):
    y = _ln_rows(x_ref[...], g_ref[...], b_ref[...])
    o32_ref[...] = y
    o16_ref[...] = y.astype(BF16)


def _ln_call(x, g, b):
    t, d = x.shape
    tm = _pick(t, 512)
    row = pl.BlockSpec((tm, d), lambda i: (i, 0))
    vec = pl.BlockSpec((1, d), lambda i: (0, 0))
    return pl.pallas_call(
        _ln_kernel, grid=(t // tm,), in_specs=[row, vec, vec], out_specs=[row, row],
        out_shape=[jax.ShapeDtypeStruct((t, d), F32), jax.ShapeDtypeStruct((t, d), BF16)],
        compiler_params=_params("parallel"), name="ln_in",
    )(x, g.reshape(1, d), b.reshape(1, d))


_CONV_PAD = SUBLANE

_ACTIVATIONS = {"none": lambda r: r, "silu": _silu, "gelu": _gelu}


def _proj_kernel(x_ref, w_ref, o_ref, *, act):
    o_ref[...] = _ACTIVATIONS[act](_dot(x_ref[...], w_ref[...])).astype(o_ref.dtype)


def _proj_gate_kernel(x_ref, w_ref, b_ref, o_ref):
    o_ref[...] = _sigmoid(_dot(x_ref[...], w_ref[...]) + b_ref[...]).astype(o_ref.dtype)


def _proj_conv_kernel(x_ref, w_ref, cw_ref, cc_ref, cbuf_ref, o_ref, tail_ref, halo_ref, *, kc, tiles_per_seq):
    i, j = pl.program_id(0), pl.program_id(1)
    tm = x_ref.shape[0]
    r = _dot(x_ref[...], w_ref[...])
    before = jnp.where(lax.rem(i, tiles_per_seq) == 0, cbuf_ref[...], halo_ref[j])
    nb = tm // _CONV_PAD
    blocks = r.reshape(nb, _CONV_PAD, r.shape[1])
    prev_blocks = jnp.concatenate([before[None], blocks[:nb - 1]], axis=0)
    sub = lax.broadcasted_iota(I32, blocks.shape, 1)
    acc = cc_ref[...] + r * cw_ref[kc - 1:kc, :]
    for s in range(1, kc):
        shifted = jnp.where(sub < s, pltpu.roll(prev_blocks, s, 1), pltpu.roll(blocks, s, 1))
        acc = acc + shifted.reshape(r.shape) * cw_ref[kc - 1 - s:kc - s, :]
    o_ref[...] = _silu(acc)
    last_rows = r[tm - _CONV_PAD:tm, :]
    tail_ref[...] = last_rows
    halo_ref[j] = last_rows


def _region_tiles(t, n, col0):
    tm = _pick(t, 1024)
    tn = next(c for c in _TILE_CANDIDATES if n % c == 0 and col0 % c == 0)
    return tm, tn


def _proj_call(x, w, layer, col0, n, act, out_dtype, name, bias=None):
    t, k = x.shape
    tm, tn = _region_tiles(t, n, col0)
    c0 = col0 // tn
    in_specs = [pl.BlockSpec((tm, k), lambda i, j: (i, 0)), pl.BlockSpec((None, k, tn), lambda i, j: (layer, 0, c0 + j))]
    args = [x, w]
    if bias is None:
        kern = functools.partial(_proj_kernel, act=act)
    else:
        kern = _proj_gate_kernel
        in_specs.append(pl.BlockSpec((1, tn), lambda i, j: (0, j)))
        args.append(bias.reshape(1, n))
    return pl.pallas_call(
        kern, grid=(t // tm, n // tn), in_specs=in_specs,
        out_specs=pl.BlockSpec((tm, tn), lambda i, j: (i, j)),
        out_shape=jax.ShapeDtypeStruct((t, n), out_dtype),
        compiler_params=_params("parallel", "parallel"), name=name,
    )(*args)


def _proj_conv_call(x, w, layer, col0, conv_w, conv_b, conv_buf, seq):
    t, k = x.shape
    kc, n = conv_w.shape
    tm, tn = _pick(seq, 1024), _region_tiles(t, n, col0)[1]
    assert t % tm == 0 and kc - 1 <= _CONV_PAD <= tm
    tiles_per_seq = seq // tm
    c0 = col0 // tn
    cbuf = jnp.pad(conv_buf, ((0, 0), (_CONV_PAD - (kc - 1), 0), (0, 0)))
    kern = functools.partial(_proj_conv_kernel, kc=kc, tiles_per_seq=tiles_per_seq)
    return pl.pallas_call(
        kern, grid=(t // tm, n // tn),
        in_specs=[pl.BlockSpec((tm, k), lambda i, j: (i, 0)),
                  pl.BlockSpec((None, k, tn), lambda i, j: (layer, 0, c0 + j)),
                  pl.BlockSpec((kc, tn), lambda i, j: (0, j)),
                  pl.BlockSpec((1, tn), lambda i, j: (0, j)),
                  pl.BlockSpec((None, _CONV_PAD, tn), lambda i, j: (i // tiles_per_seq, 0, j))],
        out_specs=[pl.BlockSpec((tm, tn), lambda i, j: (i, j)),
                   pl.BlockSpec((None, _CONV_PAD, tn), lambda i, j: (i, 0, j))],
        out_shape=[jax.ShapeDtypeStruct((t, n), F32), jax.ShapeDtypeStruct((t // tm, _CONV_PAD, n), F32)],
        scratch_shapes=[pltpu.VMEM((n // tn, _CONV_PAD, tn), F32)],
        compiler_params=_params("arbitrary", "arbitrary"), name="in_proj_conv",
    )(x, w, conv_w, conv_b.reshape(1, n), cbuf)


def _dt_kernel(x_ref, w_ref, b_ref, al_ref, dt_ref, ac_ref, *, q):
    raw = _dot(x_ref[...], w_ref[...]) + b_ref[...]
    dt = jnp.maximum(raw, 0.0) + jnp.log1p(jnp.exp(-jnp.abs(raw)))
    dt_ref[...] = dt
    a = dt * (-jnp.exp(al_ref[...]))
    ri = lax.broadcasted_iota(I32, (q, q), 0)
    ci = lax.broadcasted_iota(I32, (q, q), 1)
    tril = (ri >= ci).astype(F32)
    for c in range(a.shape[0] // q):
        ac_ref[c * q:(c + 1) * q, :] = _dot_exact(tril, a[c * q:(c + 1) * q, :])


def _dt_call(x, w, bias, a_log, q):
    t, k = x.shape
    n = w.shape[1]
    tm = max(_pick(t, 512), q)
    assert tm % q == 0 and t % tm == 0
    row = pl.BlockSpec((tm, n), lambda i: (i, 0))
    vec = pl.BlockSpec((1, n), lambda i: (0, 0))
    return pl.pallas_call(
        functools.partial(_dt_kernel, q=q), grid=(t // tm,),
        in_specs=[pl.BlockSpec((tm, k), lambda i: (i, 0)), pl.BlockSpec((k, n), lambda i: (0, 0)), vec, vec],
        out_specs=[row, row],
        out_shape=[jax.ShapeDtypeStruct((t, n), F32), jax.ShapeDtypeStruct((t, n), F32)],
        compiler_params=_params("parallel"), name="dt_proj",
    )(x, w, bias, a_log)


def _ssd_kernel(xs_ref, bm_ref, cm_ref, sz_ref, dt_ref, dtt_ref, cbx_ref, cbb_ref, cbc_ref, s0_ref,
                cwx_ref, cwb_ref, cwc_ref, ccx_ref, ccb_ref, ccc_ref, alr_ref, alc_ref, dsk_ref, ng_ref,
                y_ref, sout_ref, xpad_ref, st_ref, *, q, hpg, p, n, kc, nc):
    c = pl.program_id(2)
    w = hpg * p
    tail = kc - 1
    t0 = _CONV_PAD - tail

    @pl.when(c == 0)
    def _():
        xpad_ref[t0:_CONV_PAD, 0:w] = cbx_ref[...]
        xpad_ref[t0:_CONV_PAD, w:w + n] = cbb_ref[...]
        xpad_ref[t0:_CONV_PAD, w + n:w + 2 * n] = cbc_ref[...]
        st_ref[...] = s0_ref[...].reshape(w, n).T

    xpad_ref[_CONV_PAD:_CONV_PAD + q, 0:w] = xs_ref[...]
    xpad_ref[_CONV_PAD:_CONV_PAD + q, w:w + n] = bm_ref[...]
    xpad_ref[_CONV_PAD:_CONV_PAD + q, w + n:w + 2 * n] = cm_ref[...]

    def conv(lo, hi, cw_ref, cc_ref):
        acc = cc_ref[...]
        for k in range(kc):
            acc = acc + xpad_ref[t0 + k:t0 + k + q, lo:hi] * cw_ref[k:k + 1, :]
        return _silu(acc)

    xs = conv(0, w, cwx_ref, ccx_ref)
    bm = conv(w, w + n, cwb_ref, ccb_ref)
    cm = conv(w + n, w + 2 * n, cwc_ref, ccc_ref)
    xpad_ref[t0:_CONV_PAD, :] = xpad_ref[t0 + q:_CONV_PAD + q, :]

    a = dt_ref[...] * (-jnp.exp(alr_ref[...]))
    a_t = dtt_ref[...] * (-jnp.exp(alc_ref[...]))
    ri = lax.broadcasted_iota(I32, (q, q), 0)
    ci = lax.broadcasted_iota(I32, (q, q), 1)
    tril = ri >= ci
    acol = _dot_exact(tril.astype(F32), a)
    arow = _dot_exact(a_t, (ri <= ci).astype(F32))

    hrow = lax.broadcasted_iota(I32, (hpg, w), 0)
    hlane = lax.broadcasted_iota(I32, (hpg, w), 1)
    expand = ((hlane >= hrow * p) & (hlane < (hrow + 1) * p)).astype(F32)
    xdt = xs * _dot_exact(dt_ref[...], expand)
    eacol = _dot_exact(jnp.exp(acol), expand)
    to_end = _dot_exact(jnp.exp(acol[q - 1:q, :] - acol), expand)

    bm16 = bm.astype(BF16)
    cm16 = cm.astype(BF16)
    cb = lax.dot_general(cm16, bm16, (((1,), (1,)), ((), ())), preferred_element_type=F32)
    st = st_ref[...]
    y = _dot(cm16, st.astype(BF16)) * eacol
    lane = lax.broadcasted_iota(I32, (q, w), 1)
    for h in range(hpg):
        seg = acol[:, h:h + 1] - arow[h:h + 1, :]
        m = (cb * jnp.exp(jnp.where(tril, seg, _NEG_BIG))).astype(BF16)
        head = (lane >= h * p) & (lane < (h + 1) * p)
        y = y + _dot(m, jnp.where(head, xdt, 0.0).astype(BF16))
    y = y + dsk_ref[...] * xs
    hz = y * sz_ref[...]
    y_ref[...] = (hz * lax.rsqrt(jnp.mean(hz * hz, axis=-1, keepdims=True) + LN_EPS) * ng_ref[...]).astype(BF16)

    upd = lax.dot_general(bm16, (xdt * to_end).astype(BF16), (((0,), (0,)), ((), ())),
                          preferred_element_type=F32)
    st_ref[...] = st * eacol[q - 1:q, :] + upd

    @pl.when(c == nc - 1)
    def _():
        sout_ref[...] = st_ref[...].T.reshape(hpg, p, n)


def _ssd_call(xbc, sz, dt, conv_buf, state0, conv_w, conv_b, a_log, d_skip, norm_g, *, bsz, seq, q):
    heads, p, n = state0.shape[1:]
    kc, cdim = conv_w.shape
    d_inner = sz.shape[1]
    groups = (cdim - d_inner) // (2 * n)
    hpg = heads // groups
    w = hpg * p
    nc = seq // q
    t = bsz * seq
    assert w * groups == d_inner and seq % q == 0 and q >= kc - 1 and d_inner % n == 0

    dt4 = dt[:, :heads].reshape(bsz, seq, groups, hpg).transpose(0, 2, 1, 3)
    dtt4 = dt4.transpose(0, 1, 3, 2)
    alr = a_log.reshape(groups, 1, hpg)
    alc = a_log.reshape(groups, hpg, 1)
    dsk = jnp.repeat(d_skip, p).reshape(1, d_inner)
    ng = norm_g.reshape(1, d_inner)
    ccb = conv_b.reshape(1, cdim)

    bb, cb_ = d_inner // n, d_inner // n + groups
    row = lambda b, g, c: b * nc + c
    in_specs = [
        pl.BlockSpec((q, w), lambda b, g, c: (row(b, g, c), g)),
        pl.BlockSpec((q, n), lambda b, g, c: (row(b, g, c), bb + g)),
        pl.BlockSpec((q, n), lambda b, g, c: (row(b, g, c), cb_ + g)),
        pl.BlockSpec((q, w), lambda b, g, c: (row(b, g, c), g)),
        pl.BlockSpec((None, None, q, hpg), lambda b, g, c: (b, g, c, 0)),
        pl.BlockSpec((None, None, hpg, q), lambda b, g, c: (b, g, 0, c)),
        pl.BlockSpec((None, kc - 1, w), lambda b, g, c: (b, 0, g)),
        pl.BlockSpec((None, kc - 1, n), lambda b, g, c: (b, 0, d_inner // n + g)),
        pl.BlockSpec((None, kc - 1, n), lambda b, g, c: (b, 0, d_inner // n + groups + g)),
        pl.BlockSpec((None, hpg, p, n), lambda b, g, c: (b, g, 0, 0)),
        pl.BlockSpec((kc, w), lambda b, g, c: (0, g)),
        pl.BlockSpec((kc, n), lambda b, g, c: (0, d_inner // n + g)),
        pl.BlockSpec((kc, n), lambda b, g, c: (0, d_inner // n + groups + g)),
        pl.BlockSpec((1, w), lambda b, g, c: (0, g)),
        pl.BlockSpec((1, n), lambda b, g, c: (0, d_inner // n + g)),
        pl.BlockSpec((1, n), lambda b, g, c: (0, d_inner // n + groups + g)),
        pl.BlockSpec((None, 1, hpg), lambda b, g, c: (g, 0, 0)),
        pl.BlockSpec((None, hpg, 1), lambda b, g, c: (g, 0, 0)),
        pl.BlockSpec((1, w), lambda b, g, c: (0, g)),
        pl.BlockSpec((1, w), lambda b, g, c: (0, g)),
    ]
    out_specs = [
        pl.BlockSpec((q, w), lambda b, g, c: (row(b, g, c), g)),
        pl.BlockSpec((None, hpg, p, n), lambda b, g, c: (b, g, 0, 0)),
    ]
    kern = functools.partial(_ssd_kernel, q=q, hpg=hpg, p=p, n=n, kc=kc, nc=nc)
    return pl.pallas_call(
        kern, grid=(bsz, groups, nc), in_specs=in_specs, out_specs=out_specs,
        out_shape=[jax.ShapeDtypeStruct((t, d_inner), BF16), jax.ShapeDtypeStruct(state0.shape, F32)],
        scratch_shapes=[pltpu.VMEM((_CONV_PAD + q, w + 2 * n), F32), pltpu.VMEM((n, w), F32)],
        compiler_params=_params("parallel", "parallel", "arbitrary"), name="ssd",
    )(xbc, xbc, xbc, sz, dt4, dtt4, conv_buf, conv_buf, conv_buf, state0,
      conv_w, conv_w, conv_w, ccb, ccb, ccb, alr, alc, dsk, ng)


def _ssd_cm_kernel(xs_ref, bm_ref, cm_ref, sz_ref, dtt_ref, act_ref, ac_ref, s0_ref, dsk_ref, ng_ref,
                   y_ref, sout_ref, st_ref, *, q, gs, hpg, p, n, nc):
    c = pl.program_id(2)
    w = hpg * p

    @pl.when(c == 0)
    def _():
        st_ref[...] = s0_ref[...].reshape(gs * w, n)

    def rows(v):
        return jnp.concatenate([jnp.broadcast_to(v[h:h + 1, :], (p, v.shape[1])) for h in range(hpg)], axis=0)

    si = lax.broadcasted_iota(I32, (q, q), 0)
    li = lax.broadcasted_iota(I32, (q, q), 1)
    keep = si <= li
    nt = (((1,), (1,)), ((), ()))
    for g in range(gs):
        xs = xs_ref[:, g * w:(g + 1) * w]
        bm16 = bm_ref[:, g * n:(g + 1) * n].astype(BF16)
        cm16 = cm_ref[:, g * n:(g + 1) * n].astype(BF16)
        arow = act_ref[g]
        acol = ac_ref[g]
        ear = jnp.exp(arow)
        te = jnp.exp(arow[:, q - 1:q] - arow)
        xdt_t = xs.T * rows(dtt_ref[g])
        xdt16 = xdt_t.astype(BF16)
        cb_t = lax.dot_general(bm16, cm16, nt, preferred_element_type=F32)
        ys = []
        for h in range(hpg):
            seg = arow[h:h + 1, :] - acol[:, h:h + 1]
            m_t = (cb_t * jnp.exp(jnp.where(keep, seg, _NEG_BIG))).astype(BF16)
            ys.append(_dot(xdt16[h * p:(h + 1) * p, :], m_t))
        st = st_ref[g * w:(g + 1) * w, :]
        y_t = jnp.concatenate(ys, axis=0)
        y_t = y_t + lax.dot_general(st.astype(BF16), cm16, nt, preferred_element_type=F32) * rows(ear)
        y = y_t.T + dsk_ref[:, g * w:(g + 1) * w] * xs
        hz = y * sz_ref[:, g * w:(g + 1) * w]
        y_ref[:, g * w:(g + 1) * w] = (hz * lax.rsqrt(jnp.mean(hz * hz, axis=-1, keepdims=True) + LN_EPS)
                                       * ng_ref[:, g * w:(g + 1) * w]).astype(BF16)
        upd = _dot((xdt_t * rows(te)).astype(BF16), bm16)
        st_ref[g * w:(g + 1) * w, :] = st * rows(ear[:, q - 1:q]) + upd

    @pl.when(c == nc - 1)
    def _():
        sout_ref[...] = st_ref[...].reshape(gs * hpg, p, n)


def _ssd_cm_call(xbc, sz, dt, acum, state0, d_skip, norm_g, *, bsz, seq, q, gs):
    heads, p, n = state0.shape[1:]
    cdim = xbc.shape[1]
    d_inner = sz.shape[1]
    groups = (cdim - d_inner) // (2 * n)
    hpg = heads // groups
    w = hpg * p
    nc = seq // q
    t = bsz * seq
    gw, gn = gs * w, gs * n
    assert w * groups == d_inner and seq % q == 0 and q % LANE == 0 and groups % gs == 0
    assert d_inner % gn == 0 and (groups * n) % gn == 0

    def heads_major(v):
        return v[:, :heads].reshape(bsz, seq, groups, hpg).transpose(0, 2, 3, 1)

    dtt4 = heads_major(dt)
    act4 = heads_major(acum)
    ac4 = act4.transpose(0, 1, 3, 2)
    dsk = jnp.repeat(d_skip, p).reshape(1, d_inner)
    ng = norm_g.reshape(1, d_inner)

    bb, cb_ = d_inner // gn, (d_inner + groups * n) // gn
    row = lambda b, g, c: b * nc + c
    in_specs = [
        pl.BlockSpec((q, gw), lambda b, g, c: (row(b, g, c), g)),
        pl.BlockSpec((q, gn), lambda b, g, c: (row(b, g, c), bb + g)),
        pl.BlockSpec((q, gn), lambda b, g, c: (row(b, g, c), cb_ + g)),
        pl.BlockSpec((q, gw), lambda b, g, c: (row(b, g, c), g)),
        pl.BlockSpec((None, gs, hpg, q), lambda b, g, c: (b, g, 0, c)),
        pl.BlockSpec((None, gs, hpg, q), lambda b, g, c: (b, g, 0, c)),
        pl.BlockSpec((None, gs, q, hpg), lambda b, g, c: (b, g, c, 0)),
        pl.BlockSpec((None, gs * hpg, p, n), lambda b, g, c: (b, g, 0, 0)),
        pl.BlockSpec((1, gw), lambda b, g, c: (0, g)),
        pl.BlockSpec((1, gw), lambda b, g, c: (0, g)),
    ]
    out_specs = [
        pl.BlockSpec((q, gw), lambda b, g, c: (row(b, g, c), g)),
        pl.BlockSpec((None, gs * hpg, p, n), lambda b, g, c: (b, g, 0, 0)),
    ]
    kern = functools.partial(_ssd_cm_kernel, q=q, gs=gs, hpg=hpg, p=p, n=n, nc=nc)
    return pl.pallas_call(
        kern, grid=(bsz, groups // gs, nc), in_specs=in_specs, out_specs=out_specs,
        out_shape=[jax.ShapeDtypeStruct((t, d_inner), BF16), jax.ShapeDtypeStruct(state0.shape, F32)],
        scratch_shapes=[pltpu.VMEM((gs * w, n), F32)],
        compiler_params=_params("parallel", "parallel", "arbitrary"), name="ssd_cm",
    )(xbc, xbc, xbc, sz, dtt4, act4, ac4, state0, dsk, ng)


def _gmlp_kernel(u_ref, v_ref, lg_ref, lb_ref, ws_ref, bs_ref, yb_ref, *vn_refs, groups, q):
    u = u_ref[...]
    vn = _ln_rows(v_ref[...], lg_ref[...], lb_ref[...])
    for vn_ref in vn_refs:
        vn_ref[...] = vn
    vn16 = vn.astype(BF16)
    d = vn.shape[1] // groups
    ri = lax.broadcasted_iota(I32, (q, q), 0)
    ci = lax.broadcasted_iota(I32, (q, q), 1)
    for g in range(groups):
        wg = jnp.where(ri >= ci, ws_ref[g], 0.0).astype(BF16)
        s = _dot(wg, vn16[:, g * d:(g + 1) * d]) + bs_ref[g]
        yb_ref[:, g * d:(g + 1) * d] = (u[:, g * d:(g + 1) * d] * s).astype(BF16)


def _gmlp_call(uv, ln_g, ln_b, w_s, b_s, *, bsz, seq, q, want_v):
    groups = w_s.shape[0]
    t = bsz * seq
    gd = uv.shape[1] // 2
    assert seq % q == 0 and (gd // groups) % LANE == 0
    ws = w_s[:, :q, :q]
    bs = b_s[:, :q, None]
    row = pl.BlockSpec((q, gd), lambda i: (i, 0))
    kern = functools.partial(_gmlp_kernel, groups=groups, q=q)
    out_shape = [jax.ShapeDtypeStruct((t, gd), BF16)] + ([jax.ShapeDtypeStruct((t, gd), F32)] if want_v else [])
    return pl.pallas_call(
        kern, grid=(t // q,),
        in_specs=[pl.BlockSpec((q, gd), lambda i: (i, 0)), pl.BlockSpec((q, gd), lambda i: (i, 1)),
                  pl.BlockSpec((1, gd), lambda i: (0, 0)), pl.BlockSpec((1, gd), lambda i: (0, 0)),
                  pl.BlockSpec((groups, q, q), lambda i: (0, 0, 0)),
                  pl.BlockSpec((groups, q, 1), lambda i: (0, 0, 0))],
        out_specs=[row] * len(out_shape), out_shape=out_shape,
        compiler_params=_params("parallel"), name="gmlp",
    )(uv, uv, ln_g.reshape(1, gd), ln_b.reshape(1, gd), ws, bs)


def _merge_kernel(ya_ref, yb_ref, wa_ref, wb_ref, ga_ref, gb_ref, o_ref):
    a = _dot(ya_ref[...], wa_ref[...])
    b = _dot(yb_ref[...], wb_ref[...])
    o_ref[...] = (ga_ref[...] * a + gb_ref[...] * b).astype(o_ref.dtype)


def _merge_call(ya, yb, gates, w_a, w_b, layer):
    t, ka = ya.shape
    kb = yb.shape[1]
    d = w_a.shape[2]
    tm, tn = _pick(t, 1024), _pick(d, 512)
    nd = d // tn
    return pl.pallas_call(
        _merge_kernel, grid=(t // tm, nd),
        in_specs=[pl.BlockSpec((tm, ka), lambda i, j: (i, 0)), pl.BlockSpec((tm, kb), lambda i, j: (i, 0)),
                  pl.BlockSpec((None, ka, tn), lambda i, j: (layer, 0, j)),
                  pl.BlockSpec((None, kb, tn), lambda i, j: (layer, 0, j)),
                  pl.BlockSpec((tm, tn), lambda i, j: (i, j)), pl.BlockSpec((tm, tn), lambda i, j: (i, nd + j))],
        out_specs=pl.BlockSpec((tm, tn), lambda i, j: (i, j)),
        out_shape=jax.ShapeDtypeStruct((t, d), BF16),
        compiler_params=_params("parallel", "parallel"), name="merge",
    )(ya, yb, w_a, w_b, gates, gates)


def _mm_res_ln_kernel(a_ref, w_ref, x_ref, g_ref, b_ref, o32_ref, o16_ref, acc_ref, *, alpha, nk):
    k = pl.program_id(1)

    def finish(total):
        y = _ln_rows(alpha * x_ref[...] + total, g_ref[...], b_ref[...])
        o32_ref[...] = y
        o16_ref[...] = y.astype(BF16)

    if nk == 1:
        finish(_dot(a_ref[...], w_ref[...]))
        return

    @pl.when(k == 0)
    def _():
        acc_ref[...] = _dot(a_ref[...], w_ref[...])

    @pl.when((k > 0) & (k < nk - 1))
    def _():
        acc_ref[...] += _dot(a_ref[...], w_ref[...])

    @pl.when(k == nk - 1)
    def _():
        finish(acc_ref[...] + _dot(a_ref[...], w_ref[...]))


def _k_tile(k, cap):
    if k % LANE != 0:
        return k
    best = LANE
    for m in range(1, k // LANE + 1):
        tk = m * LANE
        if k % tk == 0 and tk <= cap:
            best = tk
    return best


def _mm_res_ln_call(a, w, layer, x, g, b, alpha, name):
    t, k = a.shape
    d = w.shape[2]
    tm = _pick(t, 512)
    tk = _k_tile(k, 2048)
    nk = k // tk
    row = pl.BlockSpec((tm, d), lambda i, kk: (i, 0))
    vec = pl.BlockSpec((1, d), lambda i, kk: (0, 0))
    kern = functools.partial(_mm_res_ln_kernel, alpha=alpha, nk=nk)
    return pl.pallas_call(
        kern, grid=(t // tm, nk),
        in_specs=[pl.BlockSpec((tm, tk), lambda i, kk: (i, kk)),
                  pl.BlockSpec((None, tk, d), lambda i, kk: (layer, kk, 0)), row, vec, vec],
        out_specs=[row, row],
        out_shape=[jax.ShapeDtypeStruct((t, d), F32), jax.ShapeDtypeStruct((t, d), BF16)],
        scratch_shapes=[pltpu.VMEM((tm, d), F32)],
        compiler_params=_params("parallel", "arbitrary"), name=name,
    )(a, w, x, g.reshape(1, d), b.reshape(1, d))


def _swiglu_up_kernel(meta_ref, x_ref, wg_ref, wu_ref, o_ref):
    live = pl.program_id(0) < meta_ref[0]

    @pl.when(live)
    def _():
        x = x_ref[...].astype(BF16)
        g = _dot(x, wg_ref[...])
        u = _dot(x, wu_ref[...])
        o_ref[...] = (_silu(g) * u).astype(o_ref.dtype)

    @pl.when(jnp.logical_not(live))
    def _():
        o_ref[...] = jnp.zeros_like(o_ref)


def _swiglu_up_call(meta, x, wg, wu, layer, tm, name):
    t, k = x.shape
    n = wg.shape[3]
    tn = _pick(n, 1024)
    nn = n // tn

    def live(i, m):
        return jnp.minimum(i, m[0] - 1)

    def col(i, j, m):
        return jnp.where(i < m[0], j, nn - 1)

    wspec = pl.BlockSpec((None, None, k, tn), lambda i, j, m: (layer, m[1 + live(i, m)], 0, col(i, j, m)))
    grid_spec = pltpu.PrefetchScalarGridSpec(
        num_scalar_prefetch=1, grid=(t // tm, nn),
        in_specs=[pl.BlockSpec((tm, k), lambda i, j, m: (live(i, m), 0)), wspec, wspec],
        out_specs=pl.BlockSpec((tm, tn), lambda i, j, m: (i, j)))
    return pl.pallas_call(
        _swiglu_up_kernel, grid_spec=grid_spec, out_shape=jax.ShapeDtypeStruct((t, n), BF16),
        compiler_params=_params("arbitrary", "arbitrary"), name=name,
    )(meta, x, wg, wu)


def _moe_down_kernel(meta_ref, h_ref, w_ref, o_ref, acc_ref, *, nk):
    k = pl.program_id(1)
    live = pl.program_id(0) < meta_ref[0]

    if nk == 1:
        @pl.when(live)
        def _():
            o_ref[...] = _dot(h_ref[...], w_ref[...])
    else:
        @pl.when(live & (k == 0))
        def _():
            acc_ref[...] = _dot(h_ref[...], w_ref[...])

        @pl.when(live & (k > 0) & (k < nk - 1))
        def _():
            acc_ref[...] += _dot(h_ref[...], w_ref[...])

        @pl.when(live & (k == nk - 1))
        def _():
            o_ref[...] = acc_ref[...] + _dot(h_ref[...], w_ref[...])

    @pl.when(jnp.logical_not(live) & (k == nk - 1))
    def _():
        o_ref[...] = jnp.zeros_like(o_ref)


def _moe_down_call(meta, h, wd, layer, tm):
    t, k = h.shape
    d = wd.shape[3]
    tk = _k_tile(k, 2048)
    nk = k // tk

    def live(i, m):
        return jnp.minimum(i, m[0] - 1)

    def kk(i, k_, m):
        return jnp.where(i < m[0], k_, nk - 1)

    grid_spec = pltpu.PrefetchScalarGridSpec(
        num_scalar_prefetch=1, grid=(t // tm, nk),
        in_specs=[pl.BlockSpec((tm, tk), lambda i, k_, m: (live(i, m), kk(i, k_, m))),
                  pl.BlockSpec((None, None, tk, d), lambda i, k_, m: (layer, m[1 + live(i, m)], kk(i, k_, m), 0))],
        out_specs=pl.BlockSpec((tm, d), lambda i, k_, m: (i, 0)),
        scratch_shapes=[pltpu.VMEM((tm, d), F32)])
    return pl.pallas_call(
        functools.partial(_moe_down_kernel, nk=nk), grid_spec=grid_spec,
        out_shape=jax.ShapeDtypeStruct((t, d), F32),
        compiler_params=_params("arbitrary", "arbitrary"), name="moe_down",
    )(meta, h, wd)


def _router_kernel(x_ref, wr_ref, e_ref, w_ref, r_ref, cnt_ref, carry_ref, *, n_exp, tm):
    @pl.when(pl.program_id(0) == 0)
    def _():
        carry_ref[...] = jnp.zeros_like(carry_ref)

    logits = _dot(x_ref[...], wr_ref[...])
    lane = lax.broadcasted_iota(I32, logits.shape, 1)
    lg = jnp.where(lane < n_exp, logits, -jnp.inf)
    m1 = jnp.max(lg, axis=-1, keepdims=True)
    i1 = jnp.min(jnp.where(lg == m1, lane, LANE), axis=-1, keepdims=True)
    lg2 = jnp.where(lane == i1, -jnp.inf, lg)
    m2 = jnp.max(lg2, axis=-1, keepdims=True)
    i2 = jnp.min(jnp.where(lg2 == m2, lane, LANE), axis=-1, keepdims=True)
    ex = jnp.exp(m2 - m1)
    w1 = 1.0 / (1.0 + ex)
    w2 = ex / (1.0 + ex)

    oh1 = (lane == i1).astype(F32)
    oh2 = (lane == i2).astype(F32)
    both = oh1 + oh2
    ri = lax.broadcasted_iota(I32, (tm, tm), 0)
    ci = lax.broadcasted_iota(I32, (tm, tm), 1)
    before = _dot((ri > ci).astype(BF16), both.astype(BF16)) + carry_ref[0:1, :]
    r1 = jnp.sum(before * oh1, axis=-1, keepdims=True)
    r2 = jnp.sum(before * oh2, axis=-1, keepdims=True)
    carry_ref[...] = carry_ref[...] + jnp.sum(both, axis=0, keepdims=True)

    e_ref[...] = jnp.where(lane == 0, i1, jnp.where(lane == 1, i2, 0))
    w_ref[...] = jnp.where(lane == 0, w1, jnp.where(lane == 1, w2, 0.0))
    r_ref[...] = jnp.where(lane == 0, r1, jnp.where(lane == 1, r2, 0.0)).astype(I32)
    cnt_ref[...] = carry_ref[...].astype(I32)


def _router_call(x, w_router):
    t, d = x.shape
    n_exp = w_router.shape[1]
    wr = jnp.zeros((d, LANE), BF16).at[:, :n_exp].set(w_router.astype(BF16))
    tm = _pick(t, 512)
    row = pl.BlockSpec((tm, LANE), lambda i: (i, 0))
    kern = functools.partial(_router_kernel, n_exp=n_exp, tm=tm)
    return pl.pallas_call(
        kern, grid=(t // tm,),
        in_specs=[pl.BlockSpec((tm, d), lambda i: (i, 0)), pl.BlockSpec((d, LANE), lambda i: (0, 0))],
        out_specs=[row, row, row, pl.BlockSpec((SUBLANE, LANE), lambda i: (0, 0))],
        out_shape=[jax.ShapeDtypeStruct((t, LANE), I32), jax.ShapeDtypeStruct((t, LANE), F32),
                   jax.ShapeDtypeStruct((t, LANE), I32), jax.ShapeDtypeStruct((SUBLANE, LANE), I32)],
        scratch_shapes=[pltpu.VMEM((SUBLANE, LANE), F32)],
        compiler_params=_params("arbitrary"), name="router",
    )(x, wr)


_DMA_UNROLL = 8
_DMA_PRIORITIES = 2


def _row_copy(src_hbm, dst_vmem, src_row, dst_row, sem):
    return pltpu.make_async_copy(src_hbm.at[pl.ds(src_row, 1)], dst_vmem.at[pl.ds(dst_row, 1)], sem)


def _gather_kernel(tok_ref, x_hbm, o_ref, sem, *, rows):
    def start(pair, carry):
        for prio in range(_DMA_PRIORITIES):
            r = pair * _DMA_PRIORITIES + prio
            _row_copy(x_hbm, o_ref, tok_ref[0, r], r, sem).start(priority=prio)
        return carry

    lax.fori_loop(0, rows // _DMA_PRIORITIES, start, 0, unroll=_DMA_UNROLL // _DMA_PRIORITIES)
    pltpu.make_async_copy(x_hbm.at[pl.ds(0, rows)], o_ref, sem).wait()


def _gather_call(x, tok, rows):
    cap = tok.shape[0]
    d = x.shape[1]
    tok3 = tok.reshape(cap // rows, 1, rows)
    return pl.pallas_call(
        functools.partial(_gather_kernel, rows=rows), grid=(cap // rows,),
        in_specs=[pl.BlockSpec((None, 1, rows), lambda i: (i, 0, 0), memory_space=pltpu.SMEM),
                  pl.BlockSpec(memory_space=pl.ANY)],
        out_specs=pl.BlockSpec((rows, d), lambda i: (i, 0)),
        out_shape=jax.ShapeDtypeStruct((cap, d), x.dtype),
        scratch_shapes=[pltpu.SemaphoreType.DMA(())],
        compiler_params=_params("arbitrary"), name="moe_gather",
    )(tok3, x)


def _combine_ln_kernel(d0_ref, d1_ref, o_hbm, w_ref, x_ref, g_ref, b_ref, o32_ref, o16_ref, buf0, buf1, sem,
                       *, alpha, rows):
    def start(r, carry):
        _row_copy(o_hbm, buf0, d0_ref[0, r], r, sem).start(priority=0)
        _row_copy(o_hbm, buf1, d1_ref[0, r], r, sem).start(priority=1)
        return carry

    lax.fori_loop(0, rows, start, 0, unroll=_DMA_UNROLL)
    pltpu.make_async_copy(o_hbm.at[pl.ds(0, rows)], buf0, sem).wait()
    pltpu.make_async_copy(o_hbm.at[pl.ds(0, rows)], buf1, sem).wait()
    wts = w_ref[...]
    ffn = wts[:, 0:1] * buf0[...] + wts[:, 1:2] * buf1[...]
    y = _ln_rows(alpha * x_ref[...] + ffn, g_ref[...], b_ref[...])
    o32_ref[...] = y
    o16_ref[...] = y.astype(BF16)


def _combine_ln_call(dest, o_sorted, wts, x, g, b, alpha):
    t, d = x.shape
    rows = _pick(t, 512)
    d0 = dest[:, 0].reshape(t // rows, 1, rows)
    d1 = dest[:, 1].reshape(t // rows, 1, rows)
    idx = pl.BlockSpec((None, 1, rows), lambda i: (i, 0, 0), memory_space=pltpu.SMEM)
    row = pl.BlockSpec((rows, d), lambda i: (i, 0))
    vec = pl.BlockSpec((1, d), lambda i: (0, 0))
    kern = functools.partial(_combine_ln_kernel, alpha=alpha, rows=rows)
    return pl.pallas_call(
        kern, grid=(t // rows,),
        in_specs=[idx, idx, pl.BlockSpec(memory_space=pl.ANY), pl.BlockSpec((rows, LANE), lambda i: (i, 0)),
                  row, vec, vec],
        out_specs=[row, row],
        out_shape=[jax.ShapeDtypeStruct((t, d), F32), jax.ShapeDtypeStruct((t, d), BF16)],
        scratch_shapes=[pltpu.VMEM((rows, d), F32), pltpu.VMEM((rows, d), F32), pltpu.SemaphoreType.DMA(())],
        compiler_params=_params("arbitrary"), name="moe_combine_ln",
    )(d0, d1, o_sorted, wts, x, g.reshape(1, d), b.reshape(1, d))


def _moe_layer(xf, xb, w_router, wg, wu, wd, layer, ln_g, ln_b, alpha):
    t, d = xf.shape
    n_exp = wg.shape[1]
    e_out, w_out, r_out, cnt = _router_call(xb, w_router)
    e = e_out[:, :MOE_TOP_K]
    counts = cnt[0, :n_exp]

    tm = 512 if t * MOE_TOP_K >= 8 * 512 else 128
    n_tiles = -(-(t * MOE_TOP_K) // tm) + n_exp
    cap = n_tiles * tm
    padded = (counts + tm - 1) // tm * tm
    pad_end = jnp.cumsum(padded)
    pad_start = pad_end - padded
    dest = (pad_start[e] + r_out[:, :MOE_TOP_K]).astype(I32)
    tok = jnp.repeat(jnp.arange(t, dtype=I32), MOE_TOP_K)
    slot_tok = jnp.zeros((cap,), I32).at[dest.reshape(-1)].set(tok)
    tile_expert = jnp.minimum(jnp.searchsorted(pad_end, jnp.arange(n_tiles, dtype=I32) * tm, side="right"),
                              n_exp - 1)
    meta = jnp.concatenate([(pad_end[-1:] // tm), tile_expert]).astype(I32)

    xs = _gather_call(xf, slot_tok, _pick(cap, 1024))
    h = _swiglu_up_call(meta, xs, wg, wu, layer, tm, "moe_up")
    o_sorted = _moe_down_call(meta, h, wd, layer, tm)
    return _combine_ln_call(dest, o_sorted, w_out, xf, ln_g, ln_b, alpha)


def _trunk(x, conv_bufs, ssm_states, pr, want_v):
    bsz, seq, d = x.shape
    t = bsz * seq
    depth = pr["w_zx"].shape[0]
    alpha = float((2 * depth) ** 0.25)
    d_inner = pr["w_br_a"].shape[1]
    cdim = pr["conv_w"].shape[2]
    gd = pr["w_br_b"].shape[1]
    kc = pr["conv_w"].shape[1]
    off_xbc = d_inner
    off_gate = 2 * gd
    q_ssd = _pick(seq, 128)
    n_groups = (cdim - d_inner) // (2 * ssm_states.shape[-1])
    gs_ssd = next(c for c in (4, 2, 1) if n_groups % c == 0)
    q_gmlp = min(pr["w_s"].shape[2], seq)

    xf, xb = _ln_call(x.reshape(t, d), pr["ln_in_g"], pr["ln_in_b"])
    convs, ssms, vs = [], [], []
    w_main, w_ug = pr["w_zx"], pr["w_ug"]
    lane_aligned = q_ssd % LANE == 0
    for i in range(depth):
        sz = _proj_call(xb, w_main, i, 0, d_inner, "silu", BF16, "in_proj_z")
        uv = _proj_call(xb, w_ug, i, 0, 2 * gd, "gelu", F32, "in_proj_uv")
        gates = _proj_call(xb, w_ug, i, off_gate, 2 * d, "none", BF16, "in_proj_gate", bias=pr["b_gate"][i])
        dt, acum = _dt_call(xb, pr["w_dt"][i], pr["dt_bias"][i], pr["a_log_pad"][i], q_ssd)
        if lane_aligned:
            xbc, tails = _proj_conv_call(xb, w_main, i, off_xbc, pr["conv_w"][i], pr["conv_b"][i], conv_bufs[i], seq)
            ya, ssm_i = _ssd_cm_call(xbc, sz, dt, acum, ssm_states[i], pr["d_skip"][i], pr["ssd_norm_g"][i],
                                     bsz=bsz, seq=seq, q=q_ssd, gs=gs_ssd)
            new_rows = tails.reshape(bsz, -1, _CONV_PAD, cdim)[:, -1, _CONV_PAD - (kc - 1):]
        else:
            xbc = _proj_call(xb, w_main, i, off_xbc, cdim, "none", F32, "in_proj_xbc")
            ya, ssm_i = _ssd_call(xbc, sz, dt, conv_bufs[i], ssm_states[i], pr["conv_w"][i], pr["conv_b"][i],
                                  pr["a_log"][i], pr["d_skip"][i], pr["ssd_norm_g"][i], bsz=bsz, seq=seq, q=q_ssd)
            new_rows = xbc.reshape(bsz, seq, cdim)[:, max(seq - (kc - 1), 0):]
        conv_i = jnp.concatenate([conv_bufs[i], new_rows], axis=1)[:, -(kc - 1):]
        yb, *vn = _gmlp_call(uv, pr["gmlp_ln_g"][i], pr["gmlp_ln_b"][i], pr["w_s"][i], pr["b_s"][i],
                             bsz=bsz, seq=seq, q=q_gmlp, want_v=want_v)
        merged = _merge_call(ya, yb, gates, pr["w_br_a"], pr["w_br_b"], i)
        xf, xb = _mm_res_ln_call(merged, pr["w_o"], i, xf, pr["ln1_g"][i], pr["ln1_b"][i], alpha, "out_proj_ln")
        j = i // 2
        if i % 2 == 0:
            tm = _pick(t, 1024)
            meta = jnp.concatenate([jnp.full((1,), t // tm, I32), jnp.zeros((t // tm,), I32)])
            h = _swiglu_up_call(meta, xb, pr["w_ff_gate"], pr["w_ff_up"], j, tm, "ffn_up")
            xf, xb = _mm_res_ln_call(h, pr["w_ff_down"], j, xf, pr["ln2_g"][i], pr["ln2_b"][i], alpha,
                                     "ffn_down_ln")
        else:
            xf, xb = _moe_layer(xf, xb, pr["w_router"][j], pr["w_moe_gate"], pr["w_moe_up"], pr["w_moe_down"], j,
                                pr["ln2_g"][i], pr["ln2_b"][i], alpha)
        convs.append(conv_i)
        ssms.append(ssm_i)
        vs.extend(v.reshape(bsz, seq, gd) for v in vn)
    return xf.reshape(bsz, seq, d), jnp.stack(convs), jnp.stack(ssms), (jnp.stack(vs) if want_v else None)


def kernel(x_prompt, x_sample, cache_conv, state_ssm, ln_in_g, ln_in_b, w_in, conv_w, conv_b, dt_bias, a_log,
           d_skip, ssd_norm_g, gmlp_ln_g, gmlp_ln_b, w_s, b_s, b_gate, w_br_a, w_br_b, w_o, ln1_g, ln1_b,
           w_ff_gate, w_ff_up, w_ff_down, w_router, w_moe_gate, w_moe_up, w_moe_down, ln2_g, ln2_b):
    depth, d_model, _ = w_in.shape
    d_inner = w_br_a.shape[1]
    cdim = conv_w.shape[2]
    heads = a_log.shape[1]
    o_dt = d_inner + cdim
    w_in16 = w_in.astype(BF16)
    w_ug = w_in16[:, :, o_dt + heads:]
    w_dt = jnp.zeros((depth, d_model, LANE), BF16).at[:, :, :heads].set(w_in16[:, :, o_dt:o_dt + heads])
    dt_b = jnp.zeros((depth, 1, LANE), F32).at[:, 0, :heads].set(dt_bias)
    al_pad = jnp.zeros((depth, 1, LANE), F32).at[:, 0, :heads].set(a_log)
    pr = dict(a_log_pad=al_pad,
        ln_in_g=ln_in_g, ln_in_b=ln_in_b, w_zx=w_in16, w_ug=w_ug, w_dt=w_dt, dt_bias=dt_b, conv_w=conv_w, conv_b=conv_b,
        a_log=a_log, d_skip=d_skip, ssd_norm_g=ssd_norm_g, gmlp_ln_g=gmlp_ln_g, gmlp_ln_b=gmlp_ln_b, w_s=w_s,
        b_s=b_s, b_gate=b_gate, w_br_a=w_br_a.astype(BF16), w_br_b=w_br_b.astype(BF16), w_o=w_o.astype(BF16),
        ln1_g=ln1_g, ln1_b=ln1_b, w_ff_gate=w_ff_gate.astype(BF16)[:, None], w_ff_up=w_ff_up.astype(BF16)[:, None],
        w_ff_down=w_ff_down.astype(BF16), w_router=w_router, w_moe_gate=w_moe_gate.astype(BF16),
        w_moe_up=w_moe_up.astype(BF16), w_moe_down=w_moe_down.astype(BF16), ln2_g=ln2_g, ln2_b=ln2_b)
    bp = x_prompt.shape[0]
    zero_conv = jnp.zeros((depth, bp) + cache_conv.shape[2:], x_prompt.dtype)
    zero_ssm = jnp.zeros((depth, bp) + state_ssm.shape[2:], state_ssm.dtype)
    y_prompt, prompt_conv, prompt_ssm, _ = _trunk(x_prompt, zero_conv, zero_ssm, pr, want_v=False)
    y_sample, sample_conv, sample_ssm, sample_v = _trunk(x_sample, cache_conv, state_ssm, pr, want_v=True)
    return (y_prompt, y_sample, prompt_conv, prompt_ssm, sample_conv, sample_ssm, sample_v)
```

```python
import functools

import jax
import jax.numpy as jnp
from jax import lax
from jax.experimental import pallas as pl
from jax.experimental.pallas import tpu as pltpu

F32 = jnp.float32
BF16 = jnp.bfloat16
I32 = jnp.int32
HIGHEST = lax.Precision.HIGHEST

LN_EPS = 1e-5
LANE = 128
SUBLANE = 8
V7X_VMEM_BYTES = 64 * 1024 * 1024
VMEM_LIMIT_BYTES = V7X_VMEM_BYTES - 8 * 1024 * 1024
MOE_TOP_K = 2
_NEG_BIG = -1e30
_TILE_CANDIDATES = (1024, 512, 256, 128, 64, 32, 16, 8)


def _pick(n, cap):
    for c in _TILE_CANDIDATES:
        if c <= cap and n % c == 0:
            return c
    raise ValueError(f"no tile for {n}")


def _params(*sem):
    return pltpu.CompilerParams(dimension_semantics=sem, vmem_limit_bytes=VMEM_LIMIT_BYTES)


def _ln_rows(x, g, b):
    mu = jnp.mean(x, axis=-1, keepdims=True)
    xc = x - mu
    var = jnp.mean(xc * xc, axis=-1, keepdims=True)
    return xc * lax.rsqrt(var + LN_EPS) * g + b


def _sigmoid(x):
    return 0.5 * jnp.tanh(0.5 * x) + 0.5


def _silu(x):
    return x * _sigmoid(x)


def _gelu(x):
    return 0.5 * x * (1.0 + lax.erf(x * (2.0 ** -0.5)))


def _dot(a, b):
    return jnp.dot(a, b, preferred_element_type=F32)


def _dot_exact(a, b):
    return jnp.dot(a, b, precision=HIGHEST, preferred_element_type=F32)


def _ln_kernel(x_ref, g_ref, b_ref, o32_ref, o16_ref):
    y = _ln_rows(x_ref[...], g_ref[...], b_ref[...])
    o32_ref[...] = y
    o16_ref[...] = y.astype(BF16)


def _ln_call(x, g, b):
    t, d = x.shape
    tm = _pick(t, 512)
    row = pl.BlockSpec((tm, d), lambda i: (i, 0))
    vec = pl.BlockSpec((1, d), lambda i: (0, 0))
    return pl.pallas_call(
        _ln_kernel, grid=(t // tm,), in_specs=[row, vec, vec], out_specs=[row, row],
        out_shape=[jax.ShapeDtypeStruct((t, d), F32), jax.ShapeDtypeStruct((t, d), BF16)],
        compiler_params=_params("parallel"), name="ln_in",
    )(x, g.reshape(1, d), b.reshape(1, d))


_CONV_PAD = SUBLANE

_ACTIVATIONS = {"none": lambda r: r, "silu": _silu, "gelu": _gelu}


def _proj_kernel(x_ref, w_ref, o_ref, *, act):
    o_ref[...] = _ACTIVATIONS[act](_dot(x_ref[...], w_ref[...])).astype(o_ref.dtype)


def _proj_gate_kernel(x_ref, w_ref, b_ref, o_ref):
    o_ref[...] = _sigmoid(_dot(x_ref[...], w_ref[...]) + b_ref[...]).astype(o_ref.dtype)


def _proj_conv_kernel(x_ref, w_ref, cw_ref, cc_ref, cbuf_ref, o_ref, tail_ref, halo_ref, *, kc, tiles_per_seq):
    i, j = pl.program_id(0), pl.program_id(1)
    tm = x_ref.shape[0]
    r = _dot(x_ref[...], w_ref[...])
    before = jnp.where(lax.rem(i, tiles_per_seq) == 0, cbuf_ref[...], halo_ref[j])
    nb = tm // _CONV_PAD
    blocks = r.reshape(nb, _CONV_PAD, r.shape[1])
    prev_blocks = jnp.concatenate([before[None], blocks[:nb - 1]], axis=0)
    sub = lax.broadcasted_iota(I32, blocks.shape, 1)
    acc = cc_ref[...] + r * cw_ref[kc - 1:kc, :]
    for s in range(1, kc):
        shifted = jnp.where(sub < s, pltpu.roll(prev_blocks, s, 1), pltpu.roll(blocks, s, 1))
        acc = acc + shifted.reshape(r.shape) * cw_ref[kc - 1 - s:kc - s, :]
    o_ref[...] = _silu(acc)
    last_rows = r[tm - _CONV_PAD:tm, :]
    tail_ref[...] = last_rows
    halo_ref[j] = last_rows


def _region_tiles(t, n, col0):
    tm = _pick(t, 1024)
    tn = next(c for c in _TILE_CANDIDATES if n % c == 0 and col0 % c == 0)
    return tm, tn


def _proj_call(x, w, layer, col0, n, act, out_dtype, name, bias=None):
    t, k = x.shape
    tm, tn = _region_tiles(t, n, col0)
    c0 = col0 // tn
    in_specs = [pl.BlockSpec((tm, k), lambda i, j: (i, 0)), pl.BlockSpec((None, k, tn), lambda i, j: (layer, 0, c0 + j))]
    args = [x, w]
    if bias is None:
        kern = functools.partial(_proj_kernel, act=act)
    else:
        kern = _proj_gate_kernel
        in_specs.append(pl.BlockSpec((1, tn), lambda i, j: (0, j)))
        args.append(bias.reshape(1, n))
    return pl.pallas_call(
        kern, grid=(t // tm, n // tn), in_specs=in_specs,
        out_specs=pl.BlockSpec((tm, tn), lambda i, j: (i, j)),
        out_shape=jax.ShapeDtypeStruct((t, n), out_dtype),
        compiler_params=_params("parallel", "parallel"), name=name,
    )(*args)


def _proj_conv_call(x, w, layer, col0, conv_w, conv_b, conv_buf, seq):
    t, k = x.shape
    kc, n = conv_w.shape
    tm, tn = _pick(seq, 1024), _region_tiles(t, n, col0)[1]
    assert t % tm == 0 and kc - 1 <= _CONV_PAD <= tm
    tiles_per_seq = seq // tm
    c0 = col0 // tn
    cbuf = jnp.pad(conv_buf, ((0, 0), (_CONV_PAD - (kc - 1), 0), (0, 0)))
    kern = functools.partial(_proj_conv_kernel, kc=kc, tiles_per_seq=tiles_per_seq)
    return pl.pallas_call(
        kern, grid=(t // tm, n // tn),
        in_specs=[pl.BlockSpec((tm, k), lambda i, j: (i, 0)),
                  pl.BlockSpec((None, k, tn), lambda i, j: (layer, 0, c0 + j)),
                  pl.BlockSpec((kc, tn), lambda i, j: (0, j)),
                  pl.BlockSpec((1, tn), lambda i, j: (0, j)),
                  pl.BlockSpec((None, _CONV_PAD, tn), lambda i, j: (i // tiles_per_seq, 0, j))],
        out_specs=[pl.BlockSpec((tm, tn), lambda i, j: (i, j)),
                   pl.BlockSpec((None, _CONV_PAD, tn), lambda i, j: (i, 0, j))],
        out_shape=[jax.ShapeDtypeStruct((t, n), F32), jax.ShapeDtypeStruct((t // tm, _CONV_PAD, n), F32)],
        scratch_shapes=[pltpu.VMEM((n // tn, _CONV_PAD, tn), F32)],
        compiler_params=_params("arbitrary", "arbitrary"), name="in_proj_conv",
    )(x, w, conv_w, conv_b.reshape(1, n), cbuf)


def _dt_kernel(x_ref, w_ref, b_ref, al_ref, dt_ref, ac_ref, *, q):
    raw = _dot(x_ref[...], w_ref[...]) + b_ref[...]
    dt = jnp.maximum(raw, 0.0) + jnp.log1p(jnp.exp(-jnp.abs(raw)))
    dt_ref[...] = dt
    a = dt * (-jnp.exp(al_ref[...]))
    ri = lax.broadcasted_iota(I32, (q, q), 0)
    ci = lax.broadcasted_iota(I32, (q, q), 1)
    tril = (ri >= ci).astype(F32)
    for c in range(a.shape[0] // q):
        ac_ref[c * q:(c + 1) * q, :] = _dot_exact(tril, a[c * q:(c + 1) * q, :])


def _dt_call(x, w, bias, a_log, q):
    t, k = x.shape
    n = w.shape[1]
    tm = max(_pick(t, 512), q)
    assert tm % q == 0 and t % tm == 0
    row = pl.BlockSpec((tm, n), lambda i: (i, 0))
    vec = pl.BlockSpec((1, n), lambda i: (0, 0))
    return pl.pallas_call(
        functools.partial(_dt_kernel, q=q), grid=(t // tm,),
        in_specs=[pl.BlockSpec((tm, k), lambda i: (i, 0)), pl.BlockSpec((k, n), lambda i: (0, 0)), vec, vec],
        out_specs=[row, row],
        out_shape=[jax.ShapeDtypeStruct((t, n), F32), jax.ShapeDtypeStruct((t, n), F32)],
        compiler_params=_params("parallel"), name="dt_proj",
    )(x, w, bias, a_log)


def _ssd_kernel(xs_ref, bm_ref, cm_ref, sz_ref, dt_ref, dtt_ref, cbx_ref, cbb_ref, cbc_ref, s0_ref,
                cwx_ref, cwb_ref, cwc_ref, ccx_ref, ccb_ref, ccc_ref, alr_ref, alc_ref, dsk_ref, ng_ref,
                y_ref, sout_ref, xpad_ref, st_ref, *, q, hpg, p, n, kc, nc):
    c = pl.program_id(2)
    w = hpg * p
    tail = kc - 1
    t0 = _CONV_PAD - tail

    @pl.when(c == 0)
    def _():
        xpad_ref[t0:_CONV_PAD, 0:w] = cbx_ref[...]
        xpad_ref[t0:_CONV_PAD, w:w + n] = cbb_ref[...]
        xpad_ref[t0:_CONV_PAD, w + n:w + 2 * n] = cbc_ref[...]
        st_ref[...] = s0_ref[...].reshape(w, n).T

    xpad_ref[_CONV_PAD:_CONV_PAD + q, 0:w] = xs_ref[...]
    xpad_ref[_CONV_PAD:_CONV_PAD + q, w:w + n] = bm_ref[...]
    xpad_ref[_CONV_PAD:_CONV_PAD + q, w + n:w + 2 * n] = cm_ref[...]

    def conv(lo, hi, cw_ref, cc_ref):
        acc = cc_ref[...]
        for k in range(kc):
            acc = acc + xpad_ref[t0 + k:t0 + k + q, lo:hi] * cw_ref[k:k + 1, :]
        return _silu(acc)

    xs = conv(0, w, cwx_ref, ccx_ref)
    bm = conv(w, w + n, cwb_ref, ccb_ref)
    cm = conv(w + n, w + 2 * n, cwc_ref, ccc_ref)
    xpad_ref[t0:_CONV_PAD, :] = xpad_ref[t0 + q:_CONV_PAD + q, :]

    a = dt_ref[...] * (-jnp.exp(alr_ref[...]))
    a_t = dtt_ref[...] * (-jnp.exp(alc_ref[...]))
    ri = lax.broadcasted_iota(I32, (q, q), 0)
    ci = lax.broadcasted_iota(I32, (q, q), 1)
    tril = ri >= ci
    acol = _dot_exact(tril.astype(F32), a)
    arow = _dot_exact(a_t, (ri <= ci).astype(F32))

    hrow = lax.broadcasted_iota(I32, (hpg, w), 0)
    hlane = lax.broadcasted_iota(I32, (hpg, w), 1)
    expand = ((hlane >= hrow * p) & (hlane < (hrow + 1) * p)).astype(F32)
    xdt = xs * _dot_exact(dt_ref[...], expand)
    eacol = _dot_exact(jnp.exp(acol), expand)
    to_end = _dot_exact(jnp.exp(acol[q - 1:q, :] - acol), expand)

    bm16 = bm.astype(BF16)
    cm16 = cm.astype(BF16)
    cb = lax.dot_general(cm16, bm16, (((1,), (1,)), ((), ())), preferred_element_type=F32)
    st = st_ref[...]
    y = _dot(cm16, st.astype(BF16)) * eacol
    lane = lax.broadcasted_iota(I32, (q, w), 1)
    for h in range(hpg):
        seg = acol[:, h:h + 1] - arow[h:h + 1, :]
        m = (cb * jnp.exp(jnp.where(tril, seg, _NEG_BIG))).astype(BF16)
        head = (lane >= h * p) & (lane < (h + 1) * p)
        y = y + _dot(m, jnp.where(head, xdt, 0.0).astype(BF16))
    y = y + dsk_ref[...] * xs
    hz = y * sz_ref[...]
    y_ref[...] = (hz * lax.rsqrt(jnp.mean(hz * hz, axis=-1, keepdims=True) + LN_EPS) * ng_ref[...]).astype(BF16)

    upd = lax.dot_general(bm16, (xdt * to_end).astype(BF16), (((0,), (0,)), ((), ())),
                          preferred_element_type=F32)
    st_ref[...] = st * eacol[q - 1:q, :] + upd

    @pl.when(c == nc - 1)
    def _():
        sout_ref[...] = st_ref[...].T.reshape(hpg, p, n)


def _ssd_call(xbc, sz, dt, conv_buf, state0, conv_w, conv_b, a_log, d_skip, norm_g, *, bsz, seq, q):
    heads, p, n = state0.shape[1:]
    kc, cdim = conv_w.shape
    d_inner = sz.shape[1]
    groups = (cdim - d_inner) // (2 * n)
    hpg = heads // groups
    w = hpg * p
    nc = seq // q
    t = bsz * seq
    assert w * groups == d_inner and seq % q == 0 and q >= kc - 1 and d_inner % n == 0

    dt4 = dt[:, :heads].reshape(bsz, seq, groups, hpg).transpose(0, 2, 1, 3)
    dtt4 = dt4.transpose(0, 1, 3, 2)
    alr = a_log.reshape(groups, 1, hpg)
    alc = a_log.reshape(groups, hpg, 1)
    dsk = jnp.repeat(d_skip, p).reshape(1, d_inner)
    ng = norm_g.reshape(1, d_inner)
    ccb = conv_b.reshape(1, cdim)

    bb, cb_ = d_inner // n, d_inner // n + groups
    row = lambda b, g, c: b * nc + c
    in_specs = [
        pl.BlockSpec((q, w), lambda b, g, c: (row(b, g, c), g)),
        pl.BlockSpec((q, n), lambda b, g, c: (row(b, g, c), bb + g)),
        pl.BlockSpec((q, n), lambda b, g, c: (row(b, g, c), cb_ + g)),
        pl.BlockSpec((q, w), lambda b, g, c: (row(b, g, c), g)),
        pl.BlockSpec((None, None, q, hpg), lambda b, g, c: (b, g, c, 0)),
        pl.BlockSpec((None, None, hpg, q), lambda b, g, c: (b, g, 0, c)),
        pl.BlockSpec((None, kc - 1, w), lambda b, g, c: (b, 0, g)),
        pl.BlockSpec((None, kc - 1, n), lambda b, g, c: (b, 0, d_inner // n + g)),
        pl.BlockSpec((None, kc - 1, n), lambda b, g, c: (b, 0, d_inner // n + groups + g)),
        pl.BlockSpec((None, hpg, p, n), lambda b, g, c: (b, g, 0, 0)),
        pl.BlockSpec((kc, w), lambda b, g, c: (0, g)),
        pl.BlockSpec((kc, n), lambda b, g, c: (0, d_inner // n + g)),
        pl.BlockSpec((kc, n), lambda b, g, c: (0, d_inner // n + groups + g)),
        pl.BlockSpec((1, w), lambda b, g, c: (0, g)),
        pl.BlockSpec((1, n), lambda b, g, c: (0, d_inner // n + g)),
        pl.BlockSpec((1, n), lambda b, g, c: (0, d_inner // n + groups + g)),
        pl.BlockSpec((None, 1, hpg), lambda b, g, c: (g, 0, 0)),
        pl.BlockSpec((None, hpg, 1), lambda b, g, c: (g, 0, 0)),
        pl.BlockSpec((1, w), lambda b, g, c: (0, g)),
        pl.BlockSpec((1, w), lambda b, g, c: (0, g)),
    ]
    out_specs = [
        pl.BlockSpec((q, w), lambda b, g, c: (row(b, g, c), g)),
        pl.BlockSpec((None, hpg, p, n), lambda b, g, c: (b, g, 0, 0)),
    ]
    kern = functools.partial(_ssd_kernel, q=q, hpg=hpg, p=p, n=n, kc=kc, nc=nc)
    return pl.pallas_call(
        kern, grid=(bsz, groups, nc), in_specs=in_specs, out_specs=out_specs,
        out_shape=[jax.ShapeDtypeStruct((t, d_inner), BF16), jax.ShapeDtypeStruct(state0.shape, F32)],
        scratch_shapes=[pltpu.VMEM((_CONV_PAD + q, w + 2 * n), F32), pltpu.VMEM((n, w), F32)],
        compiler_params=_params("parallel", "parallel", "arbitrary"), name="ssd",
    )(xbc, xbc, xbc, sz, dt4, dtt4, conv_buf, conv_buf, conv_buf, state0,
      conv_w, conv_w, conv_w, ccb, ccb, ccb, alr, alc, dsk, ng)


def _ssd_cm_kernel(xs_ref, bm_ref, cm_ref, sz_ref, dtt_ref, act_ref, ac_ref, s0_ref, dsk_ref, ng_ref,
                   y_ref, sout_ref, st_ref, *, q, gs, hpg, p, n, nc):
    c = pl.program_id(2)
    w = hpg * p

    @pl.when(c == 0)
    def _():
        st_ref[...] = s0_ref[...].reshape(gs * w, n)

    def rows(v):
        return jnp.concatenate([jnp.broadcast_to(v[h:h + 1, :], (p, v.shape[1])) for h in range(hpg)], axis=0)

    si = lax.broadcasted_iota(I32, (q, q), 0)
    li = lax.broadcasted_iota(I32, (q, q), 1)
    keep = si <= li
    nt = (((1,), (1,)), ((), ()))
    for g in range(gs):
        xs = xs_ref[:, g * w:(g + 1) * w]
        bm16 = bm_ref[:, g * n:(g + 1) * n].astype(BF16)
        cm16 = cm_ref[:, g * n:(g + 1) * n].astype(BF16)
        arow = act_ref[g]
        acol = ac_ref[g]
        ear = jnp.exp(arow)
        te = jnp.exp(arow[:, q - 1:q] - arow)
        xdt_t = xs.T * rows(dtt_ref[g])
        xdt16 = xdt_t.astype(BF16)
        cb_t = lax.dot_general(bm16, cm16, nt, preferred_element_type=F32)
        ys = []
        for h in range(hpg):
            seg = arow[h:h + 1, :] - acol[:, h:h + 1]
            m_t = (cb_t * jnp.exp(jnp.where(keep, seg, _NEG_BIG))).astype(BF16)
            ys.append(_dot(xdt16[h * p:(h + 1) * p, :], m_t))
        st = st_ref[g * w:(g + 1) * w, :]
        y_t = jnp.concatenate(ys, axis=0)
        y_t = y_t + lax.dot_general(st.astype(BF16), cm16, nt, preferred_element_type=F32) * rows(ear)
        y = y_t.T + dsk_ref[:, g * w:(g + 1) * w] * xs
        hz = y * sz_ref[:, g * w:(g + 1) * w]
        y_ref[:, g * w:(g + 1) * w] = (hz * lax.rsqrt(jnp.mean(hz * hz, axis=-1, keepdims=True) + LN_EPS)
                                       * ng_ref[:, g * w:(g + 1) * w]).astype(BF16)
        upd = _dot((xdt_t * rows(te)).astype(BF16), bm16)
        st_ref[g * w:(g + 1) * w, :] = st * rows(ear[:, q - 1:q]) + upd

    @pl.when(c == nc - 1)
    def _():
        sout_ref[...] = st_ref[...].reshape(gs * hpg, p, n)


def _ssd_cm_call(xbc, sz, dt, acum, state0, d_skip, norm_g, *, bsz, seq, q, gs):
    heads, p, n = state0.shape[1:]
    cdim = xbc.shape[1]
    d_inner = sz.shape[1]
    groups = (cdim - d_inner) // (2 * n)
    hpg = heads // groups
    w = hpg * p
    nc = seq // q
    t = bsz * seq
    gw, gn = gs * w, gs * n
    assert w * groups == d_inner and seq % q == 0 and q % LANE == 0 and groups % gs == 0
    assert d_inner % gn == 0 and (groups * n) % gn == 0

    def heads_major(v):
        return v[:, :heads].reshape(bsz, seq, groups, hpg).transpose(0, 2, 3, 1)

    dtt4 = heads_major(dt)
    act4 = heads_major(acum)
    ac4 = act4.transpose(0, 1, 3, 2)
    dsk = jnp.repeat(d_skip, p).reshape(1, d_inner)
    ng = norm_g.reshape(1, d_inner)

    bb, cb_ = d_inner // gn, (d_inner + groups * n) // gn
    row = lambda b, g, c: b * nc + c
    in_specs = [
        pl.BlockSpec((q, gw), lambda b, g, c: (row(b, g, c), g)),
        pl.BlockSpec((q, gn), lambda b, g, c: (row(b, g, c), bb + g)),
        pl.BlockSpec((q, gn), lambda b, g, c: (row(b, g, c), cb_ + g)),
        pl.BlockSpec((q, gw), lambda b, g, c: (row(b, g, c), g)),
        pl.BlockSpec((None, gs, hpg, q), lambda b, g, c: (b, g, 0, c)),
        pl.BlockSpec((None, gs, hpg, q), lambda b, g, c: (b, g, 0, c)),
        pl.BlockSpec((None, gs, q, hpg), lambda b, g, c: (b, g, c, 0)),
        pl.BlockSpec((None, gs * hpg, p, n), lambda b, g, c: (b, g, 0, 0)),
        pl.BlockSpec((1, gw), lambda b, g, c: (0, g)),
        pl.BlockSpec((1, gw), lambda b, g, c: (0, g)),
    ]
    out_specs = [
        pl.BlockSpec((q, gw), lambda b, g, c: (row(b, g, c), g)),
        pl.BlockSpec((None, gs * hpg, p, n), lambda b, g, c: (b, g, 0, 0)),
    ]
    kern = functools.partial(_ssd_cm_kernel, q=q, gs=gs, hpg=hpg, p=p, n=n, nc=nc)
    return pl.pallas_call(
        kern, grid=(bsz, groups // gs, nc), in_specs=in_specs, out_specs=out_specs,
        out_shape=[jax.ShapeDtypeStruct((t, d_inner), BF16), jax.ShapeDtypeStruct(state0.shape, F32)],
        scratch_shapes=[pltpu.VMEM((gs * w, n), F32)],
        compiler_params=_params("parallel", "parallel", "arbitrary"), name="ssd_cm",
    )(xbc, xbc, xbc, sz, dtt4, act4, ac4, state0, dsk, ng)


_GMLP_CHUNKS_PER_STEP = (4, 2, 1)


def _gmlp_kernel(u_ref, v_ref, lg_ref, lb_ref, ws_ref, bs_ref, yb_ref, *vn_refs, groups, q):
    u = u_ref[...]
    vn = _ln_rows(v_ref[...], lg_ref[...], lb_ref[...])
    for vn_ref in vn_refs:
        vn_ref[...] = vn
    vn16 = vn.astype(BF16)
    d = vn.shape[1] // groups
    ri = lax.broadcasted_iota(I32, (q, q), 0)
    ci = lax.broadcasted_iota(I32, (q, q), 1)
    for g in range(groups):
        wg = jnp.where(ri >= ci, ws_ref[g], 0.0).astype(BF16)
        for c in range(vn.shape[0] // q):
            rows = slice(c * q, (c + 1) * q)
            s = _dot(wg, vn16[rows, g * d:(g + 1) * d]) + bs_ref[g]
            yb_ref[rows, g * d:(g + 1) * d] = (u[rows, g * d:(g + 1) * d] * s).astype(BF16)


def _gmlp_call(uv, ln_g, ln_b, w_s, b_s, *, bsz, seq, q, want_v):
    groups = w_s.shape[0]
    t = bsz * seq
    gd = uv.shape[1] // 2
    assert seq % q == 0 and (gd // groups) % LANE == 0
    ws = w_s[:, :q, :q]
    bs = b_s[:, :q, None]
    rows = next(c * q for c in _GMLP_CHUNKS_PER_STEP if seq % (c * q) == 0)
    row = pl.BlockSpec((rows, gd), lambda i: (i, 0))
    kern = functools.partial(_gmlp_kernel, groups=groups, q=q)
    out_shape = [jax.ShapeDtypeStruct((t, gd), BF16)] + ([jax.ShapeDtypeStruct((t, gd), F32)] if want_v else [])
    return pl.pallas_call(
        kern, grid=(t // rows,),
        in_specs=[pl.BlockSpec((rows, gd), lambda i: (i, 0)), pl.BlockSpec((rows, gd), lambda i: (i, 1)),
                  pl.BlockSpec((1, gd), lambda i: (0, 0)), pl.BlockSpec((1, gd), lambda i: (0, 0)),
                  pl.BlockSpec((groups, q, q), lambda i: (0, 0, 0)),
                  pl.BlockSpec((groups, q, 1), lambda i: (0, 0, 0))],
        out_specs=[row] * len(out_shape), out_shape=out_shape,
        compiler_params=_params("parallel"), name="gmlp",
    )(uv, uv, ln_g.reshape(1, gd), ln_b.reshape(1, gd), ws, bs)


def _merge_kernel(ya_ref, yb_ref, wa_ref, wb_ref, ga_ref, gb_ref, o_ref):
    a = _dot(ya_ref[...], wa_ref[...])
    b = _dot(yb_ref[...], wb_ref[...])
    o_ref[...] = (ga_ref[...] * a + gb_ref[...] * b).astype(o_ref.dtype)


def _merge_call(ya, yb, gates, w_a, w_b, layer):
    t, ka = ya.shape
    kb = yb.shape[1]
    d = w_a.shape[2]
    tm, tn = _pick(t, 1024), _pick(d, 512)
    nd = d // tn
    return pl.pallas_call(
        _merge_kernel, grid=(t // tm, nd),
        in_specs=[pl.BlockSpec((tm, ka), lambda i, j: (i, 0)), pl.BlockSpec((tm, kb), lambda i, j: (i, 0)),
                  pl.BlockSpec((None, ka, tn), lambda i, j: (layer, 0, j)),
                  pl.BlockSpec((None, kb, tn), lambda i, j: (layer, 0, j)),
                  pl.BlockSpec((tm, tn), lambda i, j: (i, j)), pl.BlockSpec((tm, tn), lambda i, j: (i, nd + j))],
        out_specs=pl.BlockSpec((tm, tn), lambda i, j: (i, j)),
        out_shape=jax.ShapeDtypeStruct((t, d), BF16),
        compiler_params=_params("parallel", "parallel"), name="merge",
    )(ya, yb, w_a, w_b, gates, gates)


def _mm_res_ln_kernel(a_ref, w_ref, x_ref, g_ref, b_ref, o32_ref, o16_ref, acc_ref, *, alpha, nk):
    k = pl.program_id(1)

    def finish(total):
        y = _ln_rows(alpha * x_ref[...] + total, g_ref[...], b_ref[...])
        o32_ref[...] = y
        o16_ref[...] = y.astype(BF16)

    if nk == 1:
        finish(_dot(a_ref[...], w_ref[...]))
        return

    @pl.when(k == 0)
    def _():
        acc_ref[...] = _dot(a_ref[...], w_ref[...])

    @pl.when((k > 0) & (k < nk - 1))
    def _():
        acc_ref[...] += _dot(a_ref[...], w_ref[...])

    @pl.when(k == nk - 1)
    def _():
        finish(acc_ref[...] + _dot(a_ref[...], w_ref[...]))


def _k_tile(k, cap):
    if k % LANE != 0:
        return k
    best = LANE
    for m in range(1, k // LANE + 1):
        tk = m * LANE
        if k % tk == 0 and tk <= cap:
            best = tk
    return best


def _mm_res_ln_call(a, w, layer, x, g, b, alpha, name):
    t, k = a.shape
    d = w.shape[2]
    tm = _pick(t, 512)
    tk = _k_tile(k, 2048)
    nk = k // tk
    row = pl.BlockSpec((tm, d), lambda i, kk: (i, 0))
    vec = pl.BlockSpec((1, d), lambda i, kk: (0, 0))
    kern = functools.partial(_mm_res_ln_kernel, alpha=alpha, nk=nk)
    return pl.pallas_call(
        kern, grid=(t // tm, nk),
        in_specs=[pl.BlockSpec((tm, tk), lambda i, kk: (i, kk)),
                  pl.BlockSpec((None, tk, d), lambda i, kk: (layer, kk, 0)), row, vec, vec],
        out_specs=[row, row],
        out_shape=[jax.ShapeDtypeStruct((t, d), F32), jax.ShapeDtypeStruct((t, d), BF16)],
        scratch_shapes=[pltpu.VMEM((tm, d), F32)],
        compiler_params=_params("parallel", "arbitrary"), name=name,
    )(a, w, x, g.reshape(1, d), b.reshape(1, d))


def _swiglu_up_kernel(meta_ref, x_ref, wg_ref, wu_ref, o_ref):
    live = pl.program_id(0) < meta_ref[0]

    @pl.when(live)
    def _():
        x = x_ref[...].astype(BF16)
        g = _dot(x, wg_ref[...])
        u = _dot(x, wu_ref[...])
        o_ref[...] = (_silu(g) * u).astype(o_ref.dtype)

    @pl.when(jnp.logical_not(live))
    def _():
        o_ref[...] = jnp.zeros_like(o_ref)


def _swiglu_up_call(meta, x, wg, wu, layer, tm, name):
    t, k = x.shape
    n = wg.shape[3]
    tn = _pick(n, 1024)
    nn = n // tn

    def live(i, m):
        return jnp.minimum(i, m[0] - 1)

    def col(i, j, m):
        return jnp.where(i < m[0], j, nn - 1)

    wspec = pl.BlockSpec((None, None, k, tn), lambda i, j, m: (layer, m[1 + live(i, m)], 0, col(i, j, m)))
    grid_spec = pltpu.PrefetchScalarGridSpec(
        num_scalar_prefetch=1, grid=(t // tm, nn),
        in_specs=[pl.BlockSpec((tm, k), lambda i, j, m: (live(i, m), 0)), wspec, wspec],
        out_specs=pl.BlockSpec((tm, tn), lambda i, j, m: (i, j)))
    return pl.pallas_call(
        _swiglu_up_kernel, grid_spec=grid_spec, out_shape=jax.ShapeDtypeStruct((t, n), BF16),
        compiler_params=_params("arbitrary", "arbitrary"), name=name,
    )(meta, x, wg, wu)


def _moe_down_kernel(meta_ref, h_ref, w_ref, o_ref, acc_ref, *, nk):
    k = pl.program_id(1)
    live = pl.program_id(0) < meta_ref[0]

    if nk == 1:
        @pl.when(live)
        def _():
            o_ref[...] = _dot(h_ref[...], w_ref[...])
    else:
        @pl.when(live & (k == 0))
        def _():
            acc_ref[...] = _dot(h_ref[...], w_ref[...])

        @pl.when(live & (k > 0) & (k < nk - 1))
        def _():
            acc_ref[...] += _dot(h_ref[...], w_ref[...])

        @pl.when(live & (k == nk - 1))
        def _():
            o_ref[...] = acc_ref[...] + _dot(h_ref[...], w_ref[...])

    @pl.when(jnp.logical_not(live) & (k == nk - 1))
    def _():
        o_ref[...] = jnp.zeros_like(o_ref)


def _moe_down_call(meta, h, wd, layer, tm):
    t, k = h.shape
    d = wd.shape[3]
    tk = _k_tile(k, 2048)
    nk = k // tk

    def live(i, m):
        return jnp.minimum(i, m[0] - 1)

    def kk(i, k_, m):
        return jnp.where(i < m[0], k_, nk - 1)

    grid_spec = pltpu.PrefetchScalarGridSpec(
        num_scalar_prefetch=1, grid=(t // tm, nk),
        in_specs=[pl.BlockSpec((tm, tk), lambda i, k_, m: (live(i, m), kk(i, k_, m))),
                  pl.BlockSpec((None, None, tk, d), lambda i, k_, m: (layer, m[1 + live(i, m)], kk(i, k_, m), 0))],
        out_specs=pl.BlockSpec((tm, d), lambda i, k_, m: (i, 0)),
        scratch_shapes=[pltpu.VMEM((tm, d), F32)])
    return pl.pallas_call(
        functools.partial(_moe_down_kernel, nk=nk), grid_spec=grid_spec,
        out_shape=jax.ShapeDtypeStruct((t, d), F32),
        compiler_params=_params("arbitrary", "arbitrary"), name="moe_down",
    )(meta, h, wd)


def _router_kernel(x_ref, wr_ref, e_ref, w_ref, r_ref, cnt_ref, carry_ref, *, n_exp, tm):
    @pl.when(pl.program_id(0) == 0)
    def _():
        carry_ref[...] = jnp.zeros_like(carry_ref)

    logits = _dot(x_ref[...], wr_ref[...])
    lane = lax.broadcasted_iota(I32, logits.shape, 1)
    lg = jnp.where(lane < n_exp, logits, -jnp.inf)
    m1 = jnp.max(lg, axis=-1, keepdims=True)
    i1 = jnp.min(jnp.where(lg == m1, lane, LANE), axis=-1, keepdims=True)
    lg2 = jnp.where(lane == i1, -jnp.inf, lg)
    m2 = jnp.max(lg2, axis=-1, keepdims=True)
    i2 = jnp.min(jnp.where(lg2 == m2, lane, LANE), axis=-1, keepdims=True)
    ex = jnp.exp(m2 - m1)
    w1 = 1.0 / (1.0 + ex)
    w2 = ex / (1.0 + ex)

    oh1 = (lane == i1).astype(F32)
    oh2 = (lane == i2).astype(F32)
    both = oh1 + oh2
    ri = lax.broadcasted_iota(I32, (tm, tm), 0)
    ci = lax.broadcasted_iota(I32, (tm, tm), 1)
    before = _dot((ri > ci).astype(BF16), both.astype(BF16)) + carry_ref[0:1, :]
    r1 = jnp.sum(before * oh1, axis=-1, keepdims=True)
    r2 = jnp.sum(before * oh2, axis=-1, keepdims=True)
    carry_ref[...] = carry_ref[...] + jnp.sum(both, axis=0, keepdims=True)

    e_ref[...] = jnp.where(lane == 0, i1, jnp.where(lane == 1, i2, 0))
    w_ref[...] = jnp.where(lane == 0, w1, jnp.where(lane == 1, w2, 0.0))
    r_ref[...] = jnp.where(lane == 0, r1, jnp.where(lane == 1, r2, 0.0)).astype(I32)
    cnt_ref[...] = carry_ref[...].astype(I32)


def _router_call(x, w_router):
    t, d = x.shape
    n_exp = w_router.shape[1]
    wr = jnp.zeros((d, LANE), BF16).at[:, :n_exp].set(w_router.astype(BF16))
    tm = _pick(t, 512)
    row = pl.BlockSpec((tm, LANE), lambda i: (i, 0))
    kern = functools.partial(_router_kernel, n_exp=n_exp, tm=tm)
    return pl.pallas_call(
        kern, grid=(t // tm,),
        in_specs=[pl.BlockSpec((tm, d), lambda i: (i, 0)), pl.BlockSpec((d, LANE), lambda i: (0, 0))],
        out_specs=[row, row, row, pl.BlockSpec((SUBLANE, LANE), lambda i: (0, 0))],
        out_shape=[jax.ShapeDtypeStruct((t, LANE), I32), jax.ShapeDtypeStruct((t, LANE), F32),
                   jax.ShapeDtypeStruct((t, LANE), I32), jax.ShapeDtypeStruct((SUBLANE, LANE), I32)],
        scratch_shapes=[pltpu.VMEM((SUBLANE, LANE), F32)],
        compiler_params=_params("arbitrary"), name="router",
    )(x, wr)


_DMA_UNROLL = 8
_DMA_PRIORITIES = 2


def _row_copy(src_hbm, dst_vmem, src_row, dst_row, sem):
    return pltpu.make_async_copy(src_hbm.at[pl.ds(src_row, 1)], dst_vmem.at[pl.ds(dst_row, 1)], sem)


def _gather_kernel(tok_ref, x_hbm, o_ref, sem, *, rows):
    def start(pair, carry):
        for prio in range(_DMA_PRIORITIES):
            r = pair * _DMA_PRIORITIES + prio
            _row_copy(x_hbm, o_ref, tok_ref[0, r], r, sem).start(priority=prio)
        return carry

    lax.fori_loop(0, rows // _DMA_PRIORITIES, start, 0, unroll=_DMA_UNROLL // _DMA_PRIORITIES)
    pltpu.make_async_copy(x_hbm.at[pl.ds(0, rows)], o_ref, sem).wait()


def _gather_call(x, tok, rows):
    cap = tok.shape[0]
    d = x.shape[1]
    tok3 = tok.reshape(cap // rows, 1, rows)
    return pl.pallas_call(
        functools.partial(_gather_kernel, rows=rows), grid=(cap // rows,),
        in_specs=[pl.BlockSpec((None, 1, rows), lambda i: (i, 0, 0), memory_space=pltpu.SMEM),
                  pl.BlockSpec(memory_space=pl.ANY)],
        out_specs=pl.BlockSpec((rows, d), lambda i: (i, 0)),
        out_shape=jax.ShapeDtypeStruct((cap, d), x.dtype),
        scratch_shapes=[pltpu.SemaphoreType.DMA(())],
        compiler_params=_params("arbitrary"), name="moe_gather",
    )(tok3, x)


def _combine_ln_kernel(d0_ref, d1_ref, o_hbm, w_ref, x_ref, g_ref, b_ref, o32_ref, o16_ref, buf0, buf1, sem,
                       *, alpha, rows):
    def start(r, carry):
        _row_copy(o_hbm, buf0, d0_ref[0, r], r, sem).start(priority=0)
        _row_copy(o_hbm, buf1, d1_ref[0, r], r, sem).start(priority=1)
        return carry

    lax.fori_loop(0, rows, start, 0, unroll=_DMA_UNROLL)
    pltpu.make_async_copy(o_hbm.at[pl.ds(0, rows)], buf0, sem).wait()
    pltpu.make_async_copy(o_hbm.at[pl.ds(0, rows)], buf1, sem).wait()
    wts = w_ref[...]
    ffn = wts[:, 0:1] * buf0[...] + wts[:, 1:2] * buf1[...]
    y = _ln_rows(alpha * x_ref[...] + ffn, g_ref[...], b_ref[...])
    o32_ref[...] = y
    o16_ref[...] = y.astype(BF16)


def _combine_ln_call(dest, o_sorted, wts, x, g, b, alpha):
    t, d = x.shape
    rows = _pick(t, 512)
    d0 = dest[:, 0].reshape(t // rows, 1, rows)
    d1 = dest[:, 1].reshape(t // rows, 1, rows)
    idx = pl.BlockSpec((None, 1, rows), lambda i: (i, 0, 0), memory_space=pltpu.SMEM)
    row = pl.BlockSpec((rows, d), lambda i: (i, 0))
    vec = pl.BlockSpec((1, d), lambda i: (0, 0))
    kern = functools.partial(_combine_ln_kernel, alpha=alpha, rows=rows)
    return pl.pallas_call(
        kern, grid=(t // rows,),
        in_specs=[idx, idx, pl.BlockSpec(memory_space=pl.ANY), pl.BlockSpec((rows, LANE), lambda i: (i, 0)),
                  row, vec, vec],
        out_specs=[row, row],
        out_shape=[jax.ShapeDtypeStruct((t, d), F32), jax.ShapeDtypeStruct((t, d), BF16)],
        scratch_shapes=[pltpu.VMEM((rows, d), F32), pltpu.VMEM((rows, d), F32), pltpu.SemaphoreType.DMA(())],
        compiler_params=_params("arbitrary"), name="moe_combine_ln",
    )(d0, d1, o_sorted, wts, x, g.reshape(1, d), b.reshape(1, d))


def _moe_layer(xf, xb, w_router, wg, wu, wd, layer, ln_g, ln_b, alpha):
    t, d = xf.shape
    n_exp = wg.shape[1]
    e_out, w_out, r_out, cnt = _router_call(xb, w_router)
    e = e_out[:, :MOE_TOP_K]
    counts = cnt[0, :n_exp]

    tm = 512 if t * MOE_TOP_K >= 8 * 512 else 128
    n_tiles = -(-(t * MOE_TOP_K) // tm) + n_exp
    cap = n_tiles * tm
    padded = (counts + tm - 1) // tm * tm
    pad_end = jnp.cumsum(padded)
    pad_start = pad_end - padded
    dest = (pad_start[e] + r_out[:, :MOE_TOP_K]).astype(I32)
    tok = jnp.repeat(jnp.arange(t, dtype=I32), MOE_TOP_K)
    slot_tok = jnp.zeros((cap,), I32).at[dest.reshape(-1)].set(tok)
    tile_expert = jnp.minimum(jnp.searchsorted(pad_end, jnp.arange(n_tiles, dtype=I32) * tm, side="right"),
                              n_exp - 1)
    meta = jnp.concatenate([(pad_end[-1:] // tm), tile_expert]).astype(I32)

    xs = _gather_call(xf, slot_tok, _pick(cap, 1024))
    h = _swiglu_up_call(meta, xs, wg, wu, layer, tm, "moe_up")
    o_sorted = _moe_down_call(meta, h, wd, layer, tm)
    return _combine_ln_call(dest, o_sorted, w_out, xf, ln_g, ln_b, alpha)


def _trunk(x, conv_bufs, ssm_states, pr, want_v):
    bsz, seq, d = x.shape
    t = bsz * seq
    depth = pr["w_zx"].shape[0]
    alpha = float((2 * depth) ** 0.25)
    d_inner = pr["w_br_a"].shape[1]
    cdim = pr["conv_w"].shape[2]
    gd = pr["w_br_b"].shape[1]
    kc = pr["conv_w"].shape[1]
    off_xbc = d_inner
    off_gate = 2 * gd
    q_ssd = _pick(seq, 128)
    n_groups = (cdim - d_inner) // (2 * ssm_states.shape[-1])
    gs_ssd = next(c for c in (4, 2, 1) if n_groups % c == 0)
    q_gmlp = min(pr["w_s"].shape[2], seq)

    xf, xb = _ln_call(x.reshape(t, d), pr["ln_in_g"], pr["ln_in_b"])
    convs, ssms, vs = [], [], []
    w_main, w_ug = pr["w_zx"], pr["w_ug"]
    lane_aligned = q_ssd % LANE == 0
    for i in range(depth):
        sz = _proj_call(xb, w_main, i, 0, d_inner, "silu", BF16, "in_proj_z")
        uv = _proj_call(xb, w_ug, i, 0, 2 * gd, "gelu", F32, "in_proj_uv")
        gates = _proj_call(xb, w_ug, i, off_gate, 2 * d, "none", BF16, "in_proj_gate", bias=pr["b_gate"][i])
        dt, acum = _dt_call(xb, pr["w_dt"][i], pr["dt_bias"][i], pr["a_log_pad"][i], q_ssd)
        if lane_aligned:
            xbc, tails = _proj_conv_call(xb, w_main, i, off_xbc, pr["conv_w"][i], pr["conv_b"][i], conv_bufs[i], seq)
            ya, ssm_i = _ssd_cm_call(xbc, sz, dt, acum, ssm_states[i], pr["d_skip"][i], pr["ssd_norm_g"][i],
                                     bsz=bsz, seq=seq, q=q_ssd, gs=gs_ssd)
            new_rows = tails.reshape(bsz, -1, _CONV_PAD, cdim)[:, -1, _CONV_PAD - (kc - 1):]
        else:
            xbc = _proj_call(xb, w_main, i, off_xbc, cdim, "none", F32, "in_proj_xbc")
            ya, ssm_i = _ssd_call(xbc, sz, dt, conv_bufs[i], ssm_states[i], pr["conv_w"][i], pr["conv_b"][i],
                                  pr["a_log"][i], pr["d_skip"][i], pr["ssd_norm_g"][i], bsz=bsz, seq=seq, q=q_ssd)
            new_rows = xbc.reshape(bsz, seq, cdim)[:, max(seq - (kc - 1), 0):]
        conv_i = jnp.concatenate([conv_bufs[i], new_rows], axis=1)[:, -(kc - 1):]
        yb, *vn = _gmlp_call(uv, pr["gmlp_ln_g"][i], pr["gmlp_ln_b"][i], pr["w_s"][i], pr["b_s"][i],
                             bsz=bsz, seq=seq, q=q_gmlp, want_v=want_v)
        merged = _merge_call(ya, yb, gates, pr["w_br_a"], pr["w_br_b"], i)
        xf, xb = _mm_res_ln_call(merged, pr["w_o"], i, xf, pr["ln1_g"][i], pr["ln1_b"][i], alpha, "out_proj_ln")
        j = i // 2
        if i % 2 == 0:
            tm = _pick(t, 1024)
            meta = jnp.concatenate([jnp.full((1,), t // tm, I32), jnp.zeros((t // tm,), I32)])
            h = _swiglu_up_call(meta, xb, pr["w_ff_gate"], pr["w_ff_up"], j, tm, "ffn_up")
            xf, xb = _mm_res_ln_call(h, pr["w_ff_down"], j, xf, pr["ln2_g"][i], pr["ln2_b"][i], alpha,
                                     "ffn_down_ln")
        else:
            xf, xb = _moe_layer(xf, xb, pr["w_router"][j], pr["w_moe_gate"], pr["w_moe_up"], pr["w_moe_down"], j,
                                pr["ln2_g"][i], pr["ln2_b"][i], alpha)
        convs.append(conv_i)
        ssms.append(ssm_i)
        vs.extend(v.reshape(bsz, seq, gd) for v in vn)
    return xf.reshape(bsz, seq, d), jnp.stack(convs), jnp.stack(ssms), (jnp.stack(vs) if want_v else None)


def kernel(x_prompt, x_sample, cache_conv, state_ssm, ln_in_g, ln_in_b, w_in, conv_w, conv_b, dt_bias, a_log,
           d_skip, ssd_norm_g, gmlp_ln_g, gmlp_ln_b, w_s, b_s, b_gate, w_br_a, w_br_b, w_o, ln1_g, ln1_b,
           w_ff_gate, w_ff_up, w_ff_down, w_router, w_moe_gate, w_moe_up, w_moe_down, ln2_g, ln2_b):
    depth, d_model, _ = w_in.shape
    d_inner = w_br_a.shape[1]
    cdim = conv_w.shape[2]
    heads = a_log.shape[1]
    o_dt = d_inner + cdim
    w_in16 = w_in.astype(BF16)
    w_ug = w_in16[:, :, o_dt + heads:]
    w_dt = jnp.zeros((depth, d_model, LANE), BF16).at[:, :, :heads].set(w_in16[:, :, o_dt:o_dt + heads])
    dt_b = jnp.zeros((depth, 1, LANE), F32).at[:, 0, :heads].set(dt_bias)
    al_pad = jnp.zeros((depth, 1, LANE), F32).at[:, 0, :heads].set(a_log)
    pr = dict(a_log_pad=al_pad,
        ln_in_g=ln_in_g, ln_in_b=ln_in_b, w_zx=w_in16, w_ug=w_ug, w_dt=w_dt, dt_bias=dt_b, conv_w=conv_w, conv_b=conv_b,
        a_log=a_log, d_skip=d_skip, ssd_norm_g=ssd_norm_g, gmlp_ln_g=gmlp_ln_g, gmlp_ln_b=gmlp_ln_b, w_s=w_s,
        b_s=b_s, b_gate=b_gate, w_br_a=w_br_a.astype(BF16), w_br_b=w_br_b.astype(BF16), w_o=w_o.astype(BF16),
        ln1_g=ln1_g, ln1_b=ln1_b, w_ff_gate=w_ff_gate.astype(BF16)[:, None], w_ff_up=w_ff_up.astype(BF16)[:, None],
        w_ff_down=w_ff_down.astype(BF16), w_router=w_router, w_moe_gate=w_moe_gate.astype(BF16),
        w_moe_up=w_moe_up.astype(BF16), w_moe_down=w_moe_down.astype(BF16), ln2_g=ln2_g, ln2_b=ln2_b)
    bp = x_prompt.shape[0]
    zero_conv = jnp.zeros((depth, bp) + cache_conv.shape[2:], x_prompt.dtype)
    zero_ssm = jnp.zeros((depth, bp) + state_ssm.shape[2:], state_ssm.dtype)
    y_prompt, prompt_conv, prompt_ssm, _ = _trunk(x_prompt, zero_conv, zero_ssm, pr, want_v=False)
    y_sample, sample_conv, sample_ssm, sample_v = _trunk(x_sample, cache_conv, state_ssm, pr, want_v=True)
    return (y_prompt, y_sample, prompt_conv, prompt_ssm, sample_conv, sample_ssm, sample_v)
```

```python
import functools

import jax
import jax.numpy as jnp
from jax import lax
from jax.experimental import pallas as pl
from jax.experimental.pallas import tpu as pltpu

F32 = jnp.float32
BF16 = jnp.bfloat16
I32 = jnp.int32
HIGHEST = lax.Precision.HIGHEST

LN_EPS = 1e-5
LANE = 128
SUBLANE = 8
V7X_VMEM_BYTES = 64 * 1024 * 1024
VMEM_LIMIT_BYTES = V7X_VMEM_BYTES - 8 * 1024 * 1024
MOE_TOP_K = 2
_NEG_BIG = -1e30
_TILE_CANDIDATES = (1024, 512, 256, 128, 64, 32, 16, 8)


def _pick(n, cap):
    for c in _TILE_CANDIDATES:
        if c <= cap and n % c == 0:
            return c
    raise ValueError(f"no tile for {n}")


def _params(*sem):
    return pltpu.CompilerParams(dimension_semantics=sem, vmem_limit_bytes=VMEM_LIMIT_BYTES)


def _ln_rows(x, g, b):
    mu = jnp.mean(x, axis=-1, keepdims=True)
    xc = x - mu
    var = jnp.mean(xc * xc, axis=-1, keepdims=True)
    return xc * lax.rsqrt(var + LN_EPS) * g + b


def _sigmoid(x):
    return 0.5 * jnp.tanh(0.5 * x) + 0.5


def _silu(x):
    return x * _sigmoid(x)


def _gelu(x):
    return 0.5 * x * (1.0 + lax.erf(x * (2.0 ** -0.5)))


def _dot(a, b):
    return jnp.dot(a, b, preferred_element_type=F32)


def _dot_exact(a, b):
    return jnp.dot(a, b, precision=HIGHEST, preferred_element_type=F32)


def _ln_kernel(x_ref, g_ref, b_ref, o32_ref, o16_ref):
    y = _ln_rows(x_ref[...], g_ref[...], b_ref[...])
    o32_ref[...] = y
    o16_ref[...] = y.astype(BF16)


def _ln_call(x, g, b):
    t, d = x.shape
    tm = _pick(t, 512)
    row = pl.BlockSpec((tm, d), lambda i: (i, 0))
    vec = pl.BlockSpec((1, d), lambda i: (0, 0))
    return pl.pallas_call(
        _ln_kernel, grid=(t // tm,), in_specs=[row, vec, vec], out_specs=[row, row],
        out_shape=[jax.ShapeDtypeStruct((t, d), F32), jax.ShapeDtypeStruct((t, d), BF16)],
        compiler_params=_params("parallel"), name="ln_in",
    )(x, g.reshape(1, d), b.reshape(1, d))


_CONV_PAD = SUBLANE

_ACTIVATIONS = {"none": lambda r: r, "silu": _silu, "gelu": _gelu}


def _proj_kernel(x_ref, w_ref, o_ref, *, act):
    o_ref[...] = _ACTIVATIONS[act](_dot(x_ref[...], w_ref[...])).astype(o_ref.dtype)


def _proj_gate_kernel(x_ref, w_ref, b_ref, o_ref):
    o_ref[...] = _sigmoid(_dot(x_ref[...], w_ref[...]) + b_ref[...]).astype(o_ref.dtype)


def _proj_conv_kernel(x_ref, w_ref, cw_ref, cc_ref, cbuf_ref, o_ref, tail_ref, halo_ref, *, kc, tiles_per_seq):
    i, j = pl.program_id(0), pl.program_id(1)
    tm = x_ref.shape[0]
    r = _dot(x_ref[...], w_ref[...])
    before = jnp.where(lax.rem(i, tiles_per_seq) == 0, cbuf_ref[...], halo_ref[j])
    nb = tm // _CONV_PAD
    blocks = r.reshape(nb, _CONV_PAD, r.shape[1])
    prev_blocks = jnp.concatenate([before[None], blocks[:nb - 1]], axis=0)
    sub = lax.broadcasted_iota(I32, blocks.shape, 1)
    acc = cc_ref[...] + r * cw_ref[kc - 1:kc, :]
    for s in range(1, kc):
        shifted = jnp.where(sub < s, pltpu.roll(prev_blocks, s, 1), pltpu.roll(blocks, s, 1))
        acc = acc + shifted.reshape(r.shape) * cw_ref[kc - 1 - s:kc - s, :]
    o_ref[...] = _silu(acc)
    last_rows = r[tm - _CONV_PAD:tm, :]
    tail_ref[...] = last_rows
    halo_ref[j] = last_rows


def _region_tiles(t, n, col0):
    tm = _pick(t, 1024)
    tn = next(c for c in _TILE_CANDIDATES if n % c == 0 and col0 % c == 0)
    return tm, tn


def _proj_call(x, w, layer, col0, n, act, out_dtype, name, bias=None):
    t, k = x.shape
    tm, tn = _region_tiles(t, n, col0)
    c0 = col0 // tn
    in_specs = [pl.BlockSpec((tm, k), lambda i, j: (i, 0)), pl.BlockSpec((None, k, tn), lambda i, j: (layer, 0, c0 + j))]
    args = [x, w]
    if bias is None:
        kern = functools.partial(_proj_kernel, act=act)
    else:
        kern = _proj_gate_kernel
        in_specs.append(pl.BlockSpec((1, tn), lambda i, j: (0, j)))
        args.append(bias.reshape(1, n))
    return pl.pallas_call(
        kern, grid=(t // tm, n // tn), in_specs=in_specs,
        out_specs=pl.BlockSpec((tm, tn), lambda i, j: (i, j)),
        out_shape=jax.ShapeDtypeStruct((t, n), out_dtype),
        compiler_params=_params("parallel", "parallel"), name=name,
    )(*args)


def _proj_conv_call(x, w, layer, col0, conv_w, conv_b, conv_buf, seq):
    t, k = x.shape
    kc, n = conv_w.shape
    tm, tn = _pick(seq, 1024), _region_tiles(t, n, col0)[1]
    assert t % tm == 0 and kc - 1 <= _CONV_PAD <= tm
    tiles_per_seq = seq // tm
    c0 = col0 // tn
    cbuf = jnp.pad(conv_buf, ((0, 0), (_CONV_PAD - (kc - 1), 0), (0, 0)))
    kern = functools.partial(_proj_conv_kernel, kc=kc, tiles_per_seq=tiles_per_seq)
    return pl.pallas_call(
        kern, grid=(t // tm, n // tn),
        in_specs=[pl.BlockSpec((tm, k), lambda i, j: (i, 0)),
                  pl.BlockSpec((None, k, tn), lambda i, j: (layer, 0, c0 + j)),
                  pl.BlockSpec((kc, tn), lambda i, j: (0, j)),
                  pl.BlockSpec((1, tn), lambda i, j: (0, j)),
                  pl.BlockSpec((None, _CONV_PAD, tn), lambda i, j: (i // tiles_per_seq, 0, j))],
        out_specs=[pl.BlockSpec((tm, tn), lambda i, j: (i, j)),
                   pl.BlockSpec((None, _CONV_PAD, tn), lambda i, j: (i, 0, j))],
        out_shape=[jax.ShapeDtypeStruct((t, n), F32), jax.ShapeDtypeStruct((t // tm, _CONV_PAD, n), F32)],
        scratch_shapes=[pltpu.VMEM((n // tn, _CONV_PAD, tn), F32)],
        compiler_params=_params("arbitrary", "arbitrary"), name="in_proj_conv",
    )(x, w, conv_w, conv_b.reshape(1, n), cbuf)


def _dt_kernel(x_ref, w_ref, b_ref, al_ref, dt_ref, ac_ref, *, q):
    raw = _dot(x_ref[...], w_ref[...]) + b_ref[...]
    dt = jnp.maximum(raw, 0.0) + jnp.log1p(jnp.exp(-jnp.abs(raw)))
    dt_ref[...] = dt
    a = dt * (-jnp.exp(al_ref[...]))
    ri = lax.broadcasted_iota(I32, (q, q), 0)
    ci = lax.broadcasted_iota(I32, (q, q), 1)
    tril = (ri >= ci).astype(F32)
    for c in range(a.shape[0] // q):
        ac_ref[c * q:(c + 1) * q, :] = _dot_exact(tril, a[c * q:(c + 1) * q, :])


def _dt_call(x, w, bias, a_log, q):
    t, k = x.shape
    n = w.shape[1]
    tm = max(_pick(t, 512), q)
    assert tm % q == 0 and t % tm == 0
    row = pl.BlockSpec((tm, n), lambda i: (i, 0))
    vec = pl.BlockSpec((1, n), lambda i: (0, 0))
    return pl.pallas_call(
        functools.partial(_dt_kernel, q=q), grid=(t // tm,),
        in_specs=[pl.BlockSpec((tm, k), lambda i: (i, 0)), pl.BlockSpec((k, n), lambda i: (0, 0)), vec, vec],
        out_specs=[row, row],
        out_shape=[jax.ShapeDtypeStruct((t, n), F32), jax.ShapeDtypeStruct((t, n), F32)],
        compiler_params=_params("parallel"), name="dt_proj",
    )(x, w, bias, a_log)


def _ssd_kernel(xs_ref, bm_ref, cm_ref, sz_ref, dt_ref, dtt_ref, cbx_ref, cbb_ref, cbc_ref, s0_ref,
                cwx_ref, cwb_ref, cwc_ref, ccx_ref, ccb_ref, ccc_ref, alr_ref, alc_ref, dsk_ref, ng_ref,
                y_ref, sout_ref, xpad_ref, st_ref, *, q, hpg, p, n, kc, nc):
    c = pl.program_id(2)
    w = hpg * p
    tail = kc - 1
    t0 = _CONV_PAD - tail

    @pl.when(c == 0)
    def _():
        xpad_ref[t0:_CONV_PAD, 0:w] = cbx_ref[...]
        xpad_ref[t0:_CONV_PAD, w:w + n] = cbb_ref[...]
        xpad_ref[t0:_CONV_PAD, w + n:w + 2 * n] = cbc_ref[...]
        st_ref[...] = s0_ref[...].reshape(w, n).T

    xpad_ref[_CONV_PAD:_CONV_PAD + q, 0:w] = xs_ref[...]
    xpad_ref[_CONV_PAD:_CONV_PAD + q, w:w + n] = bm_ref[...]
    xpad_ref[_CONV_PAD:_CONV_PAD + q, w + n:w + 2 * n] = cm_ref[...]

    def conv(lo, hi, cw_ref, cc_ref):
        acc = cc_ref[...]
        for k in range(kc):
            acc = acc + xpad_ref[t0 + k:t0 + k + q, lo:hi] * cw_ref[k:k + 1, :]
        return _silu(acc)

    xs = conv(0, w, cwx_ref, ccx_ref)
    bm = conv(w, w + n, cwb_ref, ccb_ref)
    cm = conv(w + n, w + 2 * n, cwc_ref, ccc_ref)
    xpad_ref[t0:_CONV_PAD, :] = xpad_ref[t0 + q:_CONV_PAD + q, :]

    a = dt_ref[...] * (-jnp.exp(alr_ref[...]))
    a_t = dtt_ref[...] * (-jnp.exp(alc_ref[...]))
    ri = lax.broadcasted_iota(I32, (q, q), 0)
    ci = lax.broadcasted_iota(I32, (q, q), 1)
    tril = ri >= ci
    acol = _dot_exact(tril.astype(F32), a)
    arow = _dot_exact(a_t, (ri <= ci).astype(F32))

    hrow = lax.broadcasted_iota(I32, (hpg, w), 0)
    hlane = lax.broadcasted_iota(I32, (hpg, w), 1)
    expand = ((hlane >= hrow * p) & (hlane < (hrow + 1) * p)).astype(F32)
    xdt = xs * _dot_exact(dt_ref[...], expand)
    eacol = _dot_exact(jnp.exp(acol), expand)
    to_end = _dot_exact(jnp.exp(acol[q - 1:q, :] - acol), expand)

    bm16 = bm.astype(BF16)
    cm16 = cm.astype(BF16)
    cb = lax.dot_general(cm16, bm16, (((1,), (1,)), ((), ())), preferred_element_type=F32)
    st = st_ref[...]
    y = _dot(cm16, st.astype(BF16)) * eacol
    lane = lax.broadcasted_iota(I32, (q, w), 1)
    for h in range(hpg):
        seg = acol[:, h:h + 1] - arow[h:h + 1, :]
        m = (cb * jnp.exp(jnp.where(tril, seg, _NEG_BIG))).astype(BF16)
        head = (lane >= h * p) & (lane < (h + 1) * p)
        y = y + _dot(m, jnp.where(head, xdt, 0.0).astype(BF16))
    y = y + dsk_ref[...] * xs
    hz = y * sz_ref[...]
    y_ref[...] = (hz * lax.rsqrt(jnp.mean(hz * hz, axis=-1, keepdims=True) + LN_EPS) * ng_ref[...]).astype(BF16)

    upd = lax.dot_general(bm16, (xdt * to_end).astype(BF16), (((0,), (0,)), ((), ())),
                          preferred_element_type=F32)
    st_ref[...] = st * eacol[q - 1:q, :] + upd

    @pl.when(c == nc - 1)
    def _():
        sout_ref[...] = st_ref[...].T.reshape(hpg, p, n)


def _ssd_call(xbc, sz, dt, conv_buf, state0, conv_w, conv_b, a_log, d_skip, norm_g, *, bsz, seq, q):
    heads, p, n = state0.shape[1:]
    kc, cdim = conv_w.shape
    d_inner = sz.shape[1]
    groups = (cdim - d_inner) // (2 * n)
    hpg = heads // groups
    w = hpg * p
    nc = seq // q
    t = bsz * seq
    assert w * groups == d_inner and seq % q == 0 and q >= kc - 1 and d_inner % n == 0

    dt4 = dt[:, :heads].reshape(bsz, seq, groups, hpg).transpose(0, 2, 1, 3)
    dtt4 = dt4.transpose(0, 1, 3, 2)
    alr = a_log.reshape(groups, 1, hpg)
    alc = a_log.reshape(groups, hpg, 1)
    dsk = jnp.repeat(d_skip, p).reshape(1, d_inner)
    ng = norm_g.reshape(1, d_inner)
    ccb = conv_b.reshape(1, cdim)

    bb, cb_ = d_inner // n, d_inner // n + groups
    row = lambda b, g, c: b * nc + c
    in_specs = [
        pl.BlockSpec((q, w), lambda b, g, c: (row(b, g, c), g)),
        pl.BlockSpec((q, n), lambda b, g, c: (row(b, g, c), bb + g)),
        pl.BlockSpec((q, n), lambda b, g, c: (row(b, g, c), cb_ + g)),
        pl.BlockSpec((q, w), lambda b, g, c: (row(b, g, c), g)),
        pl.BlockSpec((None, None, q, hpg), lambda b, g, c: (b, g, c, 0)),
        pl.BlockSpec((None, None, hpg, q), lambda b, g, c: (b, g, 0, c)),
        pl.BlockSpec((None, kc - 1, w), lambda b, g, c: (b, 0, g)),
        pl.BlockSpec((None, kc - 1, n), lambda b, g, c: (b, 0, d_inner // n + g)),
        pl.BlockSpec((None, kc - 1, n), lambda b, g, c: (b, 0, d_inner // n + groups + g)),
        pl.BlockSpec((None, hpg, p, n), lambda b, g, c: (b, g, 0, 0)),
        pl.BlockSpec((kc, w), lambda b, g, c: (0, g)),
        pl.BlockSpec((kc, n), lambda b, g, c: (0, d_inner // n + g)),
        pl.BlockSpec((kc, n), lambda b, g, c: (0, d_inner // n + groups + g)),
        pl.BlockSpec((1, w), lambda b, g, c: (0, g)),
        pl.BlockSpec((1, n), lambda b, g, c: (0, d_inner // n + g)),
        pl.BlockSpec((1, n), lambda b, g, c: (0, d_inner // n + groups + g)),
        pl.BlockSpec((None, 1, hpg), lambda b, g, c: (g, 0, 0)),
        pl.BlockSpec((None, hpg, 1), lambda b, g, c: (g, 0, 0)),
        pl.BlockSpec((1, w), lambda b, g, c: (0, g)),
        pl.BlockSpec((1, w), lambda b, g, c: (0, g)),
    ]
    out_specs = [
        pl.BlockSpec((q, w), lambda b, g, c: (row(b, g, c), g)),
        pl.BlockSpec((None, hpg, p, n), lambda b, g, c: (b, g, 0, 0)),
    ]
    kern = functools.partial(_ssd_kernel, q=q, hpg=hpg, p=p, n=n, kc=kc, nc=nc)
    return pl.pallas_call(
        kern, grid=(bsz, groups, nc), in_specs=in_specs, out_specs=out_specs,
        out_shape=[jax.ShapeDtypeStruct((t, d_inner), BF16), jax.ShapeDtypeStruct(state0.shape, F32)],
        scratch_shapes=[pltpu.VMEM((_CONV_PAD + q, w + 2 * n), F32), pltpu.VMEM((n, w), F32)],
        compiler_params=_params("parallel", "parallel", "arbitrary"), name="ssd",
    )(xbc, xbc, xbc, sz, dt4, dtt4, conv_buf, conv_buf, conv_buf, state0,
      conv_w, conv_w, conv_w, ccb, ccb, ccb, alr, alc, dsk, ng)


def _ssd_cm_kernel(xs_ref, bm_ref, cm_ref, sz_ref, dtt_ref, act_ref, ac_ref, s0_ref, dsk_ref, ng_ref,
                   y_ref, sout_ref, st_ref, *, q, gs, hpg, p, n, nc):
    c = pl.program_id(2)
    w = hpg * p

    @pl.when(c == 0)
    def _():
        st_ref[...] = s0_ref[...].reshape(gs * w, n)

    def rows(v):
        return jnp.concatenate([jnp.broadcast_to(v[h:h + 1, :], (p, v.shape[1])) for h in range(hpg)], axis=0)

    si = lax.broadcasted_iota(I32, (q, q), 0)
    li = lax.broadcasted_iota(I32, (q, q), 1)
    keep = si <= li
    nt = (((1,), (1,)), ((), ()))
    for g in range(gs):
        xs = xs_ref[:, g * w:(g + 1) * w]
        bm16 = bm_ref[:, g * n:(g + 1) * n].astype(BF16)
        cm16 = cm_ref[:, g * n:(g + 1) * n].astype(BF16)
        arow = act_ref[g]
        acol = ac_ref[g]
        ear = jnp.exp(arow)
        te = jnp.exp(arow[:, q - 1:q] - arow)
        xdt_t = xs.T * rows(dtt_ref[g])
        xdt16 = xdt_t.astype(BF16)
        cb_t = lax.dot_general(bm16, cm16, nt, preferred_element_type=F32)
        ys = []
        for h in range(hpg):
            seg = arow[h:h + 1, :] - acol[:, h:h + 1]
            m_t = (cb_t * jnp.exp(jnp.where(keep, seg, _NEG_BIG))).astype(BF16)
            ys.append(_dot(xdt16[h * p:(h + 1) * p, :], m_t))
        st = st_ref[g * w:(g + 1) * w, :]
        y_t = jnp.concatenate(ys, axis=0)
        y_t = y_t + lax.dot_general(st.astype(BF16), cm16, nt, preferred_element_type=F32) * rows(ear)
        y = y_t.T + dsk_ref[:, g * w:(g + 1) * w] * xs
        hz = y * sz_ref[:, g * w:(g + 1) * w]
        y_ref[:, g * w:(g + 1) * w] = (hz * lax.rsqrt(jnp.mean(hz * hz, axis=-1, keepdims=True) + LN_EPS)
                                       * ng_ref[:, g * w:(g + 1) * w]).astype(BF16)
        upd = _dot((xdt_t * rows(te)).astype(BF16), bm16)
        st_ref[g * w:(g + 1) * w, :] = st * rows(ear[:, q - 1:q]) + upd

    @pl.when(c == nc - 1)
    def _():
        sout_ref[...] = st_ref[...].reshape(gs * hpg, p, n)


def _ssd_cm_call(xbc, sz, dt, acum, state0, d_skip, norm_g, *, bsz, seq, q, gs):
    heads, p, n = state0.shape[1:]
    cdim = xbc.shape[1]
    d_inner = sz.shape[1]
    groups = (cdim - d_inner) // (2 * n)
    hpg = heads // groups
    w = hpg * p
    nc = seq // q
    t = bsz * seq
    gw, gn = gs * w, gs * n
    assert w * groups == d_inner and seq % q == 0 and q % LANE == 0 and groups % gs == 0
    assert d_inner % gn == 0 and (groups * n) % gn == 0

    def heads_major(v):
        return v[:, :heads].reshape(bsz, seq, groups, hpg).transpose(0, 2, 3, 1)

    dtt4 = heads_major(dt)
    act4 = heads_major(acum)
    ac4 = act4.transpose(0, 1, 3, 2)
    dsk = jnp.repeat(d_skip, p).reshape(1, d_inner)
    ng = norm_g.reshape(1, d_inner)

    bb, cb_ = d_inner // gn, (d_inner + groups * n) // gn
    row = lambda b, g, c: b * nc + c
    in_specs = [
        pl.BlockSpec((q, gw), lambda b, g, c: (row(b, g, c), g)),
        pl.BlockSpec((q, gn), lambda b, g, c: (row(b, g, c), bb + g)),
        pl.BlockSpec((q, gn), lambda b, g, c: (row(b, g, c), cb_ + g)),
        pl.BlockSpec((q, gw), lambda b, g, c: (row(b, g, c), g)),
        pl.BlockSpec((None, gs, hpg, q), lambda b, g, c: (b, g, 0, c)),
        pl.BlockSpec((None, gs, hpg, q), lambda b, g, c: (b, g, 0, c)),
        pl.BlockSpec((None, gs, q, hpg), lambda b, g, c: (b, g, c, 0)),
        pl.BlockSpec((None, gs * hpg, p, n), lambda b, g, c: (b, g, 0, 0)),
        pl.BlockSpec((1, gw), lambda b, g, c: (0, g)),
        pl.BlockSpec((1, gw), lambda b, g, c: (0, g)),
    ]
    out_specs = [
        pl.BlockSpec((q, gw), lambda b, g, c: (row(b, g, c), g)),
        pl.BlockSpec((None, gs * hpg, p, n), lambda b, g, c: (b, g, 0, 0)),
    ]
    kern = functools.partial(_ssd_cm_kernel, q=q, gs=gs, hpg=hpg, p=p, n=n, nc=nc)
    return pl.pallas_call(
        kern, grid=(bsz, groups // gs, nc), in_specs=in_specs, out_specs=out_specs,
        out_shape=[jax.ShapeDtypeStruct((t, d_inner), BF16), jax.ShapeDtypeStruct(state0.shape, F32)],
        scratch_shapes=[pltpu.VMEM((gs * w, n), F32)],
        compiler_params=_params("parallel", "parallel", "arbitrary"), name="ssd_cm",
    )(xbc, xbc, xbc, sz, dtt4, act4, ac4, state0, dsk, ng)


_GMLP_CHUNKS_PER_STEP = (4, 2, 1)


def _gmlp_kernel(u_ref, v_ref, lg_ref, lb_ref, ws_ref, bs_ref, yb_ref, *vn_refs, groups, q):
    u = u_ref[...]
    vn = _ln_rows(v_ref[...], lg_ref[...], lb_ref[...])
    for vn_ref in vn_refs:
        vn_ref[...] = vn
    vn16 = vn.astype(BF16)
    d = vn.shape[1] // groups
    ri = lax.broadcasted_iota(I32, (q, q), 0)
    ci = lax.broadcasted_iota(I32, (q, q), 1)
    for g in range(groups):
        wg = jnp.where(ri >= ci, ws_ref[g], 0.0).astype(BF16)
        for c in range(vn.shape[0] // q):
            rows = slice(c * q, (c + 1) * q)
            s = _dot(wg, vn16[rows, g * d:(g + 1) * d]) + bs_ref[g]
            yb_ref[rows, g * d:(g + 1) * d] = (u[rows, g * d:(g + 1) * d] * s).astype(BF16)


def _gmlp_call(uv, ln_g, ln_b, w_s, b_s, *, bsz, seq, q, want_v):
    groups = w_s.shape[0]
    t = bsz * seq
    gd = uv.shape[1] // 2
    assert seq % q == 0 and (gd // groups) % LANE == 0
    ws = w_s[:, :q, :q]
    bs = b_s[:, :q, None]
    rows = next(c * q for c in _GMLP_CHUNKS_PER_STEP if seq % (c * q) == 0)
    row = pl.BlockSpec((rows, gd), lambda i: (i, 0))
    kern = functools.partial(_gmlp_kernel, groups=groups, q=q)
    out_shape = [jax.ShapeDtypeStruct((t, gd), BF16)] + ([jax.ShapeDtypeStruct((t, gd), F32)] if want_v else [])
    return pl.pallas_call(
        kern, grid=(t // rows,),
        in_specs=[pl.BlockSpec((rows, gd), lambda i: (i, 0)), pl.BlockSpec((rows, gd), lambda i: (i, 1)),
                  pl.BlockSpec((1, gd), lambda i: (0, 0)), pl.BlockSpec((1, gd), lambda i: (0, 0)),
                  pl.BlockSpec((groups, q, q), lambda i: (0, 0, 0)),
                  pl.BlockSpec((groups, q, 1), lambda i: (0, 0, 0))],
        out_specs=[row] * len(out_shape), out_shape=out_shape,
        compiler_params=_params("parallel"), name="gmlp",
    )(uv, uv, ln_g.reshape(1, gd), ln_b.reshape(1, gd), ws, bs)


def _merge_kernel(ya_ref, yb_ref, wa_ref, wb_ref, ga_ref, gb_ref, o_ref):
    a = _dot(ya_ref[...], wa_ref[...])
    b = _dot(yb_ref[...], wb_ref[...])
    o_ref[...] = (ga_ref[...] * a + gb_ref[...] * b).astype(o_ref.dtype)


def _merge_call(ya, yb, gates, w_a, w_b, layer):
    t, ka = ya.shape
    kb = yb.shape[1]
    d = w_a.shape[2]
    tm, tn = _pick(t, 1024), _pick(d, 512)
    nd = d // tn
    return pl.pallas_call(
        _merge_kernel, grid=(t // tm, nd),
        in_specs=[pl.BlockSpec((tm, ka), lambda i, j: (i, 0)), pl.BlockSpec((tm, kb), lambda i, j: (i, 0)),
                  pl.BlockSpec((None, ka, tn), lambda i, j: (layer, 0, j)),
                  pl.BlockSpec((None, kb, tn), lambda i, j: (layer, 0, j)),
                  pl.BlockSpec((tm, tn), lambda i, j: (i, j)), pl.BlockSpec((tm, tn), lambda i, j: (i, nd + j))],
        out_specs=pl.BlockSpec((tm, tn), lambda i, j: (i, j)),
        out_shape=jax.ShapeDtypeStruct((t, d), BF16),
        compiler_params=_params("parallel", "parallel"), name="merge",
    )(ya, yb, w_a, w_b, gates, gates)


def _mm_res_ln_kernel(a_ref, w_ref, x_ref, g_ref, b_ref, o32_ref, o16_ref, acc_ref, *, alpha, nk):
    k = pl.program_id(1)

    def finish(total):
        y = _ln_rows(alpha * x_ref[...] + total, g_ref[...], b_ref[...])
        o32_ref[...] = y
        o16_ref[...] = y.astype(BF16)

    if nk == 1:
        finish(_dot(a_ref[...], w_ref[...]))
        return

    @pl.when(k == 0)
    def _():
        acc_ref[...] = _dot(a_ref[...], w_ref[...])

    @pl.when((k > 0) & (k < nk - 1))
    def _():
        acc_ref[...] += _dot(a_ref[...], w_ref[...])

    @pl.when(k == nk - 1)
    def _():
        finish(acc_ref[...] + _dot(a_ref[...], w_ref[...]))


def _k_tile(k, cap):
    if k % LANE != 0:
        return k
    best = LANE
    for m in range(1, k // LANE + 1):
        tk = m * LANE
        if k % tk == 0 and tk <= cap:
            best = tk
    return best


def _mm_res_ln_call(a, w, layer, x, g, b, alpha, name):
    t, k = a.shape
    d = w.shape[2]
    tm = _pick(t, 512)
    tk = _k_tile(k, 2048)
    nk = k // tk
    row = pl.BlockSpec((tm, d), lambda i, kk: (i, 0))
    vec = pl.BlockSpec((1, d), lambda i, kk: (0, 0))
    kern = functools.partial(_mm_res_ln_kernel, alpha=alpha, nk=nk)
    return pl.pallas_call(
        kern, grid=(t // tm, nk),
        in_specs=[pl.BlockSpec((tm, tk), lambda i, kk: (i, kk)),
                  pl.BlockSpec((None, tk, d), lambda i, kk: (layer, kk, 0)), row, vec, vec],
        out_specs=[row, row],
        out_shape=[jax.ShapeDtypeStruct((t, d), F32), jax.ShapeDtypeStruct((t, d), BF16)],
        scratch_shapes=[pltpu.VMEM((tm, d), F32)],
        compiler_params=_params("parallel", "arbitrary"), name=name,
    )(a, w, x, g.reshape(1, d), b.reshape(1, d))


def _swiglu_up_kernel(meta_ref, x_ref, wg_ref, wu_ref, o_ref):
    live = pl.program_id(0) < meta_ref[0]

    @pl.when(live)
    def _():
        x = x_ref[...].astype(BF16)
        g = _dot(x, wg_ref[...])
        u = _dot(x, wu_ref[...])
        o_ref[...] = (_silu(g) * u).astype(o_ref.dtype)

    @pl.when(jnp.logical_not(live))
    def _():
        o_ref[...] = jnp.zeros_like(o_ref)


def _swiglu_up_call(meta, x, wg, wu, layer, tm, name):
    t, k = x.shape
    n = wg.shape[3]
    tn = _pick(n, 1024)
    nn = n // tn

    def live(i, m):
        return jnp.minimum(i, m[0] - 1)

    def col(i, j, m):
        return jnp.where(i < m[0], j, nn - 1)

    wspec = pl.BlockSpec((None, None, k, tn), lambda i, j, m: (layer, m[1 + live(i, m)], 0, col(i, j, m)))
    grid_spec = pltpu.PrefetchScalarGridSpec(
        num_scalar_prefetch=1, grid=(t // tm, nn),
        in_specs=[pl.BlockSpec((tm, k), lambda i, j, m: (live(i, m), 0)), wspec, wspec],
        out_specs=pl.BlockSpec((tm, tn), lambda i, j, m: (i, j)))
    return pl.pallas_call(
        _swiglu_up_kernel, grid_spec=grid_spec, out_shape=jax.ShapeDtypeStruct((t, n), BF16),
        compiler_params=_params("arbitrary", "arbitrary"), name=name,
    )(meta, x, wg, wu)


def _moe_down_kernel(meta_ref, h_ref, w_ref, o_ref, acc_ref, *, nk):
    k = pl.program_id(1)
    live = pl.program_id(0) < meta_ref[0]

    if nk == 1:
        @pl.when(live)
        def _():
            o_ref[...] = _dot(h_ref[...], w_ref[...])
    else:
        @pl.when(live & (k == 0))
        def _():
            acc_ref[...] = _dot(h_ref[...], w_ref[...])

        @pl.when(live & (k > 0) & (k < nk - 1))
        def _():
            acc_ref[...] += _dot(h_ref[...], w_ref[...])

        @pl.when(live & (k == nk - 1))
        def _():
            o_ref[...] = acc_ref[...] + _dot(h_ref[...], w_ref[...])

    @pl.when(jnp.logical_not(live) & (k == nk - 1))
    def _():
        o_ref[...] = jnp.zeros_like(o_ref)


def _moe_down_call(meta, h, wd, layer, tm):
    t, k = h.shape
    d = wd.shape[3]
    tk = _k_tile(k, 2048)
    nk = k // tk

    def live(i, m):
        return jnp.minimum(i, m[0] - 1)

    def kk(i, k_, m):
        return jnp.where(i < m[0], k_, nk - 1)

    grid_spec = pltpu.PrefetchScalarGridSpec(
        num_scalar_prefetch=1, grid=(t // tm, nk),
        in_specs=[pl.BlockSpec((tm, tk), lambda i, k_, m: (live(i, m), kk(i, k_, m))),
                  pl.BlockSpec((None, None, tk, d), lambda i, k_, m: (layer, m[1 + live(i, m)], kk(i, k_, m), 0))],
        out_specs=pl.BlockSpec((tm, d), lambda i, k_, m: (i, 0)),
        scratch_shapes=[pltpu.VMEM((tm, d), F32)])
    return pl.pallas_call(
        functools.partial(_moe_down_kernel, nk=nk), grid_spec=grid_spec,
        out_shape=jax.ShapeDtypeStruct((t, d), F32),
        compiler_params=_params("arbitrary", "arbitrary"), name="moe_down",
    )(meta, h, wd)


def _router_kernel(x_ref, wr_ref, e_ref, w_ref, r_ref, cnt_ref, carry_ref, *, n_exp, tm):
    @pl.when(pl.program_id(0) == 0)
    def _():
        carry_ref[...] = jnp.zeros_like(carry_ref)

    logits = _dot(x_ref[...], wr_ref[...])
    lane = lax.broadcasted_iota(I32, logits.shape, 1)
    lg = jnp.where(lane < n_exp, logits, -jnp.inf)
    m1 = jnp.max(lg, axis=-1, keepdims=True)
    i1 = jnp.min(jnp.where(lg == m1, lane, LANE), axis=-1, keepdims=True)
    lg2 = jnp.where(lane == i1, -jnp.inf, lg)
    m2 = jnp.max(lg2, axis=-1, keepdims=True)
    i2 = jnp.min(jnp.where(lg2 == m2, lane, LANE), axis=-1, keepdims=True)
    ex = jnp.exp(m2 - m1)
    w1 = 1.0 / (1.0 + ex)
    w2 = ex / (1.0 + ex)

    oh1 = (lane == i1).astype(F32)
    oh2 = (lane == i2).astype(F32)
    both = oh1 + oh2
    ri = lax.broadcasted_iota(I32, (tm, tm), 0)
    ci = lax.broadcasted_iota(I32, (tm, tm), 1)
    before = _dot((ri > ci).astype(BF16), both.astype(BF16)) + carry_ref[0:1, :]
    r1 = jnp.sum(before * oh1, axis=-1, keepdims=True)
    r2 = jnp.sum(before * oh2, axis=-1, keepdims=True)
    carry_ref[...] = carry_ref[...] + jnp.sum(both, axis=0, keepdims=True)

    e_ref[...] = jnp.where(lane == 0, i1, jnp.where(lane == 1, i2, 0))
    w_ref[...] = jnp.where(lane == 0, w1, jnp.where(lane == 1, w2, 0.0))
    r_ref[...] = jnp.where(lane == 0, r1, jnp.where(lane == 1, r2, 0.0)).astype(I32)
    cnt_ref[...] = carry_ref[...].astype(I32)


def _router_call(x, w_router):
    t, d = x.shape
    n_exp = w_router.shape[1]
    wr = jnp.zeros((d, LANE), BF16).at[:, :n_exp].set(w_router.astype(BF16))
    tm = _pick(t, 512)
    row = pl.BlockSpec((tm, LANE), lambda i: (i, 0))
    kern = functools.partial(_router_kernel, n_exp=n_exp, tm=tm)
    return pl.pallas_call(
        kern, grid=(t // tm,),
        in_specs=[pl.BlockSpec((tm, d), lambda i: (i, 0)), pl.BlockSpec((d, LANE), lambda i: (0, 0))],
        out_specs=[row, row, row, pl.BlockSpec((SUBLANE, LANE), lambda i: (0, 0))],
        out_shape=[jax.ShapeDtypeStruct((t, LANE), I32), jax.ShapeDtypeStruct((t, LANE), F32),
                   jax.ShapeDtypeStruct((t, LANE), I32), jax.ShapeDtypeStruct((SUBLANE, LANE), I32)],
        scratch_shapes=[pltpu.VMEM((SUBLANE, LANE), F32)],
        compiler_params=_params("arbitrary"), name="router",
    )(x, wr)


_DMA_UNROLL = 8
_DMA_PRIORITIES = 2


def _row_copy(src_hbm, dst_vmem, src_row, dst_row, sem):
    return pltpu.make_async_copy(src_hbm.at[pl.ds(src_row, 1)], dst_vmem.at[pl.ds(dst_row, 1)], sem)


def _gather_kernel(tok_ref, x_hbm, o_ref, sem, *, rows):
    def start(pair, carry):
        for prio in range(_DMA_PRIORITIES):
            r = pair * _DMA_PRIORITIES + prio
            _row_copy(x_hbm, o_ref, tok_ref[0, r], r, sem).start(priority=prio)
        return carry

    lax.fori_loop(0, rows // _DMA_PRIORITIES, start, 0, unroll=_DMA_UNROLL // _DMA_PRIORITIES)
    pltpu.make_async_copy(x_hbm.at[pl.ds(0, rows)], o_ref, sem).wait()


def _gather_call(x, tok, rows):
    cap = tok.shape[0]
    d = x.shape[1]
    tok3 = tok.reshape(cap // rows, 1, rows)
    return pl.pallas_call(
        functools.partial(_gather_kernel, rows=rows), grid=(cap // rows,),
        in_specs=[pl.BlockSpec((None, 1, rows), lambda i: (i, 0, 0), memory_space=pltpu.SMEM),
                  pl.BlockSpec(memory_space=pl.ANY)],
        out_specs=pl.BlockSpec((rows, d), lambda i: (i, 0)),
        out_shape=jax.ShapeDtypeStruct((cap, d), x.dtype),
        scratch_shapes=[pltpu.SemaphoreType.DMA(())],
        compiler_params=_params("arbitrary"), name="moe_gather",
    )(tok3, x)


def _combine_ln_kernel(d0_ref, d1_ref, o_hbm, w_ref, x_ref, g_ref, b_ref, o32_ref, o16_ref, buf0, buf1, sem,
                       *, alpha, rows):
    def start(r, carry):
        _row_copy(o_hbm, buf0, d0_ref[0, r], r, sem).start(priority=0)
        _row_copy(o_hbm, buf1, d1_ref[0, r], r, sem).start(priority=1)
        return carry

    lax.fori_loop(0, rows, start, 0, unroll=_DMA_UNROLL)
    pltpu.make_async_copy(o_hbm.at[pl.ds(0, rows)], buf0, sem).wait()
    pltpu.make_async_copy(o_hbm.at[pl.ds(0, rows)], buf1, sem).wait()
    wts = w_ref[...]
    ffn = wts[:, 0:1] * buf0[...] + wts[:, 1:2] * buf1[...]
    y = _ln_rows(alpha * x_ref[...] + ffn, g_ref[...], b_ref[...])
    o32_ref[...] = y
    o16_ref[...] = y.astype(BF16)


def _combine_ln_call(dest, o_sorted, wts, x, g, b, alpha):
    t, d = x.shape
    rows = _pick(t, 512)
    d0 = dest[:, 0].reshape(t // rows, 1, rows)
    d1 = dest[:, 1].reshape(t // rows, 1, rows)
    idx = pl.BlockSpec((None, 1, rows), lambda i: (i, 0, 0), memory_space=pltpu.SMEM)
    row = pl.BlockSpec((rows, d), lambda i: (i, 0))
    vec = pl.BlockSpec((1, d), lambda i: (0, 0))
    kern = functools.partial(_combine_ln_kernel, alpha=alpha, rows=rows)
    return pl.pallas_call(
        kern, grid=(t // rows,),
        in_specs=[idx, idx, pl.BlockSpec(memory_space=pl.ANY), pl.BlockSpec((rows, LANE), lambda i: (i, 0)),
                  row, vec, vec],
        out_specs=[row, row],
        out_shape=[jax.ShapeDtypeStruct((t, d), F32), jax.ShapeDtypeStruct((t, d), BF16)],
        scratch_shapes=[pltpu.VMEM((rows, d), F32), pltpu.VMEM((rows, d), F32), pltpu.SemaphoreType.DMA(())],
        compiler_params=_params("arbitrary"), name="moe_combine_ln",
    )(d0, d1, o_sorted, wts, x, g.reshape(1, d), b.reshape(1, d))


def _moe_layer(xf, xb, w_router, wg, wu, wd, layer, ln_g, ln_b, alpha):
    t, d = xf.shape
    n_exp = wg.shape[1]
    e_out, w_out, r_out, cnt = _router_call(xb, w_router)
    e = e_out[:, :MOE_TOP_K]
    counts = cnt[0, :n_exp]

    tm = 512 if t * MOE_TOP_K >= 8 * 512 else 128
    n_tiles = -(-(t * MOE_TOP_K) // tm) + n_exp
    cap = n_tiles * tm
    padded = (counts + tm - 1) // tm * tm
    pad_end = jnp.cumsum(padded)
    pad_start = pad_end - padded
    dest = (pad_start[e] + r_out[:, :MOE_TOP_K]).astype(I32)
    tok = jnp.repeat(jnp.arange(t, dtype=I32), MOE_TOP_K)
    slot_tok = jnp.zeros((cap,), I32).at[dest.reshape(-1)].set(tok, unique_indices=True)
    tile_expert = jnp.minimum(jnp.searchsorted(pad_end, jnp.arange(n_tiles, dtype=I32) * tm, side="right"),
                              n_exp - 1)
    meta = jnp.concatenate([(pad_end[-1:] // tm), tile_expert]).astype(I32)

    xs = _gather_call(xf, slot_tok, _pick(cap, 1024))
    h = _swiglu_up_call(meta, xs, wg, wu, layer, tm, "moe_up")
    o_sorted = _moe_down_call(meta, h, wd, layer, tm)
    return _combine_ln_call(dest, o_sorted, w_out, xf, ln_g, ln_b, alpha)


def _trunk(x, conv_bufs, ssm_states, pr, want_v):
    bsz, seq, d = x.shape
    t = bsz * seq
    depth = pr["w_zx"].shape[0]
    alpha = float((2 * depth) ** 0.25)
    d_inner = pr["w_br_a"].shape[1]
    cdim = pr["conv_w"].shape[2]
    gd = pr["w_br_b"].shape[1]
    kc = pr["conv_w"].shape[1]
    off_xbc = d_inner
    off_gate = 2 * gd
    q_ssd = _pick(seq, 128)
    n_groups = (cdim - d_inner) // (2 * ssm_states.shape[-1])
    gs_ssd = next(c for c in (8, 4, 2, 1) if n_groups % c == 0)
    q_gmlp = min(pr["w_s"].shape[2], seq)

    xf, xb = _ln_call(x.reshape(t, d), pr["ln_in_g"], pr["ln_in_b"])
    convs, ssms, vs = [], [], []
    w_main, w_ug = pr["w_zx"], pr["w_ug"]
    lane_aligned = q_ssd % LANE == 0
    for i in range(depth):
        sz = _proj_call(xb, w_main, i, 0, d_inner, "silu", BF16, "in_proj_z")
        uv = _proj_call(xb, w_ug, i, 0, 2 * gd, "gelu", F32, "in_proj_uv")
        gates = _proj_call(xb, w_ug, i, off_gate, 2 * d, "none", BF16, "in_proj_gate", bias=pr["b_gate"][i])
        dt, acum = _dt_call(xb, pr["w_dt"][i], pr["dt_bias"][i], pr["a_log_pad"][i], q_ssd)
        if lane_aligned:
            xbc, tails = _proj_conv_call(xb, w_main, i, off_xbc, pr["conv_w"][i], pr["conv_b"][i], conv_bufs[i], seq)
            ya, ssm_i = _ssd_cm_call(xbc, sz, dt, acum, ssm_states[i], pr["d_skip"][i], pr["ssd_norm_g"][i],
                                     bsz=bsz, seq=seq, q=q_ssd, gs=gs_ssd)
            new_rows = tails.reshape(bsz, -1, _CONV_PAD, cdim)[:, -1, _CONV_PAD - (kc - 1):]
        else:
            xbc = _proj_call(xb, w_main, i, off_xbc, cdim, "none", F32, "in_proj_xbc")
            ya, ssm_i = _ssd_call(xbc, sz, dt, conv_bufs[i], ssm_states[i], pr["conv_w"][i], pr["conv_b"][i],
                                  pr["a_log"][i], pr["d_skip"][i], pr["ssd_norm_g"][i], bsz=bsz, seq=seq, q=q_ssd)
            new_rows = xbc.reshape(bsz, seq, cdim)[:, max(seq - (kc - 1), 0):]
        conv_i = jnp.concatenate([conv_bufs[i], new_rows], axis=1)[:, -(kc - 1):]
        yb, *vn = _gmlp_call(uv, pr["gmlp_ln_g"][i], pr["gmlp_ln_b"][i], pr["w_s"][i], pr["b_s"][i],
                             bsz=bsz, seq=seq, q=q_gmlp, want_v=want_v)
        merged = _merge_call(ya, yb, gates, pr["w_br_a"], pr["w_br_b"], i)
        xf, xb = _mm_res_ln_call(merged, pr["w_o"], i, xf, pr["ln1_g"][i], pr["ln1_b"][i], alpha, "out_proj_ln")
        j = i // 2
        if i % 2 == 0:
            tm = _pick(t, 1024)
            meta = jnp.concatenate([jnp.full((1,), t // tm, I32), jnp.zeros((t // tm,), I32)])
            h = _swiglu_up_call(meta, xb, pr["w_ff_gate"], pr["w_ff_up"], j, tm, "ffn_up")
            xf, xb = _mm_res_ln_call(h, pr["w_ff_down"], j, xf, pr["ln2_g"][i], pr["ln2_b"][i], alpha,
                                     "ffn_down_ln")
        else:
            xf, xb = _moe_layer(xf, xb, pr["w_router"][j], pr["w_moe_gate"], pr["w_moe_up"], pr["w_moe_down"], j,
                                pr["ln2_g"][i], pr["ln2_b"][i], alpha)
        convs.append(conv_i)
        ssms.append(ssm_i)
        vs.extend(v.reshape(bsz, seq, gd) for v in vn)
    return xf.reshape(bsz, seq, d), jnp.stack(convs), jnp.stack(ssms), (jnp.stack(vs) if want_v else None)


def kernel(x_prompt, x_sample, cache_conv, state_ssm, ln_in_g, ln_in_b, w_in, conv_w, conv_b, dt_bias, a_log,
           d_skip, ssd_norm_g, gmlp_ln_g, gmlp_ln_b, w_s, b_s, b_gate, w_br_a, w_br_b, w_o, ln1_g, ln1_b,
           w_ff_gate, w_ff_up, w_ff_down, w_router, w_moe_gate, w_moe_up, w_moe_down, ln2_g, ln2_b):
    depth, d_model, _ = w_in.shape
    d_inner = w_br_a.shape[1]
    cdim = conv_w.shape[2]
    heads = a_log.shape[1]
    o_dt = d_inner + cdim
    w_in16 = w_in.astype(BF16)
    w_ug = w_in16[:, :, o_dt + heads:]
    w_dt = jnp.zeros((depth, d_model, LANE), BF16).at[:, :, :heads].set(w_in16[:, :, o_dt:o_dt + heads])
    dt_b = jnp.zeros((depth, 1, LANE), F32).at[:, 0, :heads].set(dt_bias)
    al_pad = jnp.zeros((depth, 1, LANE), F32).at[:, 0, :heads].set(a_log)
    pr = dict(a_log_pad=al_pad,
        ln_in_g=ln_in_g, ln_in_b=ln_in_b, w_zx=w_in16, w_ug=w_ug, w_dt=w_dt, dt_bias=dt_b, conv_w=conv_w, conv_b=conv_b,
        a_log=a_log, d_skip=d_skip, ssd_norm_g=ssd_norm_g, gmlp_ln_g=gmlp_ln_g, gmlp_ln_b=gmlp_ln_b, w_s=w_s,
        b_s=b_s, b_gate=b_gate, w_br_a=w_br_a.astype(BF16), w_br_b=w_br_b.astype(BF16), w_o=w_o.astype(BF16),
        ln1_g=ln1_g, ln1_b=ln1_b, w_ff_gate=w_ff_gate.astype(BF16)[:, None], w_ff_up=w_ff_up.astype(BF16)[:, None],
        w_ff_down=w_ff_down.astype(BF16), w_router=w_router, w_moe_gate=w_moe_gate.astype(BF16),
        w_moe_up=w_moe_up.astype(BF16), w_moe_down=w_moe_down.astype(BF16), ln2_g=ln2_g, ln2_b=ln2_b)
    bp = x_prompt.shape[0]
    zero_conv = jnp.zeros((depth, bp) + cache_conv.shape[2:], x_prompt.dtype)
    zero_ssm = jnp.zeros((depth, bp) + state_ssm.shape[2:], state_ssm.dtype)
    y_prompt, prompt_conv, prompt_ssm, _ = _trunk(x_prompt, zero_conv, zero_ssm, pr, want_v=False)
    y_sample, sample_conv, sample_ssm, sample_v = _trunk(x_sample, cache_conv, state_ssm, pr, want_v=True)
    return (y_prompt, y_sample, prompt_conv, prompt_ssm, sample_conv, sample_ssm, sample_v)
```

```python
import functools

import jax
import jax.numpy as jnp
from jax import lax
from jax.experimental import pallas as pl
from jax.experimental.pallas import tpu as pltpu

F32 = jnp.float32
BF16 = jnp.bfloat16
I32 = jnp.int32
HIGHEST = lax.Precision.HIGHEST

LN_EPS = 1e-5
LANE = 128
SUBLANE = 8
V7X_VMEM_BYTES = 64 * 1024 * 1024
VMEM_LIMIT_BYTES = V7X_VMEM_BYTES - 8 * 1024 * 1024
MOE_TOP_K = 2
_NEG_BIG = -1e30
_TILE_CANDIDATES = (1024, 512, 256, 128, 64, 32, 16, 8)


def _pick(n, cap):
    for c in _TILE_CANDIDATES:
        if c <= cap and n % c == 0:
            return c
    raise ValueError(f"no tile for {n}")


def _params(*sem):
    return pltpu.CompilerParams(dimension_semantics=sem, vmem_limit_bytes=VMEM_LIMIT_BYTES)


def _ln_rows(x, g, b):
    mu = jnp.mean(x, axis=-1, keepdims=True)
    xc = x - mu
    var = jnp.mean(xc * xc, axis=-1, keepdims=True)
    return xc * lax.rsqrt(var + LN_EPS) * g + b


def _sigmoid(x):
    return 0.5 * jnp.tanh(0.5 * x) + 0.5


def _silu(x):
    return x * _sigmoid(x)


def _gelu(x):
    return 0.5 * x * (1.0 + lax.erf(x * (2.0 ** -0.5)))


def _dot(a, b):
    return jnp.dot(a, b, preferred_element_type=F32)


def _dot_exact(a, b):
    return jnp.dot(a, b, precision=HIGHEST, preferred_element_type=F32)


def _ln_kernel(x_ref, g_ref, b_ref, o32_ref, o16_ref):
    y = _ln_rows(x_ref[...], g_ref[...], b_ref[...])
    o32_ref[...] = y
    o16_ref[...] = y.astype(BF16)


def _ln_call(x, g, b):
    t, d = x.shape
    tm = _pick(t, 512)
    row = pl.BlockSpec((tm, d), lambda i: (i, 0))
    vec = pl.BlockSpec((1, d), lambda i: (0, 0))
    return pl.pallas_call(
        _ln_kernel, grid=(t // tm,), in_specs=[row, vec, vec], out_specs=[row, row],
        out_shape=[jax.ShapeDtypeStruct((t, d), F32), jax.ShapeDtypeStruct((t, d), BF16)],
        compiler_params=_params("parallel"), name="ln_in",
    )(x, g.reshape(1, d), b.reshape(1, d))


_CONV_PAD = SUBLANE

_ACTIVATIONS = {"none": lambda r: r, "silu": _silu, "gelu": _gelu}


def _proj_kernel(x_ref, w_ref, o_ref, *, act):
    o_ref[...] = _ACTIVATIONS[act](_dot(x_ref[...], w_ref[...])).astype(o_ref.dtype)


def _proj_gate_kernel(x_ref, w_ref, b_ref, o_ref):
    o_ref[...] = _sigmoid(_dot(x_ref[...], w_ref[...]) + b_ref[...]).astype(o_ref.dtype)


def _proj_conv_kernel(x_ref, w_ref, cw_ref, cc_ref, cbuf_ref, o_ref, tail_ref, halo_ref, *, kc, tiles_per_seq):
    i, j = pl.program_id(0), pl.program_id(1)
    tm = x_ref.shape[0]
    r = _dot(x_ref[...], w_ref[...])
    before = jnp.where(lax.rem(i, tiles_per_seq) == 0, cbuf_ref[...], halo_ref[j])
    nb = tm // _CONV_PAD
    blocks = r.reshape(nb, _CONV_PAD, r.shape[1])
    prev_blocks = jnp.concatenate([before[None], blocks[:nb - 1]], axis=0)
    sub = lax.broadcasted_iota(I32, blocks.shape, 1)
    acc = cc_ref[...] + r * cw_ref[kc - 1:kc, :]
    for s in range(1, kc):
        shifted = jnp.where(sub < s, pltpu.roll(prev_blocks, s, 1), pltpu.roll(blocks, s, 1))
        acc = acc + shifted.reshape(r.shape) * cw_ref[kc - 1 - s:kc - s, :]
    o_ref[...] = _silu(acc)
    last_rows = r[tm - _CONV_PAD:tm, :]
    tail_ref[...] = last_rows
    halo_ref[j] = last_rows


def _region_tiles(t, n, col0, wide=False):
    tm = _pick(t, 1024)
    cands = ((2 * _TILE_CANDIDATES[0],) if wide and tm == _TILE_CANDIDATES[0] else ()) + _TILE_CANDIDATES
    tn = next(c for c in cands if n % c == 0 and col0 % c == 0)
    return tm, tn


def _proj_call(x, w, layer, col0, n, act, out_dtype, name, bias=None):
    t, k = x.shape
    tm, tn = _region_tiles(t, n, col0, wide=True)
    c0 = col0 // tn
    in_specs = [pl.BlockSpec((tm, k), lambda i, j: (i, 0)), pl.BlockSpec((None, k, tn), lambda i, j: (layer, 0, c0 + j))]
    args = [x, w]
    if bias is None:
        kern = functools.partial(_proj_kernel, act=act)
    else:
        kern = _proj_gate_kernel
        in_specs.append(pl.BlockSpec((1, tn), lambda i, j: (0, j)))
        args.append(bias.reshape(1, n))
    return pl.pallas_call(
        kern, grid=(t // tm, n // tn), in_specs=in_specs,
        out_specs=pl.BlockSpec((tm, tn), lambda i, j: (i, j)),
        out_shape=jax.ShapeDtypeStruct((t, n), out_dtype),
        compiler_params=_params("parallel", "parallel"), name=name,
    )(*args)


def _proj_conv_call(x, w, layer, col0, conv_w, conv_b, conv_buf, seq):
    t, k = x.shape
    kc, n = conv_w.shape
    tm, tn = _pick(seq, 1024), _region_tiles(t, n, col0)[1]
    assert t % tm == 0 and kc - 1 <= _CONV_PAD <= tm
    tiles_per_seq = seq // tm
    c0 = col0 // tn
    cbuf = jnp.pad(conv_buf, ((0, 0), (_CONV_PAD - (kc - 1), 0), (0, 0)))
    kern = functools.partial(_proj_conv_kernel, kc=kc, tiles_per_seq=tiles_per_seq)
    return pl.pallas_call(
        kern, grid=(t // tm, n // tn),
        in_specs=[pl.BlockSpec((tm, k), lambda i, j: (i, 0)),
                  pl.BlockSpec((None, k, tn), lambda i, j: (layer, 0, c0 + j)),
                  pl.BlockSpec((kc, tn), lambda i, j: (0, j)),
                  pl.BlockSpec((1, tn), lambda i, j: (0, j)),
                  pl.BlockSpec((None, _CONV_PAD, tn), lambda i, j: (i // tiles_per_seq, 0, j))],
        out_specs=[pl.BlockSpec((tm, tn), lambda i, j: (i, j)),
                   pl.BlockSpec((None, _CONV_PAD, tn), lambda i, j: (i, 0, j))],
        out_shape=[jax.ShapeDtypeStruct((t, n), F32), jax.ShapeDtypeStruct((t // tm, _CONV_PAD, n), F32)],
        scratch_shapes=[pltpu.VMEM((n // tn, _CONV_PAD, tn), F32)],
        compiler_params=_params("arbitrary", "arbitrary"), name="in_proj_conv",
    )(x, w, conv_w, conv_b.reshape(1, n), cbuf)


def _dt_kernel(x_ref, w_ref, b_ref, al_ref, dt_ref, ac_ref, *, q):
    raw = _dot(x_ref[...], w_ref[...]) + b_ref[...]
    dt = jnp.maximum(raw, 0.0) + jnp.log1p(jnp.exp(-jnp.abs(raw)))
    dt_ref[...] = dt
    a = dt * (-jnp.exp(al_ref[...]))
    ri = lax.broadcasted_iota(I32, (q, q), 0)
    ci = lax.broadcasted_iota(I32, (q, q), 1)
    tril = (ri >= ci).astype(F32)
    for c in range(a.shape[0] // q):
        ac_ref[c * q:(c + 1) * q, :] = _dot_exact(tril, a[c * q:(c + 1) * q, :])


def _dt_call(x, w, bias, a_log, q):
    t, k = x.shape
    n = w.shape[1]
    tm = max(_pick(t, 512), q)
    assert tm % q == 0 and t % tm == 0
    row = pl.BlockSpec((tm, n), lambda i: (i, 0))
    vec = pl.BlockSpec((1, n), lambda i: (0, 0))
    return pl.pallas_call(
        functools.partial(_dt_kernel, q=q), grid=(t // tm,),
        in_specs=[pl.BlockSpec((tm, k), lambda i: (i, 0)), pl.BlockSpec((k, n), lambda i: (0, 0)), vec, vec],
        out_specs=[row, row],
        out_shape=[jax.ShapeDtypeStruct((t, n), F32), jax.ShapeDtypeStruct((t, n), F32)],
        compiler_params=_params("parallel"), name="dt_proj",
    )(x, w, bias, a_log)


def _ssd_kernel(xs_ref, bm_ref, cm_ref, sz_ref, dt_ref, dtt_ref, cbx_ref, cbb_ref, cbc_ref, s0_ref,
                cwx_ref, cwb_ref, cwc_ref, ccx_ref, ccb_ref, ccc_ref, alr_ref, alc_ref, dsk_ref, ng_ref,
                y_ref, sout_ref, xpad_ref, st_ref, *, q, hpg, p, n, kc, nc):
    c = pl.program_id(2)
    w = hpg * p
    tail = kc - 1
    t0 = _CONV_PAD - tail

    @pl.when(c == 0)
    def _():
        xpad_ref[t0:_CONV_PAD, 0:w] = cbx_ref[...]
        xpad_ref[t0:_CONV_PAD, w:w + n] = cbb_ref[...]
        xpad_ref[t0:_CONV_PAD, w + n:w + 2 * n] = cbc_ref[...]
        st_ref[...] = s0_ref[...].reshape(w, n).T

    xpad_ref[_CONV_PAD:_CONV_PAD + q, 0:w] = xs_ref[...]
    xpad_ref[_CONV_PAD:_CONV_PAD + q, w:w + n] = bm_ref[...]
    xpad_ref[_CONV_PAD:_CONV_PAD + q, w + n:w + 2 * n] = cm_ref[...]

    def conv(lo, hi, cw_ref, cc_ref):
        acc = cc_ref[...]
        for k in range(kc):
            acc = acc + xpad_ref[t0 + k:t0 + k + q, lo:hi] * cw_ref[k:k + 1, :]
        return _silu(acc)

    xs = conv(0, w, cwx_ref, ccx_ref)
    bm = conv(w, w + n, cwb_ref, ccb_ref)
    cm = conv(w + n, w + 2 * n, cwc_ref, ccc_ref)
    xpad_ref[t0:_CONV_PAD, :] = xpad_ref[t0 + q:_CONV_PAD + q, :]

    a = dt_ref[...] * (-jnp.exp(alr_ref[...]))
    a_t = dtt_ref[...] * (-jnp.exp(alc_ref[...]))
    ri = lax.broadcasted_iota(I32, (q, q), 0)
    ci = lax.broadcasted_iota(I32, (q, q), 1)
    tril = ri >= ci
    acol = _dot_exact(tril.astype(F32), a)
    arow = _dot_exact(a_t, (ri <= ci).astype(F32))

    hrow = lax.broadcasted_iota(I32, (hpg, w), 0)
    hlane = lax.broadcasted_iota(I32, (hpg, w), 1)
    expand = ((hlane >= hrow * p) & (hlane < (hrow + 1) * p)).astype(F32)
    xdt = xs * _dot_exact(dt_ref[...], expand)
    eacol = _dot_exact(jnp.exp(acol), expand)
    to_end = _dot_exact(jnp.exp(acol[q - 1:q, :] - acol), expand)

    bm16 = bm.astype(BF16)
    cm16 = cm.astype(BF16)
    cb = lax.dot_general(cm16, bm16, (((1,), (1,)), ((), ())), preferred_element_type=F32)
    st = st_ref[...]
    y = _dot(cm16, st.astype(BF16)) * eacol
    lane = lax.broadcasted_iota(I32, (q, w), 1)
    for h in range(hpg):
        seg = acol[:, h:h + 1] - arow[h:h + 1, :]
        m = (cb * jnp.exp(jnp.where(tril, seg, _NEG_BIG))).astype(BF16)
        head = (lane >= h * p) & (lane < (h + 1) * p)
        y = y + _dot(m, jnp.where(head, xdt, 0.0).astype(BF16))
    y = y + dsk_ref[...] * xs
    hz = y * sz_ref[...]
    y_ref[...] = (hz * lax.rsqrt(jnp.mean(hz * hz, axis=-1, keepdims=True) + LN_EPS) * ng_ref[...]).astype(BF16)

    upd = lax.dot_general(bm16, (xdt * to_end).astype(BF16), (((0,), (0,)), ((), ())),
                          preferred_element_type=F32)
    st_ref[...] = st * eacol[q - 1:q, :] + upd

    @pl.when(c == nc - 1)
    def _():
        sout_ref[...] = st_ref[...].T.reshape(hpg, p, n)


def _ssd_call(xbc, sz, dt, conv_buf, state0, conv_w, conv_b, a_log, d_skip, norm_g, *, bsz, seq, q):
    heads, p, n = state0.shape[1:]
    kc, cdim = conv_w.shape
    d_inner = sz.shape[1]
    groups = (cdim - d_inner) // (2 * n)
    hpg = heads // groups
    w = hpg * p
    nc = seq // q
    t = bsz * seq
    assert w * groups == d_inner and seq % q == 0 and q >= kc - 1 and d_inner % n == 0

    dt4 = dt[:, :heads].reshape(bsz, seq, groups, hpg).transpose(0, 2, 1, 3)
    dtt4 = dt4.transpose(0, 1, 3, 2)
    alr = a_log.reshape(groups, 1, hpg)
    alc = a_log.reshape(groups, hpg, 1)
    dsk = jnp.repeat(d_skip, p).reshape(1, d_inner)
    ng = norm_g.reshape(1, d_inner)
    ccb = conv_b.reshape(1, cdim)

    bb, cb_ = d_inner // n, d_inner // n + groups
    row = lambda b, g, c: b * nc + c
    in_specs = [
        pl.BlockSpec((q, w), lambda b, g, c: (row(b, g, c), g)),
        pl.BlockSpec((q, n), lambda b, g, c: (row(b, g, c), bb + g)),
        pl.BlockSpec((q, n), lambda b, g, c: (row(b, g, c), cb_ + g)),
        pl.BlockSpec((q, w), lambda b, g, c: (row(b, g, c), g)),
        pl.BlockSpec((None, None, q, hpg), lambda b, g, c: (b, g, c, 0)),
        pl.BlockSpec((None, None, hpg, q), lambda b, g, c: (b, g, 0, c)),
        pl.BlockSpec((None, kc - 1, w), lambda b, g, c: (b, 0, g)),
        pl.BlockSpec((None, kc - 1, n), lambda b, g, c: (b, 0, d_inner // n + g)),
        pl.BlockSpec((None, kc - 1, n), lambda b, g, c: (b, 0, d_inner // n + groups + g)),
        pl.BlockSpec((None, hpg, p, n), lambda b, g, c: (b, g, 0, 0)),
        pl.BlockSpec((kc, w), lambda b, g, c: (0, g)),
        pl.BlockSpec((kc, n), lambda b, g, c: (0, d_inner // n + g)),
        pl.BlockSpec((kc, n), lambda b, g, c: (0, d_inner // n + groups + g)),
        pl.BlockSpec((1, w), lambda b, g, c: (0, g)),
        pl.BlockSpec((1, n), lambda b, g, c: (0, d_inner // n + g)),
        pl.BlockSpec((1, n), lambda b, g, c: (0, d_inner // n + groups + g)),
        pl.BlockSpec((None, 1, hpg), lambda b, g, c: (g, 0, 0)),
        pl.BlockSpec((None, hpg, 1), lambda b, g, c: (g, 0, 0)),
        pl.BlockSpec((1, w), lambda b, g, c: (0, g)),
        pl.BlockSpec((1, w), lambda b, g, c: (0, g)),
    ]
    out_specs = [
        pl.BlockSpec((q, w), lambda b, g, c: (row(b, g, c), g)),
        pl.BlockSpec((None, hpg, p, n), lambda b, g, c: (b, g, 0, 0)),
    ]
    kern = functools.partial(_ssd_kernel, q=q, hpg=hpg, p=p, n=n, kc=kc, nc=nc)
    return pl.pallas_call(
        kern, grid=(bsz, groups, nc), in_specs=in_specs, out_specs=out_specs,
        out_shape=[jax.ShapeDtypeStruct((t, d_inner), BF16), jax.ShapeDtypeStruct(state0.shape, F32)],
        scratch_shapes=[pltpu.VMEM((_CONV_PAD + q, w + 2 * n), F32), pltpu.VMEM((n, w), F32)],
        compiler_params=_params("parallel", "parallel", "arbitrary"), name="ssd",
    )(xbc, xbc, xbc, sz, dt4, dtt4, conv_buf, conv_buf, conv_buf, state0,
      conv_w, conv_w, conv_w, ccb, ccb, ccb, alr, alc, dsk, ng)


def _ssd_cm_kernel(xs_ref, bm_ref, cm_ref, sz_ref, dtt_ref, act_ref, ac_ref, s0_ref, dsk_ref, ng_ref,
                   y_ref, sout_ref, st_ref, *, q, gs, hpg, p, n, nc):
    c = pl.program_id(2)
    w = hpg * p

    @pl.when(c == 0)
    def _():
        st_ref[...] = s0_ref[...].reshape(gs * w, n)

    def rows(v):
        return jnp.concatenate([jnp.broadcast_to(v[h:h + 1, :], (p, v.shape[1])) for h in range(hpg)], axis=0)

    si = lax.broadcasted_iota(I32, (q, q), 0)
    li = lax.broadcasted_iota(I32, (q, q), 1)
    keep = si <= li
    nt = (((1,), (1,)), ((), ()))
    for g in range(gs):
        xs = xs_ref[:, g * w:(g + 1) * w]
        bm16 = bm_ref[:, g * n:(g + 1) * n].astype(BF16)
        cm16 = cm_ref[:, g * n:(g + 1) * n].astype(BF16)
        arow = act_ref[g]
        acol = ac_ref[g]
        ear = jnp.exp(arow)
        te = jnp.exp(arow[:, q - 1:q] - arow)
        xdt_t = xs.T * rows(dtt_ref[g])
        xdt16 = xdt_t.astype(BF16)
        cb_t = lax.dot_general(bm16, cm16, nt, preferred_element_type=F32)
        ys = []
        for h in range(hpg):
            seg = arow[h:h + 1, :] - acol[:, h:h + 1]
            m_t = (cb_t * jnp.exp(jnp.where(keep, seg, _NEG_BIG))).astype(BF16)
            ys.append(_dot(xdt16[h * p:(h + 1) * p, :], m_t))
        st = st_ref[g * w:(g + 1) * w, :]
        y_t = jnp.concatenate(ys, axis=0)
        y_t = y_t + lax.dot_general(st.astype(BF16), cm16, nt, preferred_element_type=F32) * rows(ear)
        y = y_t.T + dsk_ref[:, g * w:(g + 1) * w] * xs
        hz = y * sz_ref[:, g * w:(g + 1) * w]
        y_ref[:, g * w:(g + 1) * w] = (hz * lax.rsqrt(jnp.mean(hz * hz, axis=-1, keepdims=True) + LN_EPS)
                                       * ng_ref[:, g * w:(g + 1) * w]).astype(BF16)
        upd = _dot((xdt_t * rows(te)).astype(BF16), bm16)
        st_ref[g * w:(g + 1) * w, :] = st * rows(ear[:, q - 1:q]) + upd

    @pl.when(c == nc - 1)
    def _():
        sout_ref[...] = st_ref[...].reshape(gs * hpg, p, n)


def _ssd_cm_call(xbc, sz, dt, acum, state0, d_skip, norm_g, *, bsz, seq, q, gs):
    heads, p, n = state0.shape[1:]
    cdim = xbc.shape[1]
    d_inner = sz.shape[1]
    groups = (cdim - d_inner) // (2 * n)
    hpg = heads // groups
    w = hpg * p
    nc = seq // q
    t = bsz * seq
    gw, gn = gs * w, gs * n
    assert w * groups == d_inner and seq % q == 0 and q % LANE == 0 and groups % gs == 0
    assert d_inner % gn == 0 and (groups * n) % gn == 0

    def heads_major(v):
        return v[:, :heads].reshape(bsz, seq, groups, hpg).transpose(0, 2, 3, 1)

    dtt4 = heads_major(dt)
    act4 = heads_major(acum)
    ac4 = act4.transpose(0, 1, 3, 2)
    dsk = jnp.repeat(d_skip, p).reshape(1, d_inner)
    ng = norm_g.reshape(1, d_inner)

    bb, cb_ = d_inner // gn, (d_inner + groups * n) // gn
    row = lambda b, g, c: b * nc + c
    in_specs = [
        pl.BlockSpec((q, gw), lambda b, g, c: (row(b, g, c), g)),
        pl.BlockSpec((q, gn), lambda b, g, c: (row(b, g, c), bb + g)),
        pl.BlockSpec((q, gn), lambda b, g, c: (row(b, g, c), cb_ + g)),
        pl.BlockSpec((q, gw), lambda b, g, c: (row(b, g, c), g)),
        pl.BlockSpec((None, gs, hpg, q), lambda b, g, c: (b, g, 0, c)),
        pl.BlockSpec((None, gs, hpg, q), lambda b, g, c: (b, g, 0, c)),
        pl.BlockSpec((None, gs, q, hpg), lambda b, g, c: (b, g, c, 0)),
        pl.BlockSpec((None, gs * hpg, p, n), lambda b, g, c: (b, g, 0, 0)),
        pl.BlockSpec((1, gw), lambda b, g, c: (0, g)),
        pl.BlockSpec((1, gw), lambda b, g, c: (0, g)),
    ]
    out_specs = [
        pl.BlockSpec((q, gw), lambda b, g, c: (row(b, g, c), g)),
        pl.BlockSpec((None, gs * hpg, p, n), lambda b, g, c: (b, g, 0, 0)),
    ]
    kern = functools.partial(_ssd_cm_kernel, q=q, gs=gs, hpg=hpg, p=p, n=n, nc=nc)
    return pl.pallas_call(
        kern, grid=(bsz, groups // gs, nc), in_specs=in_specs, out_specs=out_specs,
        out_shape=[jax.ShapeDtypeStruct((t, d_inner), BF16), jax.ShapeDtypeStruct(state0.shape, F32)],
        scratch_shapes=[pltpu.VMEM((gs * w, n), F32)],
        compiler_params=_params("parallel", "parallel", "arbitrary"), name="ssd_cm",
    )(xbc, xbc, xbc, sz, dtt4, act4, ac4, state0, dsk, ng)


_GMLP_CHUNKS_PER_STEP = (4, 2, 1)


def _gmlp_kernel(u_ref, v_ref, lg_ref, lb_ref, ws_ref, bs_ref, yb_ref, *vn_refs, groups, q):
    u = u_ref[...]
    vn = _ln_rows(v_ref[...], lg_ref[...], lb_ref[...])
    for vn_ref in vn_refs:
        vn_ref[...] = vn
    vn16 = vn.astype(BF16)
    d = vn.shape[1] // groups
    ri = lax.broadcasted_iota(I32, (q, q), 0)
    ci = lax.broadcasted_iota(I32, (q, q), 1)
    for g in range(groups):
        wg = jnp.where(ri >= ci, ws_ref[g], 0.0).astype(BF16)
        for c in range(vn.shape[0] // q):
            rows = slice(c * q, (c + 1) * q)
            s = _dot(wg, vn16[rows, g * d:(g + 1) * d]) + bs_ref[g]
            yb_ref[rows, g * d:(g + 1) * d] = (u[rows, g * d:(g + 1) * d] * s).astype(BF16)


def _gmlp_call(uv, ln_g, ln_b, w_s, b_s, *, bsz, seq, q, want_v):
    groups = w_s.shape[0]
    t = bsz * seq
    gd = uv.shape[1] // 2
    assert seq % q == 0 and (gd // groups) % LANE == 0
    ws = w_s[:, :q, :q]
    bs = b_s[:, :q, None]
    rows = next(c * q for c in _GMLP_CHUNKS_PER_STEP if seq % (c * q) == 0)
    row = pl.BlockSpec((rows, gd), lambda i: (i, 0))
    kern = functools.partial(_gmlp_kernel, groups=groups, q=q)
    out_shape = [jax.ShapeDtypeStruct((t, gd), BF16)] + ([jax.ShapeDtypeStruct((t, gd), F32)] if want_v else [])
    return pl.pallas_call(
        kern, grid=(t // rows,),
        in_specs=[pl.BlockSpec((rows, gd), lambda i: (i, 0)), pl.BlockSpec((rows, gd), lambda i: (i, 1)),
                  pl.BlockSpec((1, gd), lambda i: (0, 0)), pl.BlockSpec((1, gd), lambda i: (0, 0)),
                  pl.BlockSpec((groups, q, q), lambda i: (0, 0, 0)),
                  pl.BlockSpec((groups, q, 1), lambda i: (0, 0, 0))],
        out_specs=[row] * len(out_shape), out_shape=out_shape,
        compiler_params=_params("parallel"), name="gmlp",
    )(uv, uv, ln_g.reshape(1, gd), ln_b.reshape(1, gd), ws, bs)


def _merge_kernel(ya_ref, yb_ref, wa_ref, wb_ref, ga_ref, gb_ref, o_ref):
    a = _dot(ya_ref[...], wa_ref[...])
    b = _dot(yb_ref[...], wb_ref[...])
    o_ref[...] = (ga_ref[...] * a + gb_ref[...] * b).astype(o_ref.dtype)


def _merge_call(ya, yb, gates, w_a, w_b, layer):
    t, ka = ya.shape
    kb = yb.shape[1]
    d = w_a.shape[2]
    tm, tn = _pick(t, 1024), _pick(d, 512)
    nd = d // tn
    return pl.pallas_call(
        _merge_kernel, grid=(t // tm, nd),
        in_specs=[pl.BlockSpec((tm, ka), lambda i, j: (i, 0)), pl.BlockSpec((tm, kb), lambda i, j: (i, 0)),
                  pl.BlockSpec((None, ka, tn), lambda i, j: (layer, 0, j)),
                  pl.BlockSpec((None, kb, tn), lambda i, j: (layer, 0, j)),
                  pl.BlockSpec((tm, tn), lambda i, j: (i, j)), pl.BlockSpec((tm, tn), lambda i, j: (i, nd + j))],
        out_specs=pl.BlockSpec((tm, tn), lambda i, j: (i, j)),
        out_shape=jax.ShapeDtypeStruct((t, d), BF16),
        compiler_params=_params("parallel", "parallel"), name="merge",
    )(ya, yb, w_a, w_b, gates, gates)


def _mm_res_ln_kernel(a_ref, w_ref, x_ref, g_ref, b_ref, o32_ref, o16_ref, acc_ref, *, alpha, nk):
    k = pl.program_id(1)

    def finish(total):
        y = _ln_rows(alpha * x_ref[...] + total, g_ref[...], b_ref[...])
        o32_ref[...] = y
        o16_ref[...] = y.astype(BF16)

    if nk == 1:
        finish(_dot(a_ref[...], w_ref[...]))
        return

    @pl.when(k == 0)
    def _():
        acc_ref[...] = _dot(a_ref[...], w_ref[...])

    @pl.when((k > 0) & (k < nk - 1))
    def _():
        acc_ref[...] += _dot(a_ref[...], w_ref[...])

    @pl.when(k == nk - 1)
    def _():
        finish(acc_ref[...] + _dot(a_ref[...], w_ref[...]))


def _k_tile(k, cap):
    if k % LANE != 0:
        return k
    best = LANE
    for m in range(1, k // LANE + 1):
        tk = m * LANE
        if k % tk == 0 and tk <= cap:
            best = tk
    return best


def _mm_res_ln_call(a, w, layer, x, g, b, alpha, name):
    t, k = a.shape
    d = w.shape[2]
    tm = _pick(t, 512)
    tk = _k_tile(k, 2048)
    nk = k // tk
    row = pl.BlockSpec((tm, d), lambda i, kk: (i, 0))
    vec = pl.BlockSpec((1, d), lambda i, kk: (0, 0))
    kern = functools.partial(_mm_res_ln_kernel, alpha=alpha, nk=nk)
    return pl.pallas_call(
        kern, grid=(t // tm, nk),
        in_specs=[pl.BlockSpec((tm, tk), lambda i, kk: (i, kk)),
                  pl.BlockSpec((None, tk, d), lambda i, kk: (layer, kk, 0)), row, vec, vec],
        out_specs=[row, row],
        out_shape=[jax.ShapeDtypeStruct((t, d), F32), jax.ShapeDtypeStruct((t, d), BF16)],
        scratch_shapes=[pltpu.VMEM((tm, d), F32)],
        compiler_params=_params("parallel", "arbitrary"), name=name,
    )(a, w, x, g.reshape(1, d), b.reshape(1, d))


def _swiglu_up_kernel(meta_ref, x_ref, wg_ref, wu_ref, o_ref):
    live = pl.program_id(0) < meta_ref[0]

    @pl.when(live)
    def _():
        x = x_ref[...].astype(BF16)
        g = _dot(x, wg_ref[...])
        u = _dot(x, wu_ref[...])
        o_ref[...] = (_silu(g) * u).astype(o_ref.dtype)

    @pl.when(jnp.logical_not(live))
    def _():
        o_ref[...] = jnp.zeros_like(o_ref)


def _swiglu_up_call(meta, x, wg, wu, layer, tm, name):
    t, k = x.shape
    n = wg.shape[3]
    tn = _pick(n, 1024)
    nn = n // tn

    def live(i, m):
        return jnp.minimum(i, m[0] - 1)

    def col(i, j, m):
        return jnp.where(i < m[0], j, nn - 1)

    wspec = pl.BlockSpec((None, None, k, tn), lambda i, j, m: (layer, m[1 + live(i, m)], 0, col(i, j, m)))
    grid_spec = pltpu.PrefetchScalarGridSpec(
        num_scalar_prefetch=1, grid=(t // tm, nn),
        in_specs=[pl.BlockSpec((tm, k), lambda i, j, m: (live(i, m), 0)), wspec, wspec],
        out_specs=pl.BlockSpec((tm, tn), lambda i, j, m: (i, j)))
    return pl.pallas_call(
        _swiglu_up_kernel, grid_spec=grid_spec, out_shape=jax.ShapeDtypeStruct((t, n), BF16),
        compiler_params=_params("arbitrary", "arbitrary"), name=name,
    )(meta, x, wg, wu)


def _moe_down_kernel(meta_ref, h_ref, w_ref, o_ref, acc_ref, *, nk):
    k = pl.program_id(1)
    live = pl.program_id(0) < meta_ref[0]

    if nk == 1:
        @pl.when(live)
        def _():
            o_ref[...] = _dot(h_ref[...], w_ref[...])
    else:
        @pl.when(live & (k == 0))
        def _():
            acc_ref[...] = _dot(h_ref[...], w_ref[...])

        @pl.when(live & (k > 0) & (k < nk - 1))
        def _():
            acc_ref[...] += _dot(h_ref[...], w_ref[...])

        @pl.when(live & (k == nk - 1))
        def _():
            o_ref[...] = acc_ref[...] + _dot(h_ref[...], w_ref[...])

    @pl.when(jnp.logical_not(live) & (k == nk - 1))
    def _():
        o_ref[...] = jnp.zeros_like(o_ref)


def _moe_down_call(meta, h, wd, layer, tm):
    t, k = h.shape
    d = wd.shape[3]
    tk = _k_tile(k, 2048)
    nk = k // tk

    def live(i, m):
        return jnp.minimum(i, m[0] - 1)

    def kk(i, k_, m):
        return jnp.where(i < m[0], k_, nk - 1)

    grid_spec = pltpu.PrefetchScalarGridSpec(
        num_scalar_prefetch=1, grid=(t // tm, nk),
        in_specs=[pl.BlockSpec((tm, tk), lambda i, k_, m: (live(i, m), kk(i, k_, m))),
                  pl.BlockSpec((None, None, tk, d), lambda i, k_, m: (layer, m[1 + live(i, m)], kk(i, k_, m), 0))],
        out_specs=pl.BlockSpec((tm, d), lambda i, k_, m: (i, 0)),
        scratch_shapes=[pltpu.VMEM((tm, d), F32)])
    return pl.pallas_call(
        functools.partial(_moe_down_kernel, nk=nk), grid_spec=grid_spec,
        out_shape=jax.ShapeDtypeStruct((t, d), F32),
        compiler_params=_params("arbitrary", "arbitrary"), name="moe_down",
    )(meta, h, wd)


def _router_kernel(x_ref, wr_ref, e_ref, w_ref, r_ref, cnt_ref, carry_ref, *, n_exp, tm):
    @pl.when(pl.program_id(0) == 0)
    def _():
        carry_ref[...] = jnp.zeros_like(carry_ref)

    logits = _dot(x_ref[...], wr_ref[...])
    lane = lax.broadcasted_iota(I32, logits.shape, 1)
    lg = jnp.where(lane < n_exp, logits, -jnp.inf)
    m1 = jnp.max(lg, axis=-1, keepdims=True)
    i1 = jnp.min(jnp.where(lg == m1, lane, LANE), axis=-1, keepdims=True)
    lg2 = jnp.where(lane == i1, -jnp.inf, lg)
    m2 = jnp.max(lg2, axis=-1, keepdims=True)
    i2 = jnp.min(jnp.where(lg2 == m2, lane, LANE), axis=-1, keepdims=True)
    ex = jnp.exp(m2 - m1)
    w1 = 1.0 / (1.0 + ex)
    w2 = ex / (1.0 + ex)

    oh1 = (lane == i1).astype(F32)
    oh2 = (lane == i2).astype(F32)
    both = oh1 + oh2
    ri = lax.broadcasted_iota(I32, (tm, tm), 0)
    ci = lax.broadcasted_iota(I32, (tm, tm), 1)
    before = _dot((ri > ci).astype(BF16), both.astype(BF16)) + carry_ref[0:1, :]
    r1 = jnp.sum(before * oh1, axis=-1, keepdims=True)
    r2 = jnp.sum(before * oh2, axis=-1, keepdims=True)
    carry_ref[...] = carry_ref[...] + jnp.sum(both, axis=0, keepdims=True)

    e_ref[...] = jnp.where(lane == 0, i1, jnp.where(lane == 1, i2, 0))
    w_ref[...] = jnp.where(lane == 0, w1, jnp.where(lane == 1, w2, 0.0))
    r_ref[...] = jnp.where(lane == 0, r1, jnp.where(lane == 1, r2, 0.0)).astype(I32)
    cnt_ref[...] = carry_ref[...].astype(I32)


def _router_call(x, w_router):
    t, d = x.shape
    n_exp = w_router.shape[1]
    wr = jnp.zeros((d, LANE), BF16).at[:, :n_exp].set(w_router.astype(BF16))
    tm = _pick(t, 512)
    row = pl.BlockSpec((tm, LANE), lambda i: (i, 0))
    kern = functools.partial(_router_kernel, n_exp=n_exp, tm=tm)
    return pl.pallas_call(
        kern, grid=(t // tm,),
        in_specs=[pl.BlockSpec((tm, d), lambda i: (i, 0)), pl.BlockSpec((d, LANE), lambda i: (0, 0))],
        out_specs=[row, row, row, pl.BlockSpec((SUBLANE, LANE), lambda i: (0, 0))],
        out_shape=[jax.ShapeDtypeStruct((t, LANE), I32), jax.ShapeDtypeStruct((t, LANE), F32),
                   jax.ShapeDtypeStruct((t, LANE), I32), jax.ShapeDtypeStruct((SUBLANE, LANE), I32)],
        scratch_shapes=[pltpu.VMEM((SUBLANE, LANE), F32)],
        compiler_params=_params("arbitrary"), name="router",
    )(x, wr)


_DMA_UNROLL = 8
_DMA_PRIORITIES = 2


def _row_copy(src_hbm, dst_vmem, src_row, dst_row, sem):
    return pltpu.make_async_copy(src_hbm.at[pl.ds(src_row, 1)], dst_vmem.at[pl.ds(dst_row, 1)], sem)


def _gather_kernel(tok_ref, x_hbm, o_ref, sem, *, rows):
    def start(pair, carry):
        for prio in range(_DMA_PRIORITIES):
            r = pair * _DMA_PRIORITIES + prio
            _row_copy(x_hbm, o_ref, tok_ref[0, r], r, sem).start(priority=prio)
        return carry

    lax.fori_loop(0, rows // _DMA_PRIORITIES, start, 0, unroll=_DMA_UNROLL // _DMA_PRIORITIES)
    pltpu.make_async_copy(x_hbm.at[pl.ds(0, rows)], o_ref, sem).wait()


def _gather_call(x, tok, rows):
    cap = tok.shape[0]
    d = x.shape[1]
    tok3 = tok.reshape(cap // rows, 1, rows)
    return pl.pallas_call(
        functools.partial(_gather_kernel, rows=rows), grid=(cap // rows,),
        in_specs=[pl.BlockSpec((None, 1, rows), lambda i: (i, 0, 0), memory_space=pltpu.SMEM),
                  pl.BlockSpec(memory_space=pl.ANY)],
        out_specs=pl.BlockSpec((rows, d), lambda i: (i, 0)),
        out_shape=jax.ShapeDtypeStruct((cap, d), x.dtype),
        scratch_shapes=[pltpu.SemaphoreType.DMA(())],
        compiler_params=_params("arbitrary"), name="moe_gather",
    )(tok3, x)


def _combine_ln_kernel(d0_ref, d1_ref, o_hbm, w_ref, x_ref, g_ref, b_ref, o32_ref, o16_ref, buf0, buf1, sem,
                       *, alpha, rows):
    def start(r, carry):
        _row_copy(o_hbm, buf0, d0_ref[0, r], r, sem).start(priority=0)
        _row_copy(o_hbm, buf1, d1_ref[0, r], r, sem).start(priority=1)
        return carry

    lax.fori_loop(0, rows, start, 0, unroll=_DMA_UNROLL)
    pltpu.make_async_copy(o_hbm.at[pl.ds(0, rows)], buf0, sem).wait()
    pltpu.make_async_copy(o_hbm.at[pl.ds(0, rows)], buf1, sem).wait()
    wts = w_ref[...]
    ffn = wts[:, 0:1] * buf0[...] + wts[:, 1:2] * buf1[...]
    y = _ln_rows(alpha * x_ref[...] + ffn, g_ref[...], b_ref[...])
    o32_ref[...] = y
    o16_ref[...] = y.astype(BF16)


def _combine_ln_call(dest, o_sorted, wts, x, g, b, alpha):
    t, d = x.shape
    rows = _pick(t, 512)
    d0 = dest[:, 0].reshape(t // rows, 1, rows)
    d1 = dest[:, 1].reshape(t // rows, 1, rows)
    idx = pl.BlockSpec((None, 1, rows), lambda i: (i, 0, 0), memory_space=pltpu.SMEM)
    row = pl.BlockSpec((rows, d), lambda i: (i, 0))
    vec = pl.BlockSpec((1, d), lambda i: (0, 0))
    kern = functools.partial(_combine_ln_kernel, alpha=alpha, rows=rows)
    return pl.pallas_call(
        kern, grid=(t // rows,),
        in_specs=[idx, idx, pl.BlockSpec(memory_space=pl.ANY), pl.BlockSpec((rows, LANE), lambda i: (i, 0)),
                  row, vec, vec],
        out_specs=[row, row],
        out_shape=[jax.ShapeDtypeStruct((t, d), F32), jax.ShapeDtypeStruct((t, d), BF16)],
        scratch_shapes=[pltpu.VMEM((rows, d), F32), pltpu.VMEM((rows, d), F32), pltpu.SemaphoreType.DMA(())],
        compiler_params=_params("arbitrary"), name="moe_combine_ln",
    )(d0, d1, o_sorted, wts, x, g.reshape(1, d), b.reshape(1, d))


def _moe_layer(xf, xb, w_router, wg, wu, wd, layer, ln_g, ln_b, alpha):
    t, d = xf.shape
    n_exp = wg.shape[1]
    e_out, w_out, r_out, cnt = _router_call(xb, w_router)
    e = e_out[:, :MOE_TOP_K]
    counts = cnt[0, :n_exp]

    tm = 512 if t * MOE_TOP_K >= 8 * 512 else 128
    n_tiles = -(-(t * MOE_TOP_K) // tm) + n_exp
    cap = n_tiles * tm
    padded = (counts + tm - 1) // tm * tm
    pad_end = jnp.cumsum(padded)
    pad_start = pad_end - padded
    dest = (pad_start[e] + r_out[:, :MOE_TOP_K]).astype(I32)
    tok = jnp.repeat(jnp.arange(t, dtype=I32), MOE_TOP_K)
    slot_tok = jnp.zeros((cap,), I32).at[dest.reshape(-1)].set(tok, unique_indices=True)
    tile_expert = jnp.minimum(jnp.searchsorted(pad_end, jnp.arange(n_tiles, dtype=I32) * tm, side="right"),
                              n_exp - 1)
    meta = jnp.concatenate([(pad_end[-1:] // tm), tile_expert]).astype(I32)

    xs = _gather_call(xf, slot_tok, _pick(cap, 1024))
    h = _swiglu_up_call(meta, xs, wg, wu, layer, tm, "moe_up")
    o_sorted = _moe_down_call(meta, h, wd, layer, tm)
    return _combine_ln_call(dest, o_sorted, w_out, xf, ln_g, ln_b, alpha)


def _trunk(x, conv_bufs, ssm_states, pr, want_v):
    bsz, seq, d = x.shape
    t = bsz * seq
    depth = pr["w_zx"].shape[0]
    alpha = float((2 * depth) ** 0.25)
    d_inner = pr["w_br_a"].shape[1]
    cdim = pr["conv_w"].shape[2]
    gd = pr["w_br_b"].shape[1]
    kc = pr["conv_w"].shape[1]
    off_xbc = d_inner
    off_gate = 2 * gd
    q_ssd = _pick(seq, 128)
    n_groups = (cdim - d_inner) // (2 * ssm_states.shape[-1])
    gs_ssd = next(c for c in (8, 4, 2, 1) if n_groups % c == 0)
    q_gmlp = min(pr["w_s"].shape[2], seq)

    xf, xb = _ln_call(x.reshape(t, d), pr["ln_in_g"], pr["ln_in_b"])
    convs, ssms, vs = [], [], []
    w_main, w_ug = pr["w_zx"], pr["w_ug"]
    lane_aligned = q_ssd % LANE == 0
    for i in range(depth):
        sz = _proj_call(xb, w_main, i, 0, d_inner, "silu", BF16, "in_proj_z")
        uv = _proj_call(xb, w_ug, i, 0, 2 * gd, "gelu", F32, "in_proj_uv")
        gates = _proj_call(xb, w_ug, i, off_gate, 2 * d, "none", BF16, "in_proj_gate", bias=pr["b_gate"][i])
        dt, acum = _dt_call(xb, pr["w_dt"][i], pr["dt_bias"][i], pr["a_log_pad"][i], q_ssd)
        if lane_aligned:
            xbc, tails = _proj_conv_call(xb, w_main, i, off_xbc, pr["conv_w"][i], pr["conv_b"][i], conv_bufs[i], seq)
            ya, ssm_i = _ssd_cm_call(xbc, sz, dt, acum, ssm_states[i], pr["d_skip"][i], pr["ssd_norm_g"][i],
                                     bsz=bsz, seq=seq, q=q_ssd, gs=gs_ssd)
            new_rows = tails.reshape(bsz, -1, _CONV_PAD, cdim)[:, -1, _CONV_PAD - (kc - 1):]
        else:
            xbc = _proj_call(xb, w_main, i, off_xbc, cdim, "none", F32, "in_proj_xbc")
            ya, ssm_i = _ssd_call(xbc, sz, dt, conv_bufs[i], ssm_states[i], pr["conv_w"][i], pr["conv_b"][i],
                                  pr["a_log"][i], pr["d_skip"][i], pr["ssd_norm_g"][i], bsz=bsz, seq=seq, q=q_ssd)
            new_rows = xbc.reshape(bsz, seq, cdim)[:, max(seq - (kc - 1), 0):]
        conv_i = jnp.concatenate([conv_bufs[i], new_rows], axis=1)[:, -(kc - 1):]
        yb, *vn = _gmlp_call(uv, pr["gmlp_ln_g"][i], pr["gmlp_ln_b"][i], pr["w_s"][i], pr["b_s"][i],
                             bsz=bsz, seq=seq, q=q_gmlp, want_v=want_v)
        merged = _merge_call(ya, yb, gates, pr["w_br_a"], pr["w_br_b"], i)
        xf, xb = _mm_res_ln_call(merged, pr["w_o"], i, xf, pr["ln1_g"][i], pr["ln1_b"][i], alpha, "out_proj_ln")
        j = i // 2
        if i % 2 == 0:
            tm = _pick(t, 1024)
            meta = jnp.concatenate([jnp.full((1,), t // tm, I32), jnp.zeros((t // tm,), I32)])
            h = _swiglu_up_call(meta, xb, pr["w_ff_gate"], pr["w_ff_up"], j, tm, "ffn_up")
            xf, xb = _mm_res_ln_call(h, pr["w_ff_down"], j, xf, pr["ln2_g"][i], pr["ln2_b"][i], alpha,
                                     "ffn_down_ln")
        else:
            xf, xb = _moe_layer(xf, xb, pr["w_router"][j], pr["w_moe_gate"], pr["w_moe_up"], pr["w_moe_down"], j,
                                pr["ln2_g"][i], pr["ln2_b"][i], alpha)
        convs.append(conv_i)
        ssms.append(ssm_i)
        vs.extend(v.reshape(bsz, seq, gd) for v in vn)
    return xf.reshape(bsz, seq, d), jnp.stack(convs), jnp.stack(ssms), (jnp.stack(vs) if want_v else None)


def kernel(x_prompt, x_sample, cache_conv, state_ssm, ln_in_g, ln_in_b, w_in, conv_w, conv_b, dt_bias, a_log,
           d_skip, ssd_norm_g, gmlp_ln_g, gmlp_ln_b, w_s, b_s, b_gate, w_br_a, w_br_b, w_o, ln1_g, ln1_b,
           w_ff_gate, w_ff_up, w_ff_down, w_router, w_moe_gate, w_moe_up, w_moe_down, ln2_g, ln2_b):
    depth, d_model, _ = w_in.shape
    d_inner = w_br_a.shape[1]
    cdim = conv_w.shape[2]
    heads = a_log.shape[1]
    o_dt = d_inner + cdim
    w_in16 = w_in.astype(BF16)
    w_ug = w_in16[:, :, o_dt + heads:]
    w_dt = jnp.zeros((depth, d_model, LANE), BF16).at[:, :, :heads].set(w_in16[:, :, o_dt:o_dt + heads])
    dt_b = jnp.zeros((depth, 1, LANE), F32).at[:, 0, :heads].set(dt_bias)
    al_pad = jnp.zeros((depth, 1, LANE), F32).at[:, 0, :heads].set(a_log)
    pr = dict(a_log_pad=al_pad,
        ln_in_g=ln_in_g, ln_in_b=ln_in_b, w_zx=w_in16, w_ug=w_ug, w_dt=w_dt, dt_bias=dt_b, conv_w=conv_w, conv_b=conv_b,
        a_log=a_log, d_skip=d_skip, ssd_norm_g=ssd_norm_g, gmlp_ln_g=gmlp_ln_g, gmlp_ln_b=gmlp_ln_b, w_s=w_s,
        b_s=b_s, b_gate=b_gate, w_br_a=w_br_a.astype(BF16), w_br_b=w_br_b.astype(BF16), w_o=w_o.astype(BF16),
        ln1_g=ln1_g, ln1_b=ln1_b, w_ff_gate=w_ff_gate.astype(BF16)[:, None], w_ff_up=w_ff_up.astype(BF16)[:, None],
        w_ff_down=w_ff_down.astype(BF16), w_router=w_router, w_moe_gate=w_moe_gate.astype(BF16),
        w_moe_up=w_moe_up.astype(BF16), w_moe_down=w_moe_down.astype(BF16), ln2_g=ln2_g, ln2_b=ln2_b)
    bp = x_prompt.shape[0]
    zero_conv = jnp.zeros((depth, bp) + cache_conv.shape[2:], x_prompt.dtype)
    zero_ssm = jnp.zeros((depth, bp) + state_ssm.shape[2:], state_ssm.dtype)
    y_prompt, prompt_conv, prompt_ssm, _ = _trunk(x_prompt, zero_conv, zero_ssm, pr, want_v=False)
    y_sample, sample_conv, sample_ssm, sample_v = _trunk(x_sample, cache_conv, state_ssm, pr, want_v=True)
    return (y_prompt, y_sample, prompt_conv, prompt_ssm, sample_conv, sample_ssm, sample_v)
```
